```python
import math
import jax
import jax.numpy as jnp
from jax import lax
import numpy as np

D_MODEL = 1024
BATCH = 8
SEQ = 2048
DEPTH = 2
DEC_BATCH = 128
DEC_SEQ = 1
PAST_LEN = 2048
PAGE_SIZE = 128

N_HEADS = 16
HEAD_DIM = D_MODEL // N_HEADS
MOBA_BLOCK = 256
MOBA_TOPK = 3
QUERY_CHUNK = 64
CONV_WIDTH = 31
D_FF = 2816
N_EXPERTS = 8
MOE_TOPK = 2
D_FF_EXPERT = 3584
EXPERT_GROUP = 128
N_ATTN_LAYERS = (DEPTH + 1) // 2
N_CONV_LAYERS = DEPTH // 2
LN_EPS = 1e-5
DEEPNORM_ALPHA = (2 * DEPTH) ** 0.25
DEEPNORM_BETA = (8 * DEPTH) ** -0.25

kernel_name = 'moba_conformer_moe_decoder_step'


def _layernorm(x, g, b):
    xf = x.astype(jnp.float32)
    mu = jnp.mean(xf, axis=-1, keepdims=True)
    var = jnp.mean(jnp.square(xf - mu), axis=-1, keepdims=True)
    y = (xf - mu) * lax.rsqrt(var + LN_EPS) * g.astype(jnp.float32) + b.astype(jnp.float32)
    return y.astype(x.dtype)


def _adaln(c, w, b):
    m = (jax.nn.silu(c) @ w + b).reshape(c.shape[0], 6, 1, D_MODEL)
    return [m[:, k] for k in range(6)]


def _modulate(x, shift, scale):
    return x * (1 + scale) + shift


def _alibi_slopes():
    return 2.0 ** (-8.0 * jnp.arange(1, N_HEADS + 1, dtype=jnp.float32) / N_HEADS)


def _qkv(h, w_qkv):
    n, t, _ = h.shape
    qkv = (h @ w_qkv).reshape(n, t, 3, N_HEADS, HEAD_DIM)
    return qkv[:, :, 0], qkv[:, :, 1], qkv[:, :, 2]


def _moba_core(q, qpos, k_seq, v_seq):
    nq = q.shape[0]
    seq_len = k_seq.shape[0]
    nb = -(-seq_len // MOBA_BLOCK)
    pad = nb * MOBA_BLOCK - seq_len
    kb = jnp.pad(k_seq, ((0, pad), (0, 0), (0, 0))).reshape(nb, MOBA_BLOCK, N_HEADS, HEAD_DIM).transpose(2, 0, 1, 3)
    vb = jnp.pad(v_seq, ((0, pad), (0, 0), (0, 0))).reshape(nb, MOBA_BLOCK, N_HEADS, HEAD_DIM).transpose(2, 0, 1, 3)
    kmean = jnp.mean(kb, axis=2, dtype=jnp.float32)
    own = qpos // MOBA_BLOCK
    gate = jnp.einsum('qhd,hnd->qhn', q.astype(jnp.float32), kmean)
    fully_past = jnp.arange(nb)[None, None, :] < own[:, None, None]
    gate = jnp.where(fully_past, gate, -jnp.inf)
    kk = min(MOBA_TOPK, nb)
    _, top = lax.top_k(gate, kk)
    sel_ok = jnp.broadcast_to(jnp.arange(kk)[None, None, :] < own[:, None, None], (nq, N_HEADS, kk))
    blocks = jnp.concatenate([top, jnp.broadcast_to(own[:, None, None], (nq, N_HEADS, 1)).astype(top.dtype)], axis=-1)
    blk_ok = jnp.concatenate([sel_ok, jnp.ones((nq, N_HEADS, 1), dtype=bool)], axis=-1)
    hidx = jnp.arange(N_HEADS)[None, :, None]
    kg = kb[hidx, blocks]
    vg = vb[hidx, blocks]
    kpos = blocks[..., None] * MOBA_BLOCK + jnp.arange(MOBA_BLOCK)
    dist = qpos[:, None, None, None] - kpos
    ok = blk_ok[..., None] & (dist >= 0)
    s = jnp.einsum('qhd,qhjkd->qhjk', q, kg, preferred_element_type=jnp.float32) * (HEAD_DIM ** -0.5)
    s = s - _alibi_slopes()[None, :, None, None] * dist.astype(jnp.float32)
    s = jnp.where(ok, s, -jnp.inf)
    p = jax.nn.softmax(s.reshape(nq, N_HEADS, -1), axis=-1).reshape(s.shape)
    return jnp.einsum('qhjk,qhjkd->qhd', p.astype(vg.dtype), vg).astype(q.dtype)


def _moba_prompt(q, k, v):
    b, s = q.shape[:2]
    nc = s // QUERY_CHUNK
    q_items = q.reshape(b * nc, QUERY_CHUNK, N_HEADS, HEAD_DIM)
    pos_items = jnp.tile(jnp.arange(s, dtype=jnp.int32).reshape(nc, QUERY_CHUNK), (b, 1))
    b_items = jnp.repeat(jnp.arange(b, dtype=jnp.int32), nc)
    out = lax.map(lambda it: _moba_core(it[0], it[1], k[it[2]], v[it[2]]), (q_items, pos_items, b_items))
    return out.reshape(b, s, N_HEADS, HEAD_DIM)


def _moba_sample(q, k_new, v_new, cache_k, cache_v, layer, page_table):
    ds = q.shape[1]
    past_len = page_table.shape[1] * PAGE_SIZE
    qpos = past_len + jnp.arange(ds, dtype=jnp.int32)

    def one(it):
        qn, kn, vn, pt = it
        k_seq = jnp.concatenate([cache_k[layer, pt].reshape(past_len, N_HEADS, HEAD_DIM), kn.astype(cache_k.dtype)], axis=0)
        v_seq = jnp.concatenate([cache_v[layer, pt].reshape(past_len, N_HEADS, HEAD_DIM), vn.astype(cache_v.dtype)], axis=0)
        return _moba_core(qn, qpos, k_seq, v_seq)

    return lax.map(one, (q, k_new, v_new, page_table))


def _conv_module(h, hist, w_pw1, b_pw1, w_dw, b_dw, ln_gain, ln_bias, w_pw2, b_pw2):
    a, g = jnp.split(h @ w_pw1 + b_pw1, 2, axis=-1)
    u = a * jax.nn.sigmoid(g)
    up = jnp.concatenate([hist.astype(u.dtype), u], axis=1)
    y = lax.conv_general_dilated(up, w_dw[:, None, :].astype(u.dtype), (1,), 'VALID',
                                 dimension_numbers=('NWC', 'WIO', 'NWC'),
                                 feature_group_count=D_MODEL) + b_dw
    y = jax.nn.silu(_layernorm(y, ln_gain, ln_bias))
    return y @ w_pw2 + b_pw2, up[:, -(CONV_WIDTH - 1):]


def _swiglu(h, w1, w3, w2):
    return (jax.nn.silu(h @ w1) * (h @ w3)) @ w2


def _moe(h, w_router, b_router, w1, w3, w2):
    t, d = h.shape
    logits = (h @ w_router + b_router).astype(jnp.float32)
    top_l, top_e = lax.top_k(logits, MOE_TOPK)
    gates = jax.nn.softmax(top_l, axis=-1)
    n_assign = t * MOE_TOPK
    e_flat = top_e.reshape(-1)
    t_flat = jnp.repeat(jnp.arange(t, dtype=jnp.int32), MOE_TOPK)
    g_flat = gates.reshape(-1)
    order = jnp.argsort(e_flat)
    e_s, t_s, g_s = e_flat[order], t_flat[order], g_flat[order]
    counts = jnp.bincount(e_flat, length=N_EXPERTS)
    starts = jnp.cumsum(counts) - counts
    padded = (counts + EXPERT_GROUP - 1) // EXPERT_GROUP * EXPERT_GROUP
    pad_end = jnp.cumsum(padded)
    pad_start = pad_end - padded
    slot = pad_start[e_s] + jnp.arange(n_assign) - starts[e_s]
    n_grp = -(-n_assign // EXPERT_GROUP) + N_EXPERTS
    buf = jnp.zeros((n_grp * EXPERT_GROUP, d), h.dtype).at[slot].set(h[t_s])
    grp_e = jnp.minimum(jnp.searchsorted(pad_end, jnp.arange(n_grp) * EXPERT_GROUP, side='right'), N_EXPERTS - 1)

    def expert_group(args):
        xg, e = args
        return (jax.nn.silu(xg @ w1[e]) * (xg @ w3[e])) @ w2[e]

    out = lax.map(expert_group, (buf.reshape(n_grp, EXPERT_GROUP, d), grp_e)).reshape(-1, d)
    return jax.ops.segment_sum(out[slot] * g_s[:, None].astype(h.dtype), t_s, num_segments=t)


def setup_inputs(seed: int = 0) -> dict:
    key = jax.random.key(seed)
    ks = jax.random.split(key, 32)
    D = D_MODEL
    n_pages = PAST_LEN // PAGE_SIZE
    n_phys = (DEC_BATCH * n_pages * 5) // 4

    def nrm(k, shape, scale):
        return jax.random.normal(k, shape, jnp.float32) * scale

    page_table = jax.random.permutation(ks[5], n_phys)[: DEC_BATCH * n_pages].reshape(DEC_BATCH, n_pages).astype(jnp.int32)
    return {
        'x_prompt': nrm(ks[0], (BATCH, SEQ, D), 1.0),
        'x_sample': nrm(ks[1], (DEC_BATCH, DEC_SEQ, D), 1.0),
        'cache_k': nrm(ks[2], (N_ATTN_LAYERS, n_phys, PAGE_SIZE, N_HEADS, HEAD_DIM), 1.0),
        'cache_v': nrm(ks[3], (N_ATTN_LAYERS, n_phys, PAGE_SIZE, N_HEADS, HEAD_DIM), 1.0),
        'state_conv': nrm(ks[4], (N_CONV_LAYERS, DEC_BATCH, CONV_WIDTH - 1, D), 0.5),
        'page_table': page_table,
        'c_prompt': nrm(ks[6], (BATCH, D), 1.0),
        'c_sample': nrm(ks[7], (DEC_BATCH, D), 1.0),
        'ada_w': nrm(ks[8], (DEPTH, D, 6 * D), 0.5 * D ** -0.5),
        'ada_b': nrm(ks[9], (DEPTH, 6 * D), 0.02),
        'ln_g': 1.0 + nrm(ks[10], (DEPTH, 2, D), 0.02),
        'ln_b': nrm(ks[11], (DEPTH, 2, D), 0.02),
        'attn_w_qkv': nrm(ks[12], (N_ATTN_LAYERS, D, 3 * D), D ** -0.5),
        'attn_w_o': nrm(ks[13], (N_ATTN_LAYERS, D, D), DEEPNORM_BETA * D ** -0.5),
        'conv_w_pw1': nrm(ks[14], (N_CONV_LAYERS, D, 2 * D), D ** -0.5),
        'conv_b_pw1': nrm(ks[15], (N_CONV_LAYERS, 2 * D), 0.02),
        'conv_w_dw': nrm(ks[16], (N_CONV_LAYERS, CONV_WIDTH, D), CONV_WIDTH ** -0.5),
        'conv_b_dw': nrm(ks[17], (N_CONV_LAYERS, D), 0.02),
        'conv_ln_g': 1.0 + nrm(ks[18], (N_CONV_LAYERS, D), 0.02),
        'conv_ln_b': nrm(ks[19], (N_CONV_LAYERS, D), 0.02),
        'conv_w_pw2': nrm(ks[20], (N_CONV_LAYERS, D, D), DEEPNORM_BETA * D ** -0.5),
        'conv_b_pw2': nrm(ks[21], (N_CONV_LAYERS, D), 0.02),
        'ffn_w1': nrm(ks[22], (N_ATTN_LAYERS, D, D_FF), D ** -0.5),
        'ffn_w3': nrm(ks[23], (N_ATTN_LAYERS, D, D_FF), D ** -0.5),
        'ffn_w2': nrm(ks[24], (N_ATTN_LAYERS, D_FF, D), DEEPNORM_BETA * D_FF ** -0.5),
        'moe_w_router': nrm(ks[25], (N_CONV_LAYERS, D, N_EXPERTS), D ** -0.5),
        'moe_b_router': nrm(ks[26], (N_CONV_LAYERS, N_EXPERTS), 0.01),
        'moe_w1': nrm(ks[27], (N_CONV_LAYERS, N_EXPERTS, D, D_FF_EXPERT), D ** -0.5),
        'moe_w3': nrm(ks[28], (N_CONV_LAYERS, N_EXPERTS, D, D_FF_EXPERT), D ** -0.5),
        'moe_w2': nrm(ks[29], (N_CONV_LAYERS, N_EXPERTS, D_FF_EXPERT, D), DEEPNORM_BETA * D_FF_EXPERT ** -0.5),
    }


def reference(x_prompt, x_sample, cache_k, cache_v, state_conv, page_table, c_prompt, c_sample,
              ada_w, ada_b, ln_g, ln_b, attn_w_qkv, attn_w_o,
              conv_w_pw1, conv_b_pw1, conv_w_dw, conv_b_dw, conv_ln_g, conv_ln_b, conv_w_pw2, conv_b_pw2,
              ffn_w1, ffn_w3, ffn_w2, moe_w_router, moe_b_router, moe_w1, moe_w3, moe_w2):
    b, s, d = x_prompt.shape
    db, ds, _ = x_sample.shape
    yp, ys = x_prompt, x_sample
    kp_l, vp_l, ks_l, vs_l, cp_l, cs_l = [], [], [], [], [], []
    for i in range(DEPTH):
        j = i // 2
        shm_p, scm_p, gm_p, shf_p, scf_p, gf_p = _adaln(c_prompt, ada_w[i], ada_b[i])
        shm_s, scm_s, gm_s, shf_s, scf_s, gf_s = _adaln(c_sample, ada_w[i], ada_b[i])
        hp = _modulate(yp, shm_p, scm_p)
        hs = _modulate(ys, shm_s, scm_s)
        if i % 2 == 0:
            qp, kp, vp = _qkv(hp, attn_w_qkv[j])
            qs, kq, vq = _qkv(hs, attn_w_qkv[j])
            mp = _moba_prompt(qp, kp, vp).reshape(b, s, d) @ attn_w_o[j]
            ms = _moba_sample(qs, kq, vq, cache_k, cache_v, j, page_table).reshape(db, ds, d) @ attn_w_o[j]
            kp_l.append(kp)
            vp_l.append(vp)
            ks_l.append(kq)
            vs_l.append(vq)
        else:
            conv_args = (conv_w_pw1[j], conv_b_pw1[j], conv_w_dw[j], conv_b_dw[j],
                         conv_ln_g[j], conv_ln_b[j], conv_w_pw2[j], conv_b_pw2[j])
            mp, cp = _conv_module(hp, jnp.zeros((b, CONV_WIDTH - 1, d), hp.dtype), *conv_args)
            ms, cs = _conv_module(hs, state_conv[j], *conv_args)
            cp_l.append(cp)
            cs_l.append(cs)
        yp = _layernorm(DEEPNORM_ALPHA * yp + (1 + gm_p) * mp, ln_g[i, 0], ln_b[i, 0])
        ys = _layernorm(DEEPNORM_ALPHA * ys + (1 + gm_s) * ms, ln_g[i, 0], ln_b[i, 0])
        hp = _modulate(yp, shf_p, scf_p)
        hs = _modulate(ys, shf_s, scf_s)
        if i % 2 == 0:
            fp = _swiglu(hp, ffn_w1[j], ffn_w3[j], ffn_w2[j])
            fs = _swiglu(hs, ffn_w1[j], ffn_w3[j], ffn_w2[j])
        else:
            fp = _moe(hp.reshape(b * s, d), moe_w_router[j], moe_b_router[j], moe_w1[j], moe_w3[j], moe_w2[j]).reshape(b, s, d)
            fs = _moe(hs.reshape(db * ds, d), moe_w_router[j], moe_b_router[j], moe_w1[j], moe_w3[j], moe_w2[j]).reshape(db, ds, d)
        yp = _layernorm(DEEPNORM_ALPHA * yp + (1 + gf_p) * fp, ln_g[i, 1], ln_b[i, 1])
        ys = _layernorm(DEEPNORM_ALPHA * ys + (1 + gf_s) * fs, ln_g[i, 1], ln_b[i, 1])
    return (yp, ys, jnp.stack(kp_l), jnp.stack(vp_l), jnp.stack(ks_l), jnp.stack(vs_l), jnp.stack(cp_l), jnp.stack(cs_l))
```

```python
import functools

import jax
import jax.numpy as jnp
from jax import lax
from jax.experimental import pallas as pl
from jax.experimental.pallas import tpu as pltpu

F32 = jnp.float32
BF16 = jnp.bfloat16
HIGHEST = lax.Precision.HIGHEST
NEG_INF = float("-inf")

SUBLANES = 8
LANES = 128

D_MODEL = 1024
N_HEADS = 16
HEAD_DIM = D_MODEL // N_HEADS
HEADS_PER_STEP = LANES // HEAD_DIM
MOBA_BLOCK = 256
MOBA_TOPK = 3
PAGE_SIZE = 128
CONV_WIDTH = 31
HIST_PAD = 32
N_EXPERTS = 8
MOE_TOPK = 2
LN_EPS = 1e-5

TM_QKV = 512
TM_POST = 512
TM_CONV = 256
TM_ROUTE = 384
TM_GMM = 512
TM_COMBINE = 256
TF_GMM = 512
FFN_CHUNK = 256
VMEM_LIMIT = 56 * 1024 * 1024

NT_DIMS = (((1,), (1,)), ((), ()))


def _dot(a, b, precision=None):
    return jnp.dot(a, b, preferred_element_type=F32, precision=precision)


def _dot_nt(a, b, precision=None):
    return lax.dot_general(a, b, NT_DIMS, preferred_element_type=F32, precision=precision)


def _silu(x):
    return x * jax.nn.sigmoid(x)


def _layernorm(z, g, b):
    mu = jnp.mean(z, axis=-1, keepdims=True)
    zc = z - mu
    var = jnp.mean(zc * zc, axis=-1, keepdims=True)
    return zc * lax.rsqrt(var + LN_EPS) * g + b


def _resident(shape):
    nd = len(shape)
    return pl.BlockSpec(shape, lambda *_: (0,) * nd, pipeline_mode=pl.Buffered(1))


def _params(n_grid_dims):
    return pltpu.CompilerParams(
        dimension_semantics=("arbitrary",) * n_grid_dims, vmem_limit_bytes=VMEM_LIMIT)


def _ada_kernel(c_ref, w_ref, b_ref, o_ref):
    c = c_ref[...]
    o_ref[...] = _dot(_silu(c).astype(BF16), w_ref[...].astype(BF16)) + b_ref[...]


def _adaln(c_all, ada_w, ada_b):
    depth = ada_w.shape[0]
    n = c_all.shape[0]
    d = D_MODEL
    return pl.pallas_call(
        _ada_kernel,
        grid=(depth, 6),
        in_specs=[
            pl.BlockSpec((n, d), lambda l, k: (0, 0)),
            pl.BlockSpec((None, d, d), lambda l, k: (l, 0, k)),
            pl.BlockSpec((None, None, 1, d), lambda l, k: (l, k, 0, 0)),
        ],
        out_specs=pl.BlockSpec((None, None, n, d), lambda l, k: (l, k, 0, 0)),
        out_shape=jax.ShapeDtypeStruct((depth, 6, n, d), F32),
        compiler_params=_params(2),
        name="adaln",
    )(c_all, ada_w, ada_b.reshape(depth, 6, 1, d))


class _Mods:
    def __init__(self, mods, n_sample):
        self.n_sample = n_sample
        self.m3 = mods
        self.m4 = mods.reshape(mods.shape[0], mods.shape[1], 1, D_MODEL)

    def prompt(self, lk, seq_of_step):
        ns = self.n_sample
        return self.m4, pl.BlockSpec(
            (None, None, 1, D_MODEL), lambda *g: (lk, ns + seq_of_step(*g), 0, 0))

    def sample(self, lk, rows, block_of_step):
        return self.m3, pl.BlockSpec((None, rows, D_MODEL), lambda *g: (lk, block_of_step(*g), 0))


def _qkv_kernel(x_ref, sh_ref, sc_ref, w_ref, q_ref, k_ref, v_ref):
    d = D_MODEL
    h = (x_ref[...] * (1.0 + sc_ref[...]) + sh_ref[...]).astype(BF16)
    q_ref[...] = _dot(h, w_ref[:, 0:d])
    k_ref[...] = _dot(h, w_ref[:, d:2 * d])
    v_ref[...] = _dot(h, w_ref[:, 2 * d:3 * d])


def _qkv(x, sh, sc, w_bf16, tm):
    t = x.shape[0]
    d = D_MODEL
    row = pl.BlockSpec((tm, d), lambda i: (i, 0))
    out = jax.ShapeDtypeStruct((t, d), F32)
    return pl.pallas_call(
        _qkv_kernel,
        grid=(t // tm,),
        in_specs=[row, sh[1], sc[1], _resident((d, 3 * d))],
        out_specs=[row, row, row],
        out_shape=[out, out, out],
        compiler_params=_params(1),
        name="qkv",
    )(x, sh[0], sc[0], w_bf16)


def _select_blocks(gate_t, n_valid):
    nb = gate_t.shape[0]
    blk = lax.broadcasted_iota(jnp.int32, gate_t.shape, 0)
    valid = blk < n_valid
    rows = []
    for n in range(nb):
        gn = gate_t[n:n + 1, :]
        beats = ((gate_t > gn) | ((gate_t == gn) & (blk < n))) & valid
        rows.append(jnp.sum(beats.astype(F32), axis=0, keepdims=True))
    cnt = jnp.concatenate(rows, axis=0)
    return (valid & (cnt < MOBA_TOPK)).astype(F32)


def _attn_kernel(q_ref, k_ref, v_ref, o_ref, km_ref):
    blk = MOBA_BLOCK
    nb = k_ref.shape[0] // blk
    hp = pl.program_id(1)
    own = pl.program_id(2)

    @pl.when(own == 0)
    def _():
        km_ref[...] = jnp.zeros_like(km_ref)
        for n in range(nb):
            km_ref[n:n + 1, :] = jnp.mean(k_ref[n * blk:(n + 1) * blk, :], axis=0, keepdims=True)

    ri = lax.broadcasted_iota(jnp.int32, (blk, blk), 0)
    ci = lax.broadcasted_iota(jnp.int32, (blk, blk), 1)
    delta = (ri - ci).astype(F32)
    causal = ri >= ci
    eye = (ri == ci).astype(BF16)
    lane = lax.broadcasted_iota(jnp.int32, (blk, LANES), 1)
    own_start = pl.multiple_of(own * blk, blk)

    outs = []
    for hh in range(HEADS_PER_STEP):
        lo = hh * HEAD_DIM
        head_no = (hp * HEADS_PER_STEP + hh + 1).astype(F32)
        slope = jnp.exp2(jnp.full((1, 1), -8.0 / N_HEADS, F32) * head_no)
        qh = q_ref[:, lo:lo + HEAD_DIM]
        km = km_ref[:, lo:lo + HEAD_DIM]
        gate_t = _dot_nt(km, qh, precision=HIGHEST)[0:nb, :]
        sel_t = _select_blocks(gate_t, own).astype(BF16)
        sel_t = jnp.concatenate([sel_t, jnp.zeros((LANES - nb, blk), BF16)], axis=0)
        sel = _dot_nt(eye, sel_t)
        blk_off = (own - lane).astype(F32) * float(blk)
        mask_bias = jnp.where(sel > 0.5, -slope * blk_off, NEG_INF)
        slope_delta = slope * delta

        qs = (qh * (HEAD_DIM ** -0.5)).astype(BF16)
        kd = k_ref[pl.ds(own_start, blk), lo:lo + HEAD_DIM].astype(BF16)
        vd = v_ref[pl.ds(own_start, blk), lo:lo + HEAD_DIM].astype(BF16)
        s = jnp.where(causal, _dot_nt(qs, kd) - slope_delta, NEG_INF)
        m = jnp.max(s, axis=1, keepdims=True)
        p = jnp.exp(s - m)
        l = jnp.sum(p, axis=1, keepdims=True)
        acc = _dot(p.astype(BF16), vd)

        def body(j, carry, lo=lo, qs=qs, mask_bias=mask_bias, slope_delta=slope_delta):
            m, l, acc = carry
            start = pl.multiple_of(j * blk, blk)
            kj = k_ref[pl.ds(start, blk), lo:lo + HEAD_DIM].astype(BF16)
            vj = v_ref[pl.ds(start, blk), lo:lo + HEAD_DIM].astype(BF16)
            mb = jnp.sum(jnp.where(lane == j, mask_bias, 0.0), axis=1, keepdims=True)
            s = _dot_nt(qs, kj) - slope_delta + mb
            m_new = jnp.maximum(m, jnp.max(s, axis=1, keepdims=True))
            a = jnp.exp(m - m_new)
            p = jnp.exp(s - m_new)
            l = a * l + jnp.sum(p, axis=1, keepdims=True)
            acc = a * acc + _dot(p.astype(BF16), vj)
            return m_new, l, acc

        m, l, acc = lax.fori_loop(0, own, body, (m, l, acc))
        outs.append(acc / l)
    o_ref[...] = jnp.concatenate(outs, axis=1).astype(o_ref.dtype)


def _moba_prompt(q, k, v):
    b, s, d = q.shape
    nqb = s // MOBA_BLOCK
    ncol = d // LANES
    qspec = pl.BlockSpec((None, MOBA_BLOCK, LANES), lambda bi, hp, qb: (bi, qb, hp))
    kvspec = pl.BlockSpec((None, s, LANES), lambda bi, hp, qb: (bi, 0, hp))
    return pl.pallas_call(
        _attn_kernel,
        grid=(b, ncol, nqb),
        in_specs=[qspec, kvspec, kvspec],
        out_specs=qspec,
        out_shape=jax.ShapeDtypeStruct((b, s, d), BF16),
        scratch_shapes=[pltpu.VMEM((LANES, LANES), F32)],
        compiler_params=_params(3),
        name="moba_prompt",
    )(q, k, v)


def _attn_sample_kernel(n_pages, pt_ref, q_ref, kn_ref, vn_ref, *refs):
    kp = refs[:n_pages]
    vp = refs[n_pages:2 * n_pages]
    o_ref = refs[2 * n_pages]
    s_ref = refs[2 * n_pages + 1]
    d = D_MODEL
    past = n_pages * PAGE_SIZE
    pages_per_blk = MOBA_BLOCK // PAGE_SIZE
    nb = past // MOBA_BLOCK

    q = q_ref[...]
    hrow = lax.broadcasted_iota(jnp.int32, (LANES, d), 0)
    dcol = lax.broadcasted_iota(jnp.int32, (LANES, d), 1)
    head_mask = (dcol // HEAD_DIM) == hrow
    expand = head_mask.astype(BF16)
    q_heads = jnp.where(head_mask, q, 0.0)
    q_scaled = (q_heads * (HEAD_DIM ** -0.5)).astype(BF16)
    lane = lax.broadcasted_iota(jnp.int32, (1, LANES), 1)
    slope = jnp.exp2((lane + 1).astype(F32) * (-8.0 / N_HEADS))

    means = []
    for n in range(nb):
        tot = jnp.zeros((1, d), F32)
        for p in range(pages_per_blk):
            tot = tot + jnp.sum(kp[n * pages_per_blk + p][...], axis=0, keepdims=True)
        means.append(tot * (1.0 / MOBA_BLOCK))
    kmean = jnp.concatenate(means, axis=0)
    gate_t = _dot_nt(kmean, q_heads, precision=HIGHEST)
    sel_t = _select_blocks(gate_t, nb)

    m = jnp.full((1, LANES), NEG_INF, F32)
    for p in range(n_pages):
        sp = _dot_nt(kp[p][...].astype(BF16), q_scaled)
        kpos = (lax.broadcasted_iota(jnp.int32, (PAGE_SIZE, 1), 0) + p * PAGE_SIZE).astype(F32)
        sp = sp - slope * (float(past) - kpos)
        n = p // pages_per_blk
        sp = jnp.where(sel_t[n:n + 1, :] > 0.5, sp, NEG_INF)
        s_ref[p * PAGE_SIZE:(p + 1) * PAGE_SIZE, :] = sp
        m = jnp.maximum(m, jnp.max(sp, axis=0, keepdims=True))
    kn = jnp.broadcast_to(kn_ref[...], (SUBLANES, d)).astype(BF16)
    s_own = _dot_nt(kn, q_scaled)[0:1, :]
    m = jnp.maximum(m, s_own)
    p_own = jnp.exp(s_own - m)
    l = p_own
    acc = jnp.zeros((1, d), F32)
    for p in range(n_pages):
        pp = jnp.exp(s_ref[p * PAGE_SIZE:(p + 1) * PAGE_SIZE, :] - m)
        l = l + jnp.sum(pp, axis=0, keepdims=True)
        pe = _dot(pp.astype(BF16), expand)
        acc = acc + jnp.sum(pe * vp[p][...], axis=0, keepdims=True)
    rows = jnp.concatenate([p_own, l, jnp.zeros((SUBLANES - 2, LANES), F32)], axis=0)
    ex = _dot(rows, head_mask.astype(F32), precision=HIGHEST)
    o_ref[...] = ((acc + ex[0:1, :] * vn_ref[...]) / ex[1:2, :]).astype(o_ref.dtype)


def _moba_sample(q, k_new, v_new, cache_k, cache_v, layer, page_table):
    n_seq, n_pages = page_table.shape
    d = D_MODEL
    n_phys = cache_k.shape[1]
    ck = cache_k.reshape(cache_k.shape[0] * n_phys, PAGE_SIZE, d)
    cv = cache_v.reshape(cache_v.shape[0] * n_phys, PAGE_SIZE, d)
    pt = (page_table + layer * n_phys).reshape(-1).astype(jnp.int32)
    row = pl.BlockSpec((None, 1, d), lambda i, pt: (i, 0, 0))

    def page_spec(p):
        return pl.BlockSpec((None, PAGE_SIZE, d), lambda i, pt: (pt[i * n_pages + p], 0, 0))

    grid_spec = pltpu.PrefetchScalarGridSpec(
        num_scalar_prefetch=1,
        grid=(n_seq,),
        in_specs=[row, row, row] + [page_spec(p) for p in range(n_pages)] * 2,
        out_specs=row,
        scratch_shapes=[pltpu.VMEM((n_pages * PAGE_SIZE, LANES), F32)],
    )
    out = pl.pallas_call(
        functools.partial(_attn_sample_kernel, n_pages),
        grid_spec=grid_spec,
        out_shape=jax.ShapeDtypeStruct((n_seq, 1, d), BF16),
        compiler_params=_params(1),
        name="moba_sample",
    )(pt, q.reshape(n_seq, 1, d), k_new.reshape(n_seq, 1, d), v_new.reshape(n_seq, 1, d),
      *([ck] * n_pages), *([cv] * n_pages))
    return out.reshape(n_seq, d)


def _post_kernel(alpha, a_ref, x_ref, gm_ref, shf_ref, scf_ref, gf_ref, lng_ref, lnb_ref,
                 wo_ref, w1_ref, w3_ref, w2_ref, o_ref, acc_ref):
    m = _dot(a_ref[...], wo_ref[...])
    y1 = _layernorm(alpha * x_ref[...] + (1.0 + gm_ref[...]) * m, lng_ref[0:1, :], lnb_ref[0:1, :])
    h = (y1 * (1.0 + scf_ref[...]) + shf_ref[...]).astype(BF16)
    acc_ref[...] = jnp.zeros_like(acc_ref)

    def body(c, carry):
        t1 = _dot(h, w1_ref[c])
        t3 = _dot(h, w3_ref[c])
        acc_ref[...] += _dot((_silu(t1) * t3).astype(BF16), w2_ref[c])
        return carry

    lax.fori_loop(0, w1_ref.shape[0], body, 0)
    o_ref[...] = _layernorm(alpha * y1 + (1.0 + gf_ref[...]) * acc_ref[...],
                            lng_ref[1:2, :], lnb_ref[1:2, :])


def _post_attn(attn, x, gm, shf, scf, gf, ln_g, ln_b, wo, w1c, w3c, w2c, tm, alpha):
    t = x.shape[0]
    d = D_MODEL
    row = pl.BlockSpec((tm, d), lambda i: (i, 0))
    return pl.pallas_call(
        functools.partial(_post_kernel, alpha),
        grid=(t // tm,),
        in_specs=[row, row, gm[1], shf[1], scf[1], gf[1], _resident((2, d)), _resident((2, d)),
                  _resident(wo.shape), _resident(w1c.shape), _resident(w3c.shape), _resident(w2c.shape)],
        out_specs=row,
        out_shape=jax.ShapeDtypeStruct((t, d), F32),
        scratch_shapes=[pltpu.VMEM((tm, d), F32)],
        compiler_params=_params(1),
        name="post_attn_ffn",
    )(attn, x, gm[0], shf[0], scf[0], gf[0], ln_g, ln_b, wo, w1c, w3c, w2c)


def _glu(y, shm_ref, scm_ref, wpw1_ref, bpw1_ref):
    d = D_MODEL
    h = (y * (1.0 + scm_ref[...]) + shm_ref[...]).astype(BF16)
    a = _dot(h, wpw1_ref[:, 0:d]) + bpw1_ref[:, 0:d]
    g = _dot(h, wpw1_ref[:, d:2 * d]) + bpw1_ref[:, d:2 * d]
    return a * jax.nn.sigmoid(g)


def _conv_tail(alpha, y, conv, gm_ref, shf_ref, scf_ref, lng_ref, lnb_ref, clg_ref, clb_ref,
               wpw2_ref, bpw2_ref, wr_ref, br_ref, y3_ref, h2_ref, lg_ref):
    z = _silu(_layernorm(conv, clg_ref[...], clb_ref[...])).astype(BF16)
    m = _dot(z, wpw2_ref[...]) + bpw2_ref[...]
    y3 = _layernorm(alpha * y + (1.0 + gm_ref[...]) * m, lng_ref[0:1, :], lnb_ref[0:1, :])
    y3_ref[...] = y3
    h2 = y3 * (1.0 + scf_ref[...]) + shf_ref[...]
    h2_ref[...] = h2
    lg_ref[...] = _dot(h2, wr_ref[...], precision=HIGHEST) + br_ref[...]


def _conv_prompt_kernel(alpha, y_ref, shm_ref, scm_ref, gm_ref, shf_ref, scf_ref, lng_ref, lnb_ref,
                        wpw1_ref, bpw1_ref, wdw_ref, bdw_ref, clg_ref, clb_ref, wpw2_ref, bpw2_ref,
                        wr_ref, br_ref, y3_ref, h2_ref, lg_ref, tail_ref, ext_ref, shift_ref, conv_ref):
    tm = y_ref.shape[0]
    d = D_MODEL
    rows_ext = tm + HIST_PAD

    @pl.when(pl.program_id(1) == 0)
    def _():
        ext_ref[0:HIST_PAD, :] = jnp.zeros((HIST_PAD, d), F32)

    y = y_ref[...]
    u = _glu(y, shm_ref, scm_ref, wpw1_ref, bpw1_ref)
    ext_ref[HIST_PAD:rows_ext, :] = u
    tail_ref[...] = u[tm - HIST_PAD:tm, :]
    span = rows_ext - SUBLANES
    for b in range(1, SUBLANES):
        shift_ref[b - 1, :, :] = ext_ref[b:b + span, :]
    base = HIST_PAD - (CONV_WIDTH - 1)
    rb = 4 * SUBLANES

    def body(r, carry):
        r0 = pl.multiple_of(r * rb, rb)
        acc = jnp.broadcast_to(bdw_ref[...], (rb, d))
        for k in range(CONV_WIDTH):
            off = base + k
            start = r0 + (off // SUBLANES) * SUBLANES
            if off % SUBLANES == 0:
                x = ext_ref[pl.ds(start, rb), :]
            else:
                x = shift_ref[off % SUBLANES - 1, pl.ds(start, rb), :]
            acc = acc + wdw_ref[k:k + 1, :] * x
        conv_ref[pl.ds(r0, rb), :] = acc
        return carry

    lax.fori_loop(0, tm // rb, body, 0)
    ext_ref[0:HIST_PAD, :] = u[tm - HIST_PAD:tm, :]
    _conv_tail(alpha, y, conv_ref[...], gm_ref, shf_ref, scf_ref, lng_ref, lnb_ref, clg_ref, clb_ref,
               wpw2_ref, bpw2_ref, wr_ref, br_ref, y3_ref, h2_ref, lg_ref)


def _conv_sample_kernel(alpha, y_ref, shm_ref, scm_ref, gm_ref, shf_ref, scf_ref, lng_ref, lnb_ref,
                        wpw1_ref, bpw1_ref, wdw_ref, bdw_ref, clg_ref, clb_ref, wpw2_ref, bpw2_ref,
                        wr_ref, br_ref, hist_ref, h2_in_ref, lg_in_ref, y3_ref, h2_ref, lg_ref, u_ref):
    del h2_in_ref, lg_in_ref
    nh = CONV_WIDTH - 1
    y = y_ref[...]
    u = _glu(y, shm_ref, scm_ref, wpw1_ref, bpw1_ref)
    u_ref[...] = u
    w = wdw_ref[...]
    conv = bdw_ref[...] + w[nh:nh + 1, :] * u + jnp.sum(hist_ref[...] * w[0:nh, :][None], axis=1)
    _conv_tail(alpha, y, conv, gm_ref, shf_ref, scf_ref, lng_ref, lnb_ref, clg_ref, clb_ref,
               wpw2_ref, bpw2_ref, wr_ref, br_ref, y3_ref, h2_ref, lg_ref)


def _conv_weight_specs(d):
    return [_resident((2, d)), _resident((2, d)), _resident((d, 2 * d)), _resident((1, 2 * d)),
            _resident((CONV_WIDTH, d)), _resident((1, d)), _resident((1, d)), _resident((1, d)),
            _resident((d, d)), _resident((1, d)), _resident((d, LANES)), _resident((1, LANES))]


def _conv_prompt(y, mods5, weights, n_seq, seq_len, t_all, alpha):
    d = D_MODEL
    tm = TM_CONV
    tps = seq_len // tm
    row = pl.BlockSpec((tm, d), lambda b, i: (b * tps + i, 0))
    lgrow = pl.BlockSpec((tm, LANES), lambda b, i: (b * tps + i, 0))
    tail = pl.BlockSpec((None, HIST_PAD, d), lambda b, i: (b, 0, 0))
    return pl.pallas_call(
        functools.partial(_conv_prompt_kernel, alpha),
        grid=(n_seq, tps),
        in_specs=[row] + [m[1] for m in mods5] + _conv_weight_specs(d),
        out_specs=[row, row, lgrow, tail],
        out_shape=[jax.ShapeDtypeStruct((n_seq * seq_len, d), F32),
                   jax.ShapeDtypeStruct((t_all, d), F32),
                   jax.ShapeDtypeStruct((t_all, LANES), F32),
                   jax.ShapeDtypeStruct((n_seq, HIST_PAD, d), F32)],
        scratch_shapes=[pltpu.VMEM((tm + HIST_PAD, d), F32),
                        pltpu.VMEM((SUBLANES - 1, tm + HIST_PAD - SUBLANES, d), F32),
                        pltpu.VMEM((tm, d), F32)],
        compiler_params=_params(2),
        name="conv_prompt",
    )(y, *[m[0] for m in mods5], *weights)


def _conv_sample(y, mods5, weights, hist, h2_all, lg_all, row0, alpha):
    d = D_MODEL
    n = y.shape[0]
    tm = 32
    blk0 = row0 // tm
    row = pl.BlockSpec((tm, d), lambda i: (i, 0))
    any_spec = pl.BlockSpec(memory_space=pl.ANY)
    return pl.pallas_call(
        functools.partial(_conv_sample_kernel, alpha),
        grid=(n // tm,),
        in_specs=[row] + [m[1] for m in mods5] + _conv_weight_specs(d)
        + [pl.BlockSpec((tm, CONV_WIDTH - 1, d), lambda i: (i, 0, 0)), any_spec, any_spec],
        out_specs=[row, pl.BlockSpec((tm, d), lambda i: (blk0 + i, 0)),
                   pl.BlockSpec((tm, LANES), lambda i: (blk0 + i, 0)), row],
        out_shape=[jax.ShapeDtypeStruct((n, d), F32),
                   jax.ShapeDtypeStruct(h2_all.shape, F32),
                   jax.ShapeDtypeStruct(lg_all.shape, F32),
                   jax.ShapeDtypeStruct((n, d), F32)],
        input_output_aliases={len(mods5) + 14: 1, len(mods5) + 15: 2},
        compiler_params=_params(1),
        name="conv_sample",
    )(y, *[m[0] for m in mods5], *weights, hist, h2_all, lg_all)


def _route_kernel(cap, lg_ref, meta_ref, meta_t_ref, cnt_ref, carry_ref):
    tm = lg_ref.shape[0]

    @pl.when(pl.program_id(0) == 0)
    def _():
        carry_ref[...] = jnp.zeros_like(carry_ref)

    lane = lax.broadcasted_iota(jnp.int32, (tm, LANES), 1)
    lg = jnp.where(lane < N_EXPERTS, lg_ref[...], NEG_INF)
    m1 = jnp.max(lg, axis=1, keepdims=True)
    i1 = jnp.min(jnp.where(lg == m1, lane, LANES), axis=1, keepdims=True)
    lg2 = jnp.where(lane == i1, NEG_INF, lg)
    m2 = jnp.max(lg2, axis=1, keepdims=True)
    i2 = jnp.min(jnp.where(lg2 == m2, lane, LANES), axis=1, keepdims=True)
    e = jnp.exp(m2 - m1)
    g1 = 1.0 / (1.0 + e)
    g2 = e / (1.0 + e)
    hot1 = lane == i1
    hot2 = lane == i2
    onehot = (hot1 | hot2).astype(BF16)
    ri = lax.broadcasted_iota(jnp.int32, (tm, tm), 0)
    ci = lax.broadcasted_iota(jnp.int32, (tm, tm), 1)
    before = (ci < ri).astype(BF16)
    rank = _dot(before, onehot) + carry_ref[...]
    r1 = jnp.sum(jnp.where(hot1, rank, 0.0), axis=1, keepdims=True)
    r2 = jnp.sum(jnp.where(hot2, rank, 0.0), axis=1, keepdims=True)
    slot1 = i1.astype(F32) * float(cap) + r1
    slot2 = i2.astype(F32) * float(cap) + r2
    meta = jnp.where(lane == 0, slot1, jnp.where(lane == 1, slot2,
                     jnp.where(lane == 2, g1, jnp.where(lane == 3, g2, 0.0))))
    meta_ref[...] = meta
    meta_t_ref[...] = jnp.transpose(meta)[0:SUBLANES, :]
    carry_ref[...] += jnp.sum(onehot.astype(F32), axis=0, keepdims=True)
    cnt_ref[...] = carry_ref[...]


def _route(lg_all, cap):
    t = lg_all.shape[0]
    tm = TM_ROUTE
    return pl.pallas_call(
        functools.partial(_route_kernel, cap),
        grid=(t // tm,),
        in_specs=[pl.BlockSpec((tm, LANES), lambda i: (i, 0))],
        out_specs=[pl.BlockSpec((tm, LANES), lambda i: (i, 0)),
                   pl.BlockSpec((SUBLANES, tm), lambda i: (0, i)),
                   pl.BlockSpec((1, LANES), lambda i: (0, 0))],
        out_shape=[jax.ShapeDtypeStruct((t, LANES), F32),
                   jax.ShapeDtypeStruct((SUBLANES, t), F32),
                   jax.ShapeDtypeStruct((1, LANES), F32)],
        scratch_shapes=[pltpu.VMEM((1, LANES), F32)],
        compiler_params=_params(1),
        name="moe_route",
    )(lg_all)


def _scatter_kernel(cap, tg, cnt_ref, slots_ref, h_ref, xs_ref, zero_ref, sem, zsem):
    tm = h_ref.shape[0]

    def issue(r, carry):
        for c in range(MOE_TOPK):
            dst = slots_ref[0, c * tm + r]
            pltpu.make_async_copy(h_ref.at[pl.ds(r, 1), :], xs_ref.at[pl.ds(dst, 1), :], sem).start()
        return carry

    lax.fori_loop(0, tm, issue, 0)
    for c in range(MOE_TOPK):
        pltpu.make_async_copy(h_ref, xs_ref.at[pl.ds(0, tm), :], sem).wait()

    @pl.when(pl.program_id(0) == pl.num_programs(0) - 1)
    def _():
        zero_ref[...] = jnp.zeros_like(zero_ref)
        for e in range(N_EXPERTS):
            n = cnt_ref[e]
            end = ((n + tg - 1) // tg) * tg

            def zero_row(r, carry, e=e):
                cp = pltpu.make_async_copy(zero_ref.at[pl.ds(0, 1), :],
                                           xs_ref.at[pl.ds(e * cap + r, 1), :], zsem)
                cp.start()
                cp.wait()
                return carry

            lax.fori_loop(n, end, zero_row, 0)


def _scatter(h2_all, slots3, counts, cap):
    t, d = h2_all.shape
    tm = TM_ROUTE
    grid_spec = pltpu.PrefetchScalarGridSpec(
        num_scalar_prefetch=1,
        grid=(t // tm,),
        in_specs=[pl.BlockSpec((None, 1, MOE_TOPK * tm), lambda i, c: (i, 0, 0), memory_space=pltpu.SMEM),
                  pl.BlockSpec((tm, d), lambda i, c: (i, 0))],
        out_specs=pl.BlockSpec(memory_space=pl.ANY),
        scratch_shapes=[pltpu.VMEM((SUBLANES, d), F32), pltpu.SemaphoreType.DMA(()),
                        pltpu.SemaphoreType.DMA(())],
    )
    return pl.pallas_call(
        functools.partial(_scatter_kernel, cap, TM_GMM),
        grid_spec=grid_spec,
        out_shape=jax.ShapeDtypeStruct((N_EXPERTS * cap, d), F32),
        compiler_params=pltpu.CompilerParams(dimension_semantics=("arbitrary",),
                                             vmem_limit_bytes=VMEM_LIMIT, has_side_effects=True),
        name="moe_scatter",
    )(counts, slots3, h2_all)


def _gmm_kernel(blk_ref, exp_ref, nv_ref, x_ref, w1_ref, w3_ref, w2_ref, o_ref, xb_ref, acc_ref):
    del blk_ref, exp_ref
    m = pl.program_id(0)
    f = pl.program_id(1)
    nf = pl.num_programs(1)

    @pl.when(m < nv_ref[0])
    def _():
        @pl.when(f == 0)
        def _():
            xb_ref[...] = x_ref[...].astype(BF16)

        x = xb_ref[...]
        t1 = _dot(x, w1_ref[...])
        t3 = _dot(x, w3_ref[...])
        part = _dot((_silu(t1) * t3).astype(BF16), w2_ref[...])

        @pl.when(f == 0)
        def _():
            acc_ref[...] = part

        @pl.when(f > 0)
        def _():
            acc_ref[...] += part

        @pl.when(f == nf - 1)
        def _():
            o_ref[...] = acc_ref[...]


def _gmm(xs, w1, w3, w2, tile_blk, tile_exp, n_valid):
    d = D_MODEL
    tg = TM_GMM
    tf = TF_GMM
    dfe = w1.shape[2]
    nf = dfe // tf
    max_tiles = tile_blk.shape[0]

    def fidx(m, f, nv):
        return jnp.where(m < nv[0], f, nf - 1)

    grid_spec = pltpu.PrefetchScalarGridSpec(
        num_scalar_prefetch=3,
        grid=(max_tiles, nf),
        in_specs=[
            pl.BlockSpec((tg, d), lambda m, f, blk, ex, nv: (blk[m], 0)),
            pl.BlockSpec((None, d, tf), lambda m, f, blk, ex, nv: (ex[m], 0, fidx(m, f, nv))),
            pl.BlockSpec((None, d, tf), lambda m, f, blk, ex, nv: (ex[m], 0, fidx(m, f, nv))),
            pl.BlockSpec((None, tf, d), lambda m, f, blk, ex, nv: (ex[m], fidx(m, f, nv), 0)),
        ],
        out_specs=pl.BlockSpec((tg, d), lambda m, f, blk, ex, nv: (blk[m], 0)),
        scratch_shapes=[pltpu.VMEM((tg, d), BF16), pltpu.VMEM((tg, d), F32)],
    )
    return pl.pallas_call(
        _gmm_kernel,
        grid_spec=grid_spec,
        out_shape=jax.ShapeDtypeStruct(xs.shape, F32),
        compiler_params=_params(2),
        name="moe_experts",
    )(tile_blk, tile_exp, n_valid, xs, w1, w3, w2)


def _combine_kernel(alpha, slots_ref, ys_ref, y_ref, meta_ref, gf_ref, lng_ref, lnb_ref, o_ref,
                    buf_ref, sem):
    tm = y_ref.shape[0]

    def issue(r, carry):
        for c in range(MOE_TOPK):
            src = slots_ref[0, c * tm + r]
            pltpu.make_async_copy(ys_ref.at[pl.ds(src, 1), :], buf_ref.at[c, pl.ds(r, 1), :], sem).start()
        return carry

    lax.fori_loop(0, tm, issue, 0)
    for c in range(MOE_TOPK):
        pltpu.make_async_copy(ys_ref.at[pl.ds(0, tm), :], buf_ref.at[c], sem).wait()
    meta = meta_ref[...]
    f = buf_ref[0] * meta[:, 2:3] + buf_ref[1] * meta[:, 3:4]
    o_ref[...] = _layernorm(alpha * y_ref[...] + (1.0 + gf_ref[...]) * f,
                            lng_ref[1:2, :], lnb_ref[1:2, :])


def _combine(ys, y3, slots3, meta, gf, ln_g, ln_b, tm, alpha):
    t, d = y3.shape
    grid_spec = pltpu.PrefetchScalarGridSpec(
        num_scalar_prefetch=0,
        grid=(t // tm,),
        in_specs=[pl.BlockSpec((None, 1, MOE_TOPK * tm), lambda i: (i, 0, 0), memory_space=pltpu.SMEM),
                  pl.BlockSpec(memory_space=pl.ANY),
                  pl.BlockSpec((tm, d), lambda i: (i, 0)),
                  pl.BlockSpec((tm, LANES), lambda i: (i, 0)),
                  gf[1], _resident((2, d)), _resident((2, d))],
        out_specs=pl.BlockSpec((tm, d), lambda i: (i, 0)),
        scratch_shapes=[pltpu.VMEM((MOE_TOPK, tm, d), F32), pltpu.SemaphoreType.DMA(())],
    )
    return pl.pallas_call(
        functools.partial(_combine_kernel, alpha),
        grid_spec=grid_spec,
        out_shape=jax.ShapeDtypeStruct((t, d), F32),
        compiler_params=_params(1),
        name="moe_combine",
    )(slots3, ys, y3, meta, gf[0], ln_g, ln_b)


def _tile_slots(slots, tm):
    k, t = slots.shape
    return slots.reshape(k, t // tm, tm).transpose(1, 0, 2).reshape(t // tm, 1, k * tm)


def kernel(x_prompt, x_sample, cache_k, cache_v, state_conv, page_table, c_prompt, c_sample,
           ada_w, ada_b, ln_g, ln_b, attn_w_qkv, attn_w_o,
           conv_w_pw1, conv_b_pw1, conv_w_dw, conv_b_dw, conv_ln_g, conv_ln_b, conv_w_pw2, conv_b_pw2,
           ffn_w1, ffn_w3, ffn_w2, moe_w_router, moe_b_router, moe_w1, moe_w3, moe_w2):
    b, s, d = x_prompt.shape
    db, ds, _ = x_sample.shape
    depth = ada_w.shape[0]
    assert d == D_MODEL and ds == 1 and depth == 2
    assert page_table.shape[1] * PAGE_SIZE % MOBA_BLOCK == 0
    alpha = (2 * depth) ** 0.25
    tp = b * s
    t_all = tp + db
    assert t_all % TM_ROUTE == 0 and tp % TM_COMBINE == 0

    c_all = jnp.concatenate([c_sample, c_prompt], axis=0)
    mods = _adaln(c_all, ada_w, ada_b)
    mods = _Mods(mods.reshape(depth * 6, db + b, d), db)

    def pmod(layer, k, tm):
        tps = s // tm
        return mods.prompt(layer * 6 + k, lambda i: i // tps)

    def smod(layer, k, rows=None, block_of_step=lambda i: 0):
        return mods.sample(layer * 6 + k, db if rows is None else rows, block_of_step)

    xp = x_prompt.reshape(tp, d)
    xs_ = x_sample.reshape(db, d)

    wqkv = attn_w_qkv[0].astype(BF16)
    qp, kp, vp = _qkv(xp, pmod(0, 0, TM_QKV), pmod(0, 1, TM_QKV), wqkv, TM_QKV)
    qs, ks, vs = _qkv(xs_, smod(0, 0), smod(0, 1), wqkv, db)
    attn_p = _moba_prompt(qp.reshape(b, s, d), kp.reshape(b, s, d), vp.reshape(b, s, d)).reshape(tp, d)
    attn_s = _moba_sample(qs, ks, vs, cache_k, cache_v, 0, page_table)

    dff = ffn_w1.shape[2]
    nck = dff // FFN_CHUNK
    wo = attn_w_o[0].astype(BF16)
    w1c = ffn_w1[0].astype(BF16).reshape(d, nck, FFN_CHUNK).transpose(1, 0, 2)
    w3c = ffn_w3[0].astype(BF16).reshape(d, nck, FFN_CHUNK).transpose(1, 0, 2)
    w2c = ffn_w2[0].astype(BF16).reshape(nck, FFN_CHUNK, d)
    y_p = _post_attn(attn_p, xp, pmod(0, 2, TM_POST), pmod(0, 3, TM_POST), pmod(0, 4, TM_POST),
                     pmod(0, 5, TM_POST), ln_g[0], ln_b[0], wo, w1c, w3c, w2c, TM_POST, alpha)
    y_s = _post_attn(attn_s, xs_, smod(0, 2), smod(0, 3), smod(0, 4), smod(0, 5),
                     ln_g[0], ln_b[0], wo, w1c, w3c, w2c, db, alpha)

    wr = jnp.pad(moe_w_router[0], ((0, 0), (0, LANES - N_EXPERTS)))
    br = jnp.pad(moe_b_router[0], (0, LANES - N_EXPERTS)).reshape(1, LANES)
    conv_weights = (ln_g[1], ln_b[1], conv_w_pw1[0].astype(BF16), conv_b_pw1[0].reshape(1, 2 * d),
                    conv_w_dw[0], conv_b_dw[0].reshape(1, d), conv_ln_g[0].reshape(1, d),
                    conv_ln_b[0].reshape(1, d), conv_w_pw2[0].astype(BF16), conv_b_pw2[0].reshape(1, d),
                    wr, br)
    tps_conv = s // TM_CONV
    pm = [mods.prompt(6 + k, lambda bi, i: bi) for k in (0, 1, 2, 3, 4)]
    y3_p, h2_all, lg_all, tail = _conv_prompt(y_p, pm, conv_weights, b, s, t_all, alpha)
    del tps_conv
    sm = [smod(1, k, 32, lambda i: i) for k in (0, 1, 2, 3, 4)]
    y3_s, h2_all, lg_all, u_s = _conv_sample(y_s, sm, conv_weights, state_conv[0], h2_all, lg_all, tp, alpha)

    cap = -(-t_all // TM_GMM) * TM_GMM
    meta, meta_t, cnt = _route(lg_all, cap)
    slots = meta_t[0:MOE_TOPK].astype(jnp.int32)
    counts = cnt[0, 0:N_EXPERTS].astype(jnp.int32)
    xs_buf = _scatter(h2_all, _tile_slots(slots, TM_ROUTE), counts, cap)

    tiles_e = (counts + TM_GMM - 1) // TM_GMM
    tile_end = jnp.cumsum(tiles_e)
    n_valid = tile_end[-1]
    max_tiles = -(-t_all * MOE_TOPK // TM_GMM) + N_EXPERTS
    mi = jnp.minimum(jnp.arange(max_tiles, dtype=jnp.int32), n_valid - 1)
    tile_exp = jnp.minimum(jnp.searchsorted(tile_end, mi, side="right"), N_EXPERTS - 1).astype(jnp.int32)
    tile_blk = (tile_exp * (cap // TM_GMM) + mi - (tile_end - tiles_e)[tile_exp]).astype(jnp.int32)
    ys_buf = _gmm(xs_buf, moe_w1[0].astype(BF16), moe_w3[0].astype(BF16), moe_w2[0].astype(BF16),
                  tile_blk, tile_exp, n_valid.reshape(1).astype(jnp.int32))

    out_p = _combine(ys_buf, y3_p, _tile_slots(slots[:, :tp], TM_COMBINE), meta,
                     pmod(1, 5, TM_COMBINE), ln_g[1], ln_b[1], TM_COMBINE, alpha)
    out_s = _combine(ys_buf, y3_s, _tile_slots(slots[:, tp:], db), meta[tp:],
                     smod(1, 5), ln_g[1], ln_b[1], db, alpha)

    nh = CONV_WIDTH - 1
    new_conv_p = tail[:, HIST_PAD - nh:, :][None]
    new_conv_s = jnp.concatenate([state_conv[0][:, 1:, :], u_s[:, None, :]], axis=1)[None]
    hshape = (N_HEADS, HEAD_DIM)
    return (out_p.reshape(b, s, d), out_s.reshape(db, 1, d),
            kp.reshape(1, b, s, *hshape), vp.reshape(1, b, s, *hshape),
            ks.reshape(1, db, 1, *hshape), vs.reshape(1, db, 1, *hshape),
            new_conv_p, new_conv_s)
```

```python
import functools
import math

import jax
import jax.numpy as jnp
from jax import lax
from jax.experimental import pallas as pl
from jax.experimental.pallas import tpu as pltpu

F32 = jnp.float32
BF16 = jnp.bfloat16
HIGHEST = lax.Precision.HIGHEST
NEG_INF = float("-inf")
LOG2E = math.log2(math.e)

SUBLANES = 8
LANES = 128

D_MODEL = 1024
N_HEADS = 16
HEAD_DIM = D_MODEL // N_HEADS
HEADS_PER_STEP = 8
MOBA_BLOCK = 256
MOBA_TOPK = 3
PAGE_SIZE = 128
CONV_WIDTH = 31
HIST_PAD = 32
N_EXPERTS = 8
MOE_TOPK = 2
LN_EPS = 1e-5
ROW_TILES = D_MODEL // LANES

TM_QKV = 512
TM_POST = 512
TM_CONV = 256
TM_ROUTE = 512
TM_GMM = 512
TM_COMBINE = 256
TF_GMM = 512
FFN_CHUNK = 256
ZERO_ROWS = 64
VMEM_LIMIT = 56 * 1024 * 1024

META_EXPERT, META_GATE, META_RANK = 0, 2, 4

NT_DIMS = (((1,), (1,)), ((), ()))


def _dot(a, b, precision=None):
    return jnp.dot(a, b, preferred_element_type=F32, precision=precision)


def _dot_nt(a, b, precision=None):
    return lax.dot_general(a, b, NT_DIMS, preferred_element_type=F32, precision=precision)


def _silu(x):
    return x * jax.nn.sigmoid(x)


def _layernorm(z, g, b):
    mu = jnp.mean(z, axis=-1, keepdims=True)
    zc = z - mu
    var = jnp.mean(zc * zc, axis=-1, keepdims=True)
    return zc * lax.rsqrt(var + LN_EPS) * g + b


def _resident(shape):
    nd = len(shape)
    return pl.BlockSpec(shape, lambda *_: (0,) * nd, pipeline_mode=pl.Buffered(1))


def _params(n_grid_dims):
    return pltpu.CompilerParams(
        dimension_semantics=("arbitrary",) * n_grid_dims, vmem_limit_bytes=VMEM_LIMIT)


def _rows_to_tiles(ref, x):
    t = x.shape[0]
    for s in range(ROW_TILES):
        ref[pl.ds(s, t, stride=ROW_TILES), :] = x[:, s * LANES:(s + 1) * LANES]


def _tiles_to_rows(ref, t):
    return jnp.concatenate([ref[pl.ds(s, t, stride=ROW_TILES), :] for s in range(ROW_TILES)], axis=1)


def _ada_kernel(c_ref, w_ref, b_ref, o_ref):
    c = c_ref[...]
    o_ref[...] = _dot(_silu(c).astype(BF16), w_ref[...].astype(BF16)) + b_ref[...]


def _adaln(c_all, ada_w, ada_b):
    depth = ada_w.shape[0]
    n = c_all.shape[0]
    d = D_MODEL
    return pl.pallas_call(
        _ada_kernel,
        grid=(depth, 6),
        in_specs=[
            pl.BlockSpec((n, d), lambda l, k: (0, 0)),
            pl.BlockSpec((None, d, d), lambda l, k: (l, 0, k)),
            pl.BlockSpec((None, None, 1, d), lambda l, k: (l, k, 0, 0)),
        ],
        out_specs=pl.BlockSpec((None, None, n, d), lambda l, k: (l, k, 0, 0)),
        out_shape=jax.ShapeDtypeStruct((depth, 6, n, d), F32),
        compiler_params=_params(2),
        name="adaln",
    )(c_all, ada_w, ada_b.reshape(depth, 6, 1, d))


class _Mods:
    def __init__(self, mods, n_sample):
        self.n_sample = n_sample
        self.m3 = mods
        self.m4 = mods.reshape(mods.shape[0], mods.shape[1], 1, D_MODEL)

    def prompt(self, lk, seq_of_step):
        ns = self.n_sample
        return self.m4, pl.BlockSpec(
            (None, None, 1, D_MODEL), lambda *g: (lk, ns + seq_of_step(*g), 0, 0))

    def sample(self, lk, rows, block_of_step):
        return self.m3, pl.BlockSpec((None, rows, D_MODEL), lambda *g: (lk, block_of_step(*g), 0))


def _qkv_kernel(x_ref, sh_ref, sc_ref, w_ref, q_ref, k_ref, v_ref):
    d = D_MODEL
    h = (x_ref[...] * (1.0 + sc_ref[...]) + sh_ref[...]).astype(BF16)
    q_ref[...] = _dot(h, w_ref[:, 0:d])
    k_ref[...] = _dot(h, w_ref[:, d:2 * d])
    v_ref[...] = _dot(h, w_ref[:, 2 * d:3 * d])


def _qkv(x, sh, sc, w_bf16, tm):
    t = x.shape[0]
    d = D_MODEL
    row = pl.BlockSpec((tm, d), lambda i: (i, 0))
    out = jax.ShapeDtypeStruct((t, d), F32)
    return pl.pallas_call(
        _qkv_kernel,
        grid=(t // tm,),
        in_specs=[row, sh[1], sc[1], _resident((d, 3 * d))],
        out_specs=[row, row, row],
        out_shape=[out, out, out],
        compiler_params=_params(1),
        name="qkv",
    )(x, sh[0], sc[0], w_bf16)


def _select_blocks(gate_t, n_valid):
    nb = gate_t.shape[0]
    blk = lax.broadcasted_iota(jnp.int32, gate_t.shape, 0)
    valid = blk < n_valid
    rows = []
    for n in range(nb):
        gn = gate_t[n:n + 1, :]
        beats = ((gate_t > gn) | ((gate_t == gn) & (blk < n))) & valid
        rows.append(jnp.sum(beats.astype(F32), axis=0, keepdims=True))
    cnt = jnp.concatenate(rows, axis=0)
    return (valid & (cnt < MOBA_TOPK)).astype(F32)


def _attn_kernel(q_ref, k_ref, v_ref, bias_ref, o_ref, kb_ref, ve_ref, km_ref, mb_ref, acc_ref):
    blk = MOBA_BLOCK
    s_len = k_ref.shape[0]
    nb = s_len // blk
    nh = HEADS_PER_STEP
    cb = pl.program_id(1)
    own = pl.program_id(2)

    @pl.when(own == 0)
    def _():
        kb_ref[...] = k_ref[...].astype(BF16)
        lane_s = lax.broadcasted_iota(jnp.int32, (s_len, LANES), 1)
        lane_k = lax.broadcasted_iota(jnp.int32, (nb, LANES), 1) // HEAD_DIM
        for pp in range(nh // 2):
            cols = slice(pp * LANES, (pp + 1) * LANES)
            rows = [jnp.mean(k_ref[n * blk:(n + 1) * blk, cols], axis=0, keepdims=True) for n in range(nb)]
            km = jnp.concatenate(rows, axis=0)
            km_ref[pp] = jnp.concatenate([jnp.where(lane_k == 0, km, 0.0), jnp.where(lane_k == 1, km, 0.0)], axis=0)
            v = v_ref[:, cols]
            ve_ref[2 * pp] = jnp.where(lane_s < HEAD_DIM, v, jnp.where(lane_s == HEAD_DIM, 1.0, 0.0)).astype(BF16)
            ve_ref[2 * pp + 1] = jnp.where(lane_s >= HEAD_DIM, v, jnp.where(lane_s == 0, 1.0, 0.0)).astype(BF16)

    qi = lax.broadcasted_iota(jnp.int32, (blk, blk), 0)
    ki = lax.broadcasted_iota(jnp.int32, (blk, blk), 1)
    causal = qi >= ki
    eye = (qi == ki).astype(BF16)
    lane = lax.broadcasted_iota(jnp.int32, (blk, LANES), 1)
    lane_head = lane // HEAD_DIM
    blk_off = (own - lane % nb).astype(F32) * float(blk)
    own_start = pl.multiple_of(own * blk, blk)

    qh = []
    for pp in range(nh // 2):
        q = q_ref[:, pp * LANES:(pp + 1) * LANES]
        qs_all = (q * (HEAD_DIM ** -0.5 * LOG2E)).astype(BF16)
        gates = _dot_nt(km_ref[pp], q, precision=HIGHEST)
        sel_t = jnp.concatenate([_select_blocks(gates[hh * nb:(hh + 1) * nb, :], own) for hh in range(2)]
                                + [jnp.zeros((LANES - 2 * nb, blk), F32)], axis=0).astype(BF16)
        sel = _dot_nt(eye, sel_t)
        head_no = (cb * nh + 2 * pp + 1 + (lane >= nb).astype(jnp.int32)).astype(F32)
        slope2 = jnp.exp2(head_no * (-8.0 / N_HEADS)) * LOG2E
        mb_ref[pp] = jnp.where(sel > 0.5, -slope2 * blk_off, NEG_INF)
        for hh in range(2):
            qh.append(jnp.where(lane_head == hh, qs_all, jnp.zeros_like(qs_all)))

    ms = []
    for h in range(nh):
        pp = h // 2
        kd = kb_ref[pl.ds(own_start, blk), pp * LANES:(pp + 1) * LANES]
        s = jnp.where(causal, _dot_nt(qh[h], kd) + bias_ref[h], NEG_INF)
        m = jnp.max(s, axis=1, keepdims=True)
        p = jnp.exp2(s - m)
        acc_ref[h] = _dot(p.astype(BF16), ve_ref[h, pl.ds(own_start, blk), :])
        ms.append(m)

    def body(j, carry):
        start = pl.multiple_of(j * blk, blk)
        out = []
        for h in range(nh):
            pp, hh = h // 2, h % 2
            kj = kb_ref[pl.ds(start, blk), pp * LANES:(pp + 1) * LANES]
            m = carry[h]
            s = _dot_nt(qh[h], kj) + bias_ref[h]
            mb = jnp.sum(jnp.where(lane == j + hh * nb, mb_ref[pp], 0.0), axis=1, keepdims=True)
            m_new = jnp.maximum(m, jnp.max(s, axis=1, keepdims=True) + mb)
            p = jnp.exp2(s + (mb - m_new))
            a = jnp.exp2(m - m_new)
            acc_ref[h] = a * acc_ref[h] + _dot(p.astype(BF16), ve_ref[h, pl.ds(start, blk), :])
            out.append(m_new)
        return tuple(out)

    lax.fori_loop(0, own, body, tuple(ms))
    for pp in range(nh // 2):
        acc0 = acc_ref[2 * pp]
        acc1 = acc_ref[2 * pp + 1]
        out = jnp.where(lane < HEAD_DIM, acc0 / acc0[:, HEAD_DIM:HEAD_DIM + 1], acc1 / acc1[:, 0:1])
        o_ref[:, pp * LANES:(pp + 1) * LANES] = out.astype(o_ref.dtype)


def _alibi_bias():
    slopes = 2.0 ** (-8.0 * jnp.arange(1, N_HEADS + 1, dtype=F32) / N_HEADS)
    qi = jnp.arange(MOBA_BLOCK, dtype=F32)[:, None]
    ki = jnp.arange(MOBA_BLOCK, dtype=F32)[None, :]
    return -(slopes * LOG2E)[:, None, None] * (qi - ki)[None]


def _moba_prompt(q, k, v):
    b, s, d = q.shape
    nb = s // MOBA_BLOCK
    nh = HEADS_PER_STEP
    w = nh * HEAD_DIM
    qspec = pl.BlockSpec((None, MOBA_BLOCK, w), lambda bi, cb, qb: (bi, qb, cb))
    kvspec = pl.BlockSpec((None, s, w), lambda bi, cb, qb: (bi, 0, cb))
    bspec = pl.BlockSpec((nh, MOBA_BLOCK, MOBA_BLOCK), lambda bi, cb, qb: (cb, 0, 0))
    return pl.pallas_call(
        _attn_kernel,
        grid=(b, d // w, nb),
        in_specs=[qspec, kvspec, kvspec, bspec],
        out_specs=qspec,
        out_shape=jax.ShapeDtypeStruct((b, s, d), BF16),
        scratch_shapes=[pltpu.VMEM((s, w), BF16), pltpu.VMEM((nh, s, LANES), BF16),
                        pltpu.VMEM((nh // 2, 2 * nb, LANES), F32),
                        pltpu.VMEM((nh // 2, MOBA_BLOCK, LANES), F32),
                        pltpu.VMEM((nh, MOBA_BLOCK, LANES), F32)],
        compiler_params=_params(3),
        name="moba_prompt",
    )(q, k, v, _alibi_bias())


def _attn_sample_kernel(nb, ppb, pt_ref, q_ref, kn_ref, vn_ref, *refs):
    del pt_ref
    kp = refs[:ppb]
    vp = refs[ppb:2 * ppb]
    o_ref = refs[2 * ppb]
    km_ref, m_ref, l_ref, acc_ref = refs[2 * ppb + 1:]
    b = pl.program_id(1)
    past = nb * MOBA_BLOCK

    q = q_ref[...]
    head = lax.broadcasted_iota(jnp.int32, (N_HEADS, 1), 0)
    slope = jnp.exp2((head + 1).astype(F32) * (-8.0 / N_HEADS))
    qs = q * (HEAD_DIM ** -0.5)

    ksum = jnp.zeros((N_HEADS, HEAD_DIM), F32)
    scores = []
    m = jnp.full((N_HEADS, 1), NEG_INF, F32)
    for p in range(ppb):
        k = kp[p][...]
        ksum = ksum + jnp.sum(k, axis=0)
        s = jnp.sum(k * qs[None], axis=-1, keepdims=True)
        kpos = (lax.broadcasted_iota(jnp.int32, (PAGE_SIZE, 1, 1), 0)
                + (b * ppb + p) * PAGE_SIZE).astype(F32)
        s = s - slope[None] * (float(past) - kpos)
        scores.append(s)
        m = jnp.maximum(m, jnp.max(s, axis=0))
    l = jnp.zeros((N_HEADS, 1), F32)
    acc = jnp.zeros((N_HEADS, HEAD_DIM), F32)
    for p in range(ppb):
        e = jnp.exp(scores[p] - m[None])
        l = l + jnp.sum(e, axis=0)
        acc = acc + jnp.sum(e * vp[p][...], axis=0)
    km_ref[b] = ksum * (1.0 / MOBA_BLOCK)
    m_ref[b] = m
    l_ref[b] = l
    acc_ref[b] = acc

    @pl.when(b == nb - 1)
    def _():
        gate = jnp.sum(km_ref[...] * q[None], axis=-1, keepdims=True)
        ms = m_ref[...]
        keep = []
        for n in range(nb):
            cnt = jnp.zeros((N_HEADS, 1), F32)
            for j in range(nb):
                if j != n:
                    beats = (gate[j] > gate[n]) | ((gate[j] == gate[n]) & (j < n))
                    cnt = cnt + beats.astype(F32)
            keep.append(cnt < MOBA_TOPK)
        s_own = jnp.sum(kn_ref[...] * qs, axis=-1, keepdims=True)
        mx = s_own
        for n in range(nb):
            mx = jnp.maximum(mx, jnp.where(keep[n], ms[n], NEG_INF))
        w_own = jnp.exp(s_own - mx)
        den = w_own
        num = w_own * vn_ref[...]
        for n in range(nb):
            w = jnp.where(keep[n], jnp.exp(ms[n] - mx), 0.0)
            den = den + w * l_ref[n]
            num = num + w * acc_ref[n]
        o_ref[...] = (num / den).astype(o_ref.dtype)


def _moba_sample(q, k_new, v_new, cache_k, cache_v, layer, page_table):
    n_seq, n_pages = page_table.shape
    nb = n_pages * PAGE_SIZE // MOBA_BLOCK
    ppb = MOBA_BLOCK // PAGE_SIZE
    hd = (N_HEADS, HEAD_DIM)
    pt = page_table.reshape(-1).astype(jnp.int32)
    row = pl.BlockSpec((None,) + hd, lambda i, b, pt: (i, 0, 0))

    def page_spec(p):
        return pl.BlockSpec((None, None, PAGE_SIZE) + hd,
                            lambda i, b, pt: (layer, pt[i * n_pages + b * ppb + p], 0, 0, 0))

    grid_spec = pltpu.PrefetchScalarGridSpec(
        num_scalar_prefetch=1,
        grid=(n_seq, nb),
        in_specs=[row, row, row] + [page_spec(p) for p in range(ppb)] * 2,
        out_specs=row,
        scratch_shapes=[pltpu.VMEM((nb,) + hd, F32), pltpu.VMEM((nb, N_HEADS, 1), F32),
                        pltpu.VMEM((nb, N_HEADS, 1), F32), pltpu.VMEM((nb,) + hd, F32)],
    )
    out = pl.pallas_call(
        functools.partial(_attn_sample_kernel, nb, ppb),
        grid_spec=grid_spec,
        out_shape=jax.ShapeDtypeStruct((n_seq,) + hd, BF16),
        compiler_params=_params(2),
        name="moba_sample",
    )(pt, q.reshape((n_seq,) + hd), k_new.reshape((n_seq,) + hd), v_new.reshape((n_seq,) + hd),
      *([cache_k] * ppb), *([cache_v] * ppb))
    return out.reshape(n_seq, D_MODEL)


def _post_kernel(alpha, a_ref, x_ref, gm_ref, shf_ref, scf_ref, gf_ref, lng_ref, lnb_ref,
                 wo_ref, w1_ref, w3_ref, w2_ref, o_ref, acc_ref):
    m = _dot(a_ref[...], wo_ref[...])
    y1 = _layernorm(alpha * x_ref[...] + (1.0 + gm_ref[...]) * m, lng_ref[0:1, :], lnb_ref[0:1, :])
    h = (y1 * (1.0 + scf_ref[...]) + shf_ref[...]).astype(BF16)
    acc_ref[...] = jnp.zeros_like(acc_ref)

    def body(c, carry):
        t1 = _dot(h, w1_ref[c])
        t3 = _dot(h, w3_ref[c])
        acc_ref[...] += _dot((_silu(t1) * t3).astype(BF16), w2_ref[c])
        return carry

    lax.fori_loop(0, w1_ref.shape[0], body, 0)
    o_ref[...] = _layernorm(alpha * y1 + (1.0 + gf_ref[...]) * acc_ref[...],
                            lng_ref[1:2, :], lnb_ref[1:2, :])


def _post_attn(attn, x, gm, shf, scf, gf, ln_g, ln_b, wo, w1c, w3c, w2c, tm, alpha):
    t = x.shape[0]
    d = D_MODEL
    row = pl.BlockSpec((tm, d), lambda i: (i, 0))
    return pl.pallas_call(
        functools.partial(_post_kernel, alpha),
        grid=(t // tm,),
        in_specs=[row, row, gm[1], shf[1], scf[1], gf[1], _resident((2, d)), _resident((2, d)),
                  _resident(wo.shape), _resident(w1c.shape), _resident(w3c.shape), _resident(w2c.shape)],
        out_specs=row,
        out_shape=jax.ShapeDtypeStruct((t, d), F32),
        scratch_shapes=[pltpu.VMEM((tm, d), F32)],
        compiler_params=_params(1),
        name="post_attn_ffn",
    )(attn, x, gm[0], shf[0], scf[0], gf[0], ln_g, ln_b, wo, w1c, w3c, w2c)


def _glu(y, shm_ref, scm_ref, wpw1_ref, bpw1_ref):
    d = D_MODEL
    h = (y * (1.0 + scm_ref[...]) + shm_ref[...]).astype(BF16)
    a = _dot(h, wpw1_ref[:, 0:d]) + bpw1_ref[:, 0:d]
    g = _dot(h, wpw1_ref[:, d:2 * d]) + bpw1_ref[:, d:2 * d]
    return a * jax.nn.sigmoid(g)


def _conv_tail(alpha, y, conv, gm_ref, shf_ref, scf_ref, lng_ref, lnb_ref, clg_ref, clb_ref,
               wpw2_ref, bpw2_ref, wr_ref, br_ref, y3_ref, h2_ref, lg_ref):
    z = _silu(_layernorm(conv, clg_ref[...], clb_ref[...])).astype(BF16)
    m = _dot(z, wpw2_ref[...]) + bpw2_ref[...]
    y3 = _layernorm(alpha * y + (1.0 + gm_ref[...]) * m, lng_ref[0:1, :], lnb_ref[0:1, :])
    y3_ref[...] = y3
    h2 = y3 * (1.0 + scf_ref[...]) + shf_ref[...]
    _rows_to_tiles(h2_ref, h2)
    lg_ref[...] = _dot(h2, wr_ref[...], precision=HIGHEST) + br_ref[...]


def _conv_prompt_kernel(alpha, y_ref, shm_ref, scm_ref, gm_ref, shf_ref, scf_ref, lng_ref, lnb_ref,
                        wpw1_ref, bpw1_ref, wdw_ref, bdw_ref, clg_ref, clb_ref, wpw2_ref, bpw2_ref,
                        wr_ref, br_ref, y3_ref, h2_ref, lg_ref, tail_ref, ext_ref, shift_ref, conv_ref):
    tm = y_ref.shape[0]
    d = D_MODEL
    rows_ext = tm + HIST_PAD

    @pl.when(pl.program_id(1) == 0)
    def _():
        ext_ref[0:HIST_PAD, :] = jnp.zeros((HIST_PAD, d), F32)

    y = y_ref[...]
    u = _glu(y, shm_ref, scm_ref, wpw1_ref, bpw1_ref)
    ext_ref[HIST_PAD:rows_ext, :] = u
    tail_ref[...] = u[tm - HIST_PAD:tm, :]
    span = rows_ext - SUBLANES
    for b in range(1, SUBLANES):
        shift_ref[b - 1, :, :] = ext_ref[b:b + span, :]
    base = HIST_PAD - (CONV_WIDTH - 1)
    rb = 4 * SUBLANES

    def body(r, carry):
        r0 = pl.multiple_of(r * rb, rb)
        acc = jnp.broadcast_to(bdw_ref[...], (rb, d))
        for k in range(CONV_WIDTH):
            off = base + k
            start = r0 + (off // SUBLANES) * SUBLANES
            if off % SUBLANES == 0:
                x = ext_ref[pl.ds(start, rb), :]
            else:
                x = shift_ref[off % SUBLANES - 1, pl.ds(start, rb), :]
            acc = acc + wdw_ref[k:k + 1, :] * x
        conv_ref[pl.ds(r0, rb), :] = acc
        return carry

    lax.fori_loop(0, tm // rb, body, 0)
    ext_ref[0:HIST_PAD, :] = u[tm - HIST_PAD:tm, :]
    _conv_tail(alpha, y, conv_ref[...], gm_ref, shf_ref, scf_ref, lng_ref, lnb_ref, clg_ref, clb_ref,
               wpw2_ref, bpw2_ref, wr_ref, br_ref, y3_ref, h2_ref, lg_ref)


def _conv_sample_kernel(alpha, y_ref, shm_ref, scm_ref, gm_ref, shf_ref, scf_ref, lng_ref, lnb_ref,
                        wpw1_ref, bpw1_ref, wdw_ref, bdw_ref, clg_ref, clb_ref, wpw2_ref, bpw2_ref,
                        wr_ref, br_ref, hist_ref, y3_ref, h2_ref, lg_ref, u_ref):
    nh = CONV_WIDTH - 1
    y = y_ref[...]
    u = _glu(y, shm_ref, scm_ref, wpw1_ref, bpw1_ref)
    u_ref[...] = u
    w = wdw_ref[...]
    conv = bdw_ref[...] + w[nh:nh + 1, :] * u + jnp.sum(hist_ref[...] * w[0:nh, :][None], axis=1)
    _conv_tail(alpha, y, conv, gm_ref, shf_ref, scf_ref, lng_ref, lnb_ref, clg_ref, clb_ref,
               wpw2_ref, bpw2_ref, wr_ref, br_ref, y3_ref, h2_ref, lg_ref)


def _conv_weight_specs(d):
    return [_resident((2, d)), _resident((2, d)), _resident((d, 2 * d)), _resident((1, 2 * d)),
            _resident((CONV_WIDTH, d)), _resident((1, d)), _resident((1, d)), _resident((1, d)),
            _resident((d, d)), _resident((1, d)), _resident((d, LANES)), _resident((1, LANES))]


def _conv_prompt(y, mods5, weights, n_seq, seq_len, alpha):
    d = D_MODEL
    tm = TM_CONV
    tps = seq_len // tm
    t = n_seq * seq_len
    row = pl.BlockSpec((tm, d), lambda b, i: (b * tps + i, 0))
    trow = pl.BlockSpec((tm * ROW_TILES, LANES), lambda b, i: (b * tps + i, 0))
    lgrow = pl.BlockSpec((tm, LANES), lambda b, i: (b * tps + i, 0))
    tail = pl.BlockSpec((None, HIST_PAD, d), lambda b, i: (b, 0, 0))
    return pl.pallas_call(
        functools.partial(_conv_prompt_kernel, alpha),
        grid=(n_seq, tps),
        in_specs=[row] + [m[1] for m in mods5] + _conv_weight_specs(d),
        out_specs=[row, trow, lgrow, tail],
        out_shape=[jax.ShapeDtypeStruct((t, d), F32),
                   jax.ShapeDtypeStruct((t * ROW_TILES, LANES), F32),
                   jax.ShapeDtypeStruct((t, LANES), F32),
                   jax.ShapeDtypeStruct((n_seq, HIST_PAD, d), F32)],
        scratch_shapes=[pltpu.VMEM((tm + HIST_PAD, d), F32),
                        pltpu.VMEM((SUBLANES - 1, tm + HIST_PAD - SUBLANES, d), F32),
                        pltpu.VMEM((tm, d), F32)],
        compiler_params=_params(2),
        name="conv_prompt",
    )(y, *[m[0] for m in mods5], *weights)


def _conv_sample(y, mods5, weights, hist, alpha):
    d = D_MODEL
    n = y.shape[0]
    tm = 32
    row = pl.BlockSpec((tm, d), lambda i: (i, 0))
    return pl.pallas_call(
        functools.partial(_conv_sample_kernel, alpha),
        grid=(n // tm,),
        in_specs=[row] + [m[1] for m in mods5] + _conv_weight_specs(d)
        + [pl.BlockSpec((tm, CONV_WIDTH - 1, d), lambda i: (i, 0, 0))],
        out_specs=[row, pl.BlockSpec((tm * ROW_TILES, LANES), lambda i: (i, 0)),
                   pl.BlockSpec((tm, LANES), lambda i: (i, 0)), row],
        out_shape=[jax.ShapeDtypeStruct((n, d), F32),
                   jax.ShapeDtypeStruct((n * ROW_TILES, LANES), F32),
                   jax.ShapeDtypeStruct((n, LANES), F32),
                   jax.ShapeDtypeStruct((n, d), F32)],
        compiler_params=_params(1),
        name="conv_sample",
    )(y, *[m[0] for m in mods5], *weights, hist)


def _route_tile(lg_ref, meta_ref, meta_t_ref, carry_ref):
    tm = lg_ref.shape[0]
    lane = lax.broadcasted_iota(jnp.int32, (tm, LANES), 1)
    lg = jnp.where(lane < N_EXPERTS, lg_ref[...], NEG_INF)
    m1 = jnp.max(lg, axis=1, keepdims=True)
    i1 = jnp.min(jnp.where(lg == m1, lane, LANES), axis=1, keepdims=True)
    lg2 = jnp.where(lane == i1, NEG_INF, lg)
    m2 = jnp.max(lg2, axis=1, keepdims=True)
    i2 = jnp.min(jnp.where(lg2 == m2, lane, LANES), axis=1, keepdims=True)
    e = jnp.exp(m2 - m1)
    g1 = 1.0 / (1.0 + e)
    g2 = e / (1.0 + e)
    hot1 = lane == i1
    hot2 = lane == i2
    onehot = (hot1 | hot2).astype(BF16)
    ri = lax.broadcasted_iota(jnp.int32, (tm, tm), 0)
    ci = lax.broadcasted_iota(jnp.int32, (tm, tm), 1)
    before = (ci < ri).astype(BF16)
    rank = _dot(before, onehot) + carry_ref[...]
    r1 = jnp.sum(jnp.where(hot1, rank, 0.0), axis=1, keepdims=True)
    r2 = jnp.sum(jnp.where(hot2, rank, 0.0), axis=1, keepdims=True)
    cols = (i1.astype(F32), i2.astype(F32), g1, g2, r1, r2)
    meta = jnp.zeros((tm, LANES), F32)
    for c, val in enumerate(cols):
        meta = jnp.where(lane == c, val, meta)
    meta_ref[...] = meta
    meta_t_ref[...] = jnp.transpose(meta)[0:SUBLANES, :]
    carry_ref[...] += jnp.sum(onehot.astype(F32), axis=0, keepdims=True)


def _route_kernel(n_prompt_tiles, lgp_ref, lgs_ref, mp_ref, mpt_ref, ms_ref, mst_ref, cnt_ref, carry_ref):
    i = pl.program_id(0)

    @pl.when(i == 0)
    def _():
        carry_ref[...] = jnp.zeros_like(carry_ref)

    @pl.when(i < n_prompt_tiles)
    def _():
        _route_tile(lgp_ref, mp_ref, mpt_ref, carry_ref)

    @pl.when(i == n_prompt_tiles)
    def _():
        _route_tile(lgs_ref, ms_ref, mst_ref, carry_ref)
        cnt_ref[...] = carry_ref[...]


def _route(lg_p, lg_s):
    tp, ts = lg_p.shape[0], lg_s.shape[0]
    tm = TM_ROUTE
    npt = tp // tm
    last = npt - 1
    return pl.pallas_call(
        functools.partial(_route_kernel, npt),
        grid=(npt + 1,),
        in_specs=[pl.BlockSpec((tm, LANES), lambda i: (jnp.minimum(i, last), 0)),
                  pl.BlockSpec((ts, LANES), lambda i: (0, 0))],
        out_specs=[pl.BlockSpec((tm, LANES), lambda i: (jnp.minimum(i, last), 0)),
                   pl.BlockSpec((SUBLANES, tm), lambda i: (0, jnp.minimum(i, last))),
                   pl.BlockSpec((ts, LANES), lambda i: (0, 0)),
                   pl.BlockSpec((SUBLANES, ts), lambda i: (0, 0)),
                   pl.BlockSpec((1, LANES), lambda i: (0, 0))],
        out_shape=[jax.ShapeDtypeStruct((tp, LANES), F32),
                   jax.ShapeDtypeStruct((SUBLANES, tp), F32),
                   jax.ShapeDtypeStruct((ts, LANES), F32),
                   jax.ShapeDtypeStruct((SUBLANES, ts), F32),
                   jax.ShapeDtypeStruct((1, LANES), F32)],
        scratch_shapes=[pltpu.VMEM((1, LANES), F32)],
        compiler_params=_params(1),
        name="moe_route",
    )(lg_p, lg_s)


def _row_copy(src_ref, src_row, dst_ref, dst_row, sem, n=1):
    return pltpu.make_async_copy(
        src_ref.at[pl.ds(pl.multiple_of(src_row * ROW_TILES, ROW_TILES), n * ROW_TILES), :],
        dst_ref.at[pl.ds(pl.multiple_of(dst_row * ROW_TILES, ROW_TILES), n * ROW_TILES), :], sem)


def _scatter_rows(slots_ref, h_ref, xs_ref, sem):
    tm = h_ref.shape[0] // ROW_TILES

    def issue(r, carry):
        for c in range(MOE_TOPK):
            _row_copy(h_ref, r, xs_ref, slots_ref[0, c * tm + r], sem).start()
        return carry

    lax.fori_loop(0, tm, issue, 0)
    for c in range(MOE_TOPK):
        _row_copy(h_ref, 0, xs_ref, 0, sem, n=tm).wait()


def _scatter_kernel(n_prompt_tiles, zlo_ref, zhi_ref, sp_ref, ss_ref, hp_ref, hs_ref, xs_ref,
                    zero_ref, sem, zsem):
    i = pl.program_id(0)

    @pl.when(i < n_prompt_tiles)
    def _():
        _scatter_rows(sp_ref, hp_ref, xs_ref, sem)

    @pl.when(i == n_prompt_tiles)
    def _():
        _scatter_rows(ss_ref, hs_ref, xs_ref, sem)
        zero_ref[...] = jnp.zeros_like(zero_ref)
        for e in range(N_EXPERTS):
            lo = zlo_ref[e]
            n = zhi_ref[e] - lo
            nbig = n // ZERO_ROWS

            def big(k, carry, lo=lo):
                return _row_copy(zero_ref, 0, xs_ref, lo + k * ZERO_ROWS, zsem, n=ZERO_ROWS)

            def small(r, carry, lo=lo):
                return _row_copy(zero_ref, 0, xs_ref, lo + r, zsem)

            lax.fori_loop(0, nbig, lambda k, c: (big(k, c).start(), c)[1], 0)
            lax.fori_loop(nbig * ZERO_ROWS, n, lambda r, c: (small(r, c).start(), c)[1], 0)
            lax.fori_loop(0, nbig, lambda k, c: (big(k, c).wait(), c)[1], 0)
            lax.fori_loop(nbig * ZERO_ROWS, n, lambda r, c: (small(r, c).wait(), c)[1], 0)


def _scatter(h2t_p, h2t_s, slots_p, slots_s, zero_lo, zero_hi, n_rows):
    tp = h2t_p.shape[0] // ROW_TILES
    ts = h2t_s.shape[0] // ROW_TILES
    tm = TM_ROUTE
    npt = tp // tm
    last = npt - 1
    grid_spec = pltpu.PrefetchScalarGridSpec(
        num_scalar_prefetch=2,
        grid=(npt + 1,),
        in_specs=[pl.BlockSpec((None, 1, MOE_TOPK * tm), lambda i, lo, hi: (jnp.minimum(i, last), 0, 0),
                               memory_space=pltpu.SMEM),
                  pl.BlockSpec((None, 1, MOE_TOPK * ts), lambda i, lo, hi: (0, 0, 0), memory_space=pltpu.SMEM),
                  pl.BlockSpec((tm * ROW_TILES, LANES), lambda i, lo, hi: (jnp.minimum(i, last), 0)),
                  pl.BlockSpec((ts * ROW_TILES, LANES), lambda i, lo, hi: (0, 0))],
        out_specs=pl.BlockSpec(memory_space=pl.ANY),
        scratch_shapes=[pltpu.VMEM((ZERO_ROWS * ROW_TILES, LANES), F32), pltpu.SemaphoreType.DMA(()),
                        pltpu.SemaphoreType.DMA(())],
    )
    return pl.pallas_call(
        functools.partial(_scatter_kernel, npt),
        grid_spec=grid_spec,
        out_shape=jax.ShapeDtypeStruct((n_rows * ROW_TILES, LANES), F32),
        compiler_params=pltpu.CompilerParams(dimension_semantics=("arbitrary",),
                                             vmem_limit_bytes=VMEM_LIMIT, has_side_effects=True),
        name="moe_scatter",
    )(zero_lo, zero_hi, slots_p, slots_s, h2t_p, h2t_s)


def _gmm_kernel(exp_ref, nv_ref, x_ref, w1_ref, w3_ref, w2_ref, o_ref, xb_ref, acc_ref):
    del exp_ref
    m = pl.program_id(0)
    f = pl.program_id(1)
    nf = pl.num_programs(1)
    tg = xb_ref.shape[0]
    live = m < nv_ref[0]

    @pl.when(live)
    def _():
        @pl.when(f == 0)
        def _():
            xb_ref[...] = _tiles_to_rows(x_ref, tg).astype(BF16)

        x = xb_ref[...]
        t1 = _dot(x, w1_ref[...])
        t3 = _dot(x, w3_ref[...])
        part = _dot((_silu(t1) * t3).astype(BF16), w2_ref[...])

        @pl.when(f == 0)
        def _():
            acc_ref[...] = part

        @pl.when(f > 0)
        def _():
            acc_ref[...] += part

        @pl.when(f == nf - 1)
        def _():
            _rows_to_tiles(o_ref, acc_ref[...])

    @pl.when(jnp.logical_not(live) & (f == nf - 1))
    def _():
        o_ref[...] = jnp.zeros_like(o_ref)


def _gmm(xs, w1, w3, w2, tile_exp, n_valid):
    d = D_MODEL
    tg = TM_GMM
    tf = TF_GMM
    dfe = w1.shape[2]
    nf = dfe // tf
    n_tiles = tile_exp.shape[0]

    def fidx(m, f, nv):
        return jnp.where(m < nv[0], f, nf - 1)

    grid_spec = pltpu.PrefetchScalarGridSpec(
        num_scalar_prefetch=2,
        grid=(n_tiles, nf),
        in_specs=[
            pl.BlockSpec((tg * ROW_TILES, LANES), lambda m, f, ex, nv: (m, 0)),
            pl.BlockSpec((None, d, tf), lambda m, f, ex, nv: (ex[m], 0, fidx(m, f, nv))),
            pl.BlockSpec((None, d, tf), lambda m, f, ex, nv: (ex[m], 0, fidx(m, f, nv))),
            pl.BlockSpec((None, tf, d), lambda m, f, ex, nv: (ex[m], fidx(m, f, nv), 0)),
        ],
        out_specs=pl.BlockSpec((tg * ROW_TILES, LANES), lambda m, f, ex, nv: (m, 0)),
        scratch_shapes=[pltpu.VMEM((tg, d), BF16), pltpu.VMEM((tg, d), F32)],
    )
    return pl.pallas_call(
        _gmm_kernel,
        grid_spec=grid_spec,
        out_shape=jax.ShapeDtypeStruct(xs.shape, F32),
        compiler_params=_params(2),
        name="moe_experts",
    )(tile_exp, n_valid, xs, w1, w3, w2)


def _combine_kernel(alpha, slots_ref, ys_ref, y_ref, meta_ref, gf_ref, lng_ref, lnb_ref, o_ref,
                    buf_ref, sem):
    tm = y_ref.shape[0]

    def issue(r, carry):
        for c in range(MOE_TOPK):
            _row_copy(ys_ref, slots_ref[0, c * tm + r], buf_ref.at[c], r, sem).start()
        return carry

    lax.fori_loop(0, tm, issue, 0)
    for c in range(MOE_TOPK):
        _row_copy(ys_ref, 0, buf_ref.at[c], 0, sem, n=tm).wait()
    meta = meta_ref[...]
    f = jnp.zeros((tm, D_MODEL), F32)
    for c in range(MOE_TOPK):
        f = f + _tiles_to_rows(buf_ref.at[c], tm) * meta[:, META_GATE + c:META_GATE + c + 1]
    o_ref[...] = _layernorm(alpha * y_ref[...] + (1.0 + gf_ref[...]) * f,
                            lng_ref[1:2, :], lnb_ref[1:2, :])


def _combine(ys, y3, slots3, meta, gf, ln_g, ln_b, tm, alpha):
    t, d = y3.shape
    return pl.pallas_call(
        functools.partial(_combine_kernel, alpha),
        grid=(t // tm,),
        in_specs=[pl.BlockSpec((None, 1, MOE_TOPK * tm), lambda i: (i, 0, 0), memory_space=pltpu.SMEM),
                  pl.BlockSpec(memory_space=pl.ANY),
                  pl.BlockSpec((tm, d), lambda i: (i, 0)),
                  pl.BlockSpec((tm, LANES), lambda i: (i, 0)),
                  gf[1], _resident((2, d)), _resident((2, d))],
        out_specs=pl.BlockSpec((tm, d), lambda i: (i, 0)),
        out_shape=jax.ShapeDtypeStruct((t, d), F32),
        scratch_shapes=[pltpu.VMEM((MOE_TOPK, tm * ROW_TILES, LANES), F32), pltpu.SemaphoreType.DMA(())],
        compiler_params=_params(1),
        name="moe_combine",
    )(slots3, ys, y3, meta, gf[0], ln_g, ln_b)


def _tile_slots(slots, tm):
    k, t = slots.shape
    return slots.reshape(k, t // tm, tm).transpose(1, 0, 2).reshape(t // tm, 1, k * tm)


def kernel(x_prompt, x_sample, cache_k, cache_v, state_conv, page_table, c_prompt, c_sample,
           ada_w, ada_b, ln_g, ln_b, attn_w_qkv, attn_w_o,
           conv_w_pw1, conv_b_pw1, conv_w_dw, conv_b_dw, conv_ln_g, conv_ln_b, conv_w_pw2, conv_b_pw2,
           ffn_w1, ffn_w3, ffn_w2, moe_w_router, moe_b_router, moe_w1, moe_w3, moe_w2):
    b, s, d = x_prompt.shape
    db, ds, _ = x_sample.shape
    depth = ada_w.shape[0]
    assert d == D_MODEL and ds == 1 and depth == 2
    assert page_table.shape[1] * PAGE_SIZE % MOBA_BLOCK == 0
    alpha = (2 * depth) ** 0.25
    tp = b * s
    t_all = tp + db
    assert tp % TM_ROUTE == 0 and tp % TM_COMBINE == 0

    c_all = jnp.concatenate([c_sample, c_prompt], axis=0)
    mods = _adaln(c_all, ada_w, ada_b)
    mods = _Mods(mods.reshape(depth * 6, db + b, d), db)

    def pmod(layer, k, tm):
        tps = s // tm
        return mods.prompt(layer * 6 + k, lambda i: i // tps)

    def smod(layer, k, rows=None, block_of_step=lambda i: 0):
        return mods.sample(layer * 6 + k, db if rows is None else rows, block_of_step)

    xp = x_prompt.reshape(tp, d)
    xs_ = x_sample.reshape(db, d)

    wqkv = attn_w_qkv[0].astype(BF16)
    qp, kp, vp = _qkv(xp, pmod(0, 0, TM_QKV), pmod(0, 1, TM_QKV), wqkv, TM_QKV)
    qs, ks, vs = _qkv(xs_, smod(0, 0), smod(0, 1), wqkv, db)
    attn_p = _moba_prompt(qp.reshape(b, s, d), kp.reshape(b, s, d), vp.reshape(b, s, d)).reshape(tp, d)
    attn_s = _moba_sample(qs, ks, vs, cache_k, cache_v, 0, page_table)

    dff = ffn_w1.shape[2]
    nck = dff // FFN_CHUNK
    wo = attn_w_o[0].astype(BF16)
    w1c = ffn_w1[0].astype(BF16).reshape(d, nck, FFN_CHUNK).transpose(1, 0, 2)
    w3c = ffn_w3[0].astype(BF16).reshape(d, nck, FFN_CHUNK).transpose(1, 0, 2)
    w2c = ffn_w2[0].astype(BF16).reshape(nck, FFN_CHUNK, d)
    y_p = _post_attn(attn_p, xp, pmod(0, 2, TM_POST), pmod(0, 3, TM_POST), pmod(0, 4, TM_POST),
                     pmod(0, 5, TM_POST), ln_g[0], ln_b[0], wo, w1c, w3c, w2c, TM_POST, alpha)
    y_s = _post_attn(attn_s, xs_, smod(0, 2), smod(0, 3), smod(0, 4), smod(0, 5),
                     ln_g[0], ln_b[0], wo, w1c, w3c, w2c, db, alpha)

    wr = jnp.pad(moe_w_router[0], ((0, 0), (0, LANES - N_EXPERTS)))
    br = jnp.pad(moe_b_router[0], (0, LANES - N_EXPERTS)).reshape(1, LANES)
    conv_weights = (ln_g[1], ln_b[1], conv_w_pw1[0].astype(BF16), conv_b_pw1[0].reshape(1, 2 * d),
                    conv_w_dw[0], conv_b_dw[0].reshape(1, d), conv_ln_g[0].reshape(1, d),
                    conv_ln_b[0].reshape(1, d), conv_w_pw2[0].astype(BF16), conv_b_pw2[0].reshape(1, d),
                    wr, br)
    pm = [mods.prompt(6 + k, lambda bi, i: bi) for k in range(5)]
    y3_p, h2t_p, lg_p, tail = _conv_prompt(y_p, pm, conv_weights, b, s, alpha)
    sm = [smod(1, k, 32, lambda i: i) for k in range(5)]
    y3_s, h2t_s, lg_s, u_s = _conv_sample(y_s, sm, conv_weights, state_conv[0], alpha)

    meta_p, meta_pt, meta_s, meta_st, cnt = _route(lg_p, lg_s)
    counts = cnt[0, 0:N_EXPERTS].astype(jnp.int32)
    padded = (counts + TM_GMM - 1) // TM_GMM * TM_GMM
    pad_end = jnp.cumsum(padded)
    pad_start = pad_end - padded
    n_tiles = -(-t_all * MOE_TOPK // TM_GMM) + N_EXPERTS
    n_rows = n_tiles * TM_GMM

    def slots_of(meta_t):
        e = meta_t[META_EXPERT:META_EXPERT + MOE_TOPK].astype(jnp.int32)
        r = meta_t[META_RANK:META_RANK + MOE_TOPK].astype(jnp.int32)
        return pad_start[e] + r

    slots_p, slots_s = slots_of(meta_pt), slots_of(meta_st)
    zero_lo = pad_start + counts
    zero_hi = jnp.concatenate([pad_start[1:], jnp.full((1,), n_rows, jnp.int32)])
    xs_buf = _scatter(h2t_p, h2t_s, _tile_slots(slots_p, TM_ROUTE), _tile_slots(slots_s, db),
                      zero_lo.astype(jnp.int32), zero_hi.astype(jnp.int32), n_rows)

    tile_ids = jnp.arange(n_tiles, dtype=jnp.int32)
    tile_exp = jnp.minimum(jnp.sum((pad_end // TM_GMM)[None, :] <= tile_ids[:, None], axis=1),
                           N_EXPERTS - 1).astype(jnp.int32)
    n_valid = (pad_end[-1] // TM_GMM).reshape(1).astype(jnp.int32)
    ys_buf = _gmm(xs_buf, moe_w1[0].astype(BF16), moe_w3[0].astype(BF16), moe_w2[0].astype(BF16),
                  tile_exp, n_valid)

    out_p = _combine(ys_buf, y3_p, _tile_slots(slots_p, TM_COMBINE), meta_p,
                     pmod(1, 5, TM_COMBINE), ln_g[1], ln_b[1], TM_COMBINE, alpha)
    out_s = _combine(ys_buf, y3_s, _tile_slots(slots_s, db), meta_s,
                     smod(1, 5), ln_g[1], ln_b[1], db, alpha)

    nh = CONV_WIDTH - 1
    new_conv_p = tail[:, HIST_PAD - nh:, :][None]
    new_conv_s = jnp.concatenate([state_conv[0][:, 1:, :], u_s[:, None, :]], axis=1)[None]
    hshape = (N_HEADS, HEAD_DIM)
    return (out_p.reshape(b, s, d), out_s.reshape(db, 1, d),
            kp.reshape(1, b, s, *hshape), vp.reshape(1, b, s, *hshape),
            ks.reshape(1, db, 1, *hshape), vs.reshape(1, db, 1, *hshape),
            new_conv_p, new_conv_s)
```

```python
import functools
import math

import jax
import jax.numpy as jnp
from jax import lax
from jax.experimental import pallas as pl
from jax.experimental.pallas import tpu as pltpu

F32 = jnp.float32
BF16 = jnp.bfloat16
HIGHEST = lax.Precision.HIGHEST
NEG_INF = float("-inf")
LOG2E = math.log2(math.e)

SUBLANES = 8
LANES = 128

D_MODEL = 1024
N_HEADS = 16
HEAD_DIM = D_MODEL // N_HEADS
HEADS_PER_STEP = 8
MOBA_BLOCK = 256
MOBA_TOPK = 3
PAGE_SIZE = 128
CONV_WIDTH = 31
HIST_PAD = 32
N_EXPERTS = 8
MOE_TOPK = 2
LN_EPS = 1e-5
ROW_TILES = D_MODEL // LANES

TM_QKV = 512
TM_POST = 512
TM_CONV = 256
TM_ROUTE = 512
TM_GMM = 512
TM_COMBINE = 256
TF_GMM = 512
FFN_CHUNK = 256
ZERO_ROWS = 64
VMEM_LIMIT = 56 * 1024 * 1024

META_EXPERT, META_GATE, META_RANK = 0, 2, 4

NT_DIMS = (((1,), (1,)), ((), ()))


def _dot(a, b, precision=None):
    return jnp.dot(a, b, preferred_element_type=F32, precision=precision)


def _dot_nt(a, b, precision=None):
    return lax.dot_general(a, b, NT_DIMS, preferred_element_type=F32, precision=precision)


def _silu(x):
    return x * jax.nn.sigmoid(x)


def _layernorm(z, g, b):
    mu = jnp.mean(z, axis=-1, keepdims=True)
    zc = z - mu
    var = jnp.mean(zc * zc, axis=-1, keepdims=True)
    return zc * lax.rsqrt(var + LN_EPS) * g + b


def _resident(shape):
    nd = len(shape)
    return pl.BlockSpec(shape, lambda *_: (0,) * nd, pipeline_mode=pl.Buffered(1))


def _params(n_grid_dims):
    return pltpu.CompilerParams(
        dimension_semantics=("arbitrary",) * n_grid_dims, vmem_limit_bytes=VMEM_LIMIT)


def _rows_to_tiles(ref, x):
    t = x.shape[0]
    for s in range(ROW_TILES):
        ref[pl.ds(s, t, stride=ROW_TILES), :] = x[:, s * LANES:(s + 1) * LANES]


def _tiles_to_rows(ref, t):
    return jnp.concatenate([ref[pl.ds(s, t, stride=ROW_TILES), :] for s in range(ROW_TILES)], axis=1)


def _ada_kernel(c_ref, w_ref, b_ref, o_ref):
    c = c_ref[...]
    o_ref[...] = _dot(_silu(c).astype(BF16), w_ref[...].astype(BF16)) + b_ref[...]


def _adaln(c_all, ada_w, ada_b):
    depth = ada_w.shape[0]
    n = c_all.shape[0]
    d = D_MODEL
    return pl.pallas_call(
        _ada_kernel,
        grid=(depth, 6),
        in_specs=[
            pl.BlockSpec((n, d), lambda l, k: (0, 0)),
            pl.BlockSpec((None, d, d), lambda l, k: (l, 0, k)),
            pl.BlockSpec((None, None, 1, d), lambda l, k: (l, k, 0, 0)),
        ],
        out_specs=pl.BlockSpec((None, None, n, d), lambda l, k: (l, k, 0, 0)),
        out_shape=jax.ShapeDtypeStruct((depth, 6, n, d), F32),
        compiler_params=_params(2),
        name="adaln",
    )(c_all, ada_w, ada_b.reshape(depth, 6, 1, d))


class _Mods:
    def __init__(self, mods, n_sample):
        self.n_sample = n_sample
        self.m3 = mods
        self.m4 = mods.reshape(mods.shape[0], mods.shape[1], 1, D_MODEL)

    def prompt(self, lk, seq_of_step):
        ns = self.n_sample
        return self.m4, pl.BlockSpec(
            (None, None, 1, D_MODEL), lambda *g: (lk, ns + seq_of_step(*g), 0, 0))

    def sample(self, lk, rows, block_of_step):
        return self.m3, pl.BlockSpec((None, rows, D_MODEL), lambda *g: (lk, block_of_step(*g), 0))


def _qkv_kernel(x_ref, sh_ref, sc_ref, w_ref, q_ref, k_ref, v_ref):
    d = D_MODEL
    h = (x_ref[...] * (1.0 + sc_ref[...]) + sh_ref[...]).astype(BF16)
    q_ref[...] = _dot(h, w_ref[:, 0:d])
    k_ref[...] = _dot(h, w_ref[:, d:2 * d])
    v_ref[...] = _dot(h, w_ref[:, 2 * d:3 * d])


def _qkv(x, sh, sc, w_bf16, tm):
    t = x.shape[0]
    d = D_MODEL
    row = pl.BlockSpec((tm, d), lambda i: (i, 0))
    out = jax.ShapeDtypeStruct((t, d), F32)
    return pl.pallas_call(
        _qkv_kernel,
        grid=(t // tm,),
        in_specs=[row, sh[1], sc[1], _resident((d, 3 * d))],
        out_specs=[row, row, row],
        out_shape=[out, out, out],
        compiler_params=_params(1),
        name="qkv",
    )(x, sh[0], sc[0], w_bf16)


def _select_blocks(gate_t, n_valid):
    nb = gate_t.shape[0]
    blk = lax.broadcasted_iota(jnp.int32, gate_t.shape, 0)
    valid = blk < n_valid
    rows = []
    for n in range(nb):
        gn = gate_t[n:n + 1, :]
        beats = ((gate_t > gn) | ((gate_t == gn) & (blk < n))) & valid
        rows.append(jnp.sum(beats.astype(F32), axis=0, keepdims=True))
    cnt = jnp.concatenate(rows, axis=0)
    return (valid & (cnt < MOBA_TOPK)).astype(F32)


def _attn_kernel(q_ref, k_ref, v_ref, bias_ref, o_ref, kb_ref, ve_ref, km_ref, mb_ref, acc_ref):
    blk = MOBA_BLOCK
    s_len = k_ref.shape[0]
    nb = s_len // blk
    nh = HEADS_PER_STEP
    cb = pl.program_id(1)
    own = pl.program_id(2)

    @pl.when(own == 0)
    def _():
        kb_ref[...] = k_ref[...].astype(BF16)
        lane_s = lax.broadcasted_iota(jnp.int32, (s_len, LANES), 1)
        lane_k = lax.broadcasted_iota(jnp.int32, (nb, LANES), 1) // HEAD_DIM
        for pp in range(nh // 2):
            cols = slice(pp * LANES, (pp + 1) * LANES)
            rows = [jnp.mean(k_ref[n * blk:(n + 1) * blk, cols], axis=0, keepdims=True) for n in range(nb)]
            km = jnp.concatenate(rows, axis=0)
            km_ref[pp] = jnp.concatenate([jnp.where(lane_k == 0, km, 0.0), jnp.where(lane_k == 1, km, 0.0)], axis=0)
            v = v_ref[:, cols]
            ve_ref[2 * pp] = jnp.where(lane_s < HEAD_DIM, v, jnp.where(lane_s == HEAD_DIM, 1.0, 0.0)).astype(BF16)
            ve_ref[2 * pp + 1] = jnp.where(lane_s >= HEAD_DIM, v, jnp.where(lane_s == 0, 1.0, 0.0)).astype(BF16)

    qi = lax.broadcasted_iota(jnp.int32, (blk, blk), 0)
    ki = lax.broadcasted_iota(jnp.int32, (blk, blk), 1)
    causal = qi >= ki
    eye = (qi == ki).astype(BF16)
    lane = lax.broadcasted_iota(jnp.int32, (blk, LANES), 1)
    lane_head = lane // HEAD_DIM
    blk_off = (own - lane % nb).astype(F32) * float(blk)
    own_start = pl.multiple_of(own * blk, blk)

    qh = []
    for pp in range(nh // 2):
        q = q_ref[:, pp * LANES:(pp + 1) * LANES]
        qs_all = (q * (HEAD_DIM ** -0.5 * LOG2E)).astype(BF16)
        gates = _dot_nt(km_ref[pp], q, precision=HIGHEST)
        sel_t = jnp.concatenate([_select_blocks(gates[hh * nb:(hh + 1) * nb, :], own) for hh in range(2)]
                                + [jnp.zeros((LANES - 2 * nb, blk), F32)], axis=0).astype(BF16)
        sel = _dot_nt(eye, sel_t)
        head_no = (cb * nh + 2 * pp + 1 + (lane >= nb).astype(jnp.int32)).astype(F32)
        slope2 = jnp.exp2(head_no * (-8.0 / N_HEADS)) * LOG2E
        mb_ref[pp] = jnp.where(sel > 0.5, -slope2 * blk_off, NEG_INF)
        for hh in range(2):
            qh.append(jnp.where(lane_head == hh, qs_all, jnp.zeros_like(qs_all)))

    ms = []
    for h in range(nh):
        pp = h // 2
        kd = kb_ref[pl.ds(own_start, blk), pp * LANES:(pp + 1) * LANES]
        s = jnp.where(causal, _dot_nt(qh[h], kd) + bias_ref[h], NEG_INF)
        m = jnp.max(s, axis=1, keepdims=True)
        p = jnp.exp2(s - m)
        acc_ref[h] = _dot(p.astype(BF16), ve_ref[h, pl.ds(own_start, blk), :])
        ms.append(m)

    def body(j, carry):
        start = pl.multiple_of(j * blk, blk)
        out = []
        for h in range(nh):
            pp, hh = h // 2, h % 2
            kj = kb_ref[pl.ds(start, blk), pp * LANES:(pp + 1) * LANES]
            m = carry[h]
            s = _dot_nt(qh[h], kj) + bias_ref[h]
            mb = jnp.sum(jnp.where(lane == j + hh * nb, mb_ref[pp], 0.0), axis=1, keepdims=True)
            m_new = jnp.maximum(m, jnp.max(s, axis=1, keepdims=True) + mb)
            p = jnp.exp2(s + (mb - m_new))
            a = jnp.exp2(m - m_new)
            acc_ref[h] = a * acc_ref[h] + _dot(p.astype(BF16), ve_ref[h, pl.ds(start, blk), :])
            out.append(m_new)
        return tuple(out)

    lax.fori_loop(0, own, body, tuple(ms))
    for pp in range(nh // 2):
        acc0 = acc_ref[2 * pp]
        acc1 = acc_ref[2 * pp + 1]
        out = jnp.where(lane < HEAD_DIM, acc0 / acc0[:, HEAD_DIM:HEAD_DIM + 1], acc1 / acc1[:, 0:1])
        o_ref[:, pp * LANES:(pp + 1) * LANES] = out.astype(o_ref.dtype)


def _alibi_bias():
    slopes = 2.0 ** (-8.0 * jnp.arange(1, N_HEADS + 1, dtype=F32) / N_HEADS)
    qi = jnp.arange(MOBA_BLOCK, dtype=F32)[:, None]
    ki = jnp.arange(MOBA_BLOCK, dtype=F32)[None, :]
    return -(slopes * LOG2E)[:, None, None] * (qi - ki)[None]


def _moba_prompt(q, k, v):
    b, s, d = q.shape
    nb = s // MOBA_BLOCK
    nh = HEADS_PER_STEP
    w = nh * HEAD_DIM
    qspec = pl.BlockSpec((None, MOBA_BLOCK, w), lambda bi, cb, qb: (bi, qb, cb))
    kvspec = pl.BlockSpec((None, s, w), lambda bi, cb, qb: (bi, 0, cb))
    bspec = pl.BlockSpec((nh, MOBA_BLOCK, MOBA_BLOCK), lambda bi, cb, qb: (cb, 0, 0))
    return pl.pallas_call(
        _attn_kernel,
        grid=(b, d // w, nb),
        in_specs=[qspec, kvspec, kvspec, bspec],
        out_specs=qspec,
        out_shape=jax.ShapeDtypeStruct((b, s, d), BF16),
        scratch_shapes=[pltpu.VMEM((s, w), BF16), pltpu.VMEM((nh, s, LANES), BF16),
                        pltpu.VMEM((nh // 2, 2 * nb, LANES), F32),
                        pltpu.VMEM((nh // 2, MOBA_BLOCK, LANES), F32),
                        pltpu.VMEM((nh, MOBA_BLOCK, LANES), F32)],
        compiler_params=_params(3),
        name="moba_prompt",
    )(q, k, v, _alibi_bias())


def _attn_sample_kernel(n_pages, pt_ref, q_ref, qt_ref, kn_ref, vnt_ref, *refs):
    del pt_ref
    kp = refs[:n_pages]
    vp = refs[n_pages:2 * n_pages]
    o_ref = refs[2 * n_pages]
    s_ref, p_ref = refs[2 * n_pages + 1:]
    past = n_pages * PAGE_SIZE
    nb = past // MOBA_BLOCK
    scale = HEAD_DIM ** -0.5

    q = q_ref[...]
    qt = qt_ref[...]
    for h in range(N_HEADS):
        qb = jnp.broadcast_to(qt[:, h:h + 1], (HEAD_DIM, PAGE_SIZE))
        for p in range(n_pages):
            s_ref[h:h + 1, p * PAGE_SIZE:(p + 1) * PAGE_SIZE] = jnp.sum(kp[p][h] * qb, axis=0, keepdims=True)
    raw = s_ref[...]
    gates = [jnp.sum(raw[:, n * MOBA_BLOCK:(n + 1) * MOBA_BLOCK], axis=1, keepdims=True) * (1.0 / MOBA_BLOCK)
             for n in range(nb)]
    lane_blk = lax.broadcasted_iota(jnp.int32, (N_HEADS, past), 1) // MOBA_BLOCK
    keep = jnp.zeros((N_HEADS, past), jnp.bool_)
    for n in range(nb):
        cnt = jnp.zeros((N_HEADS, 1), F32)
        for j in range(nb):
            if j != n:
                beats = (gates[j] > gates[n]) | ((gates[j] == gates[n]) & (j < n))
                cnt = cnt + beats.astype(F32)
        keep = keep | ((lane_blk == n) & (cnt < MOBA_TOPK))
    head = lax.broadcasted_iota(jnp.int32, (N_HEADS, 1), 0)
    slope = jnp.exp2((head + 1).astype(F32) * (-8.0 / N_HEADS))
    kpos = lax.broadcasted_iota(jnp.int32, (1, past), 1).astype(F32)
    s = jnp.where(keep, raw * scale - slope * (float(past) - kpos), NEG_INF)
    s_own = jnp.sum(q * kn_ref[...], axis=1, keepdims=True) * scale
    m = jnp.maximum(jnp.max(s, axis=1, keepdims=True), s_own)
    e = jnp.exp(s - m)
    w_own = jnp.exp(s_own - m)
    inv = 1.0 / (jnp.sum(e, axis=1, keepdims=True) + w_own)
    p_ref[...] = e * inv
    w_own = w_own * inv
    lane_h = lax.broadcasted_iota(jnp.int32, (HEAD_DIM, N_HEADS), 1)
    out_t = jnp.zeros((HEAD_DIM, N_HEADS), F32)
    for h in range(N_HEADS):
        acc = jnp.zeros((HEAD_DIM, PAGE_SIZE), F32)
        for p in range(n_pages):
            acc = acc + vp[p][h] * p_ref[h:h + 1, p * PAGE_SIZE:(p + 1) * PAGE_SIZE]
        col = jnp.sum(acc, axis=1, keepdims=True) + w_own[h:h + 1, :] * vnt_ref[:, h:h + 1]
        out_t = jnp.where(lane_h == h, col, out_t)
    o_ref[...] = out_t


def _moba_sample(q, k_new, v_new, cache_k, cache_v, layer, page_table):
    n_seq, n_pages = page_table.shape
    hd = (N_HEADS, HEAD_DIM)
    dh = (HEAD_DIM, N_HEADS)
    ck = jnp.transpose(cache_k, (0, 1, 3, 4, 2))
    cv = jnp.transpose(cache_v, (0, 1, 3, 4, 2))
    pt = page_table.reshape(-1).astype(jnp.int32)
    q3 = q.reshape((n_seq,) + hd)
    hd_spec = pl.BlockSpec((None,) + hd, lambda i, pt: (i, 0, 0))
    dh_spec = pl.BlockSpec((None,) + dh, lambda i, pt: (i, 0, 0))

    def page_spec(p):
        return pl.BlockSpec((None, None) + hd + (PAGE_SIZE,),
                            lambda i, pt: (layer, pt[i * n_pages + p], 0, 0, 0))

    grid_spec = pltpu.PrefetchScalarGridSpec(
        num_scalar_prefetch=1,
        grid=(n_seq,),
        in_specs=[hd_spec, dh_spec, hd_spec, dh_spec] + [page_spec(p) for p in range(n_pages)] * 2,
        out_specs=dh_spec,
        scratch_shapes=[pltpu.VMEM((N_HEADS, n_pages * PAGE_SIZE), F32)] * 2,
    )
    out_t = pl.pallas_call(
        functools.partial(_attn_sample_kernel, n_pages),
        grid_spec=grid_spec,
        out_shape=jax.ShapeDtypeStruct((n_seq,) + dh, F32),
        compiler_params=_params(1),
        name="moba_sample",
    )(pt, q3, q3.transpose(0, 2, 1), k_new.reshape((n_seq,) + hd),
      v_new.reshape((n_seq,) + hd).transpose(0, 2, 1), *([ck] * n_pages), *([cv] * n_pages))
    return out_t.transpose(0, 2, 1).reshape(n_seq, D_MODEL).astype(BF16)


def _post_kernel(alpha, a_ref, x_ref, gm_ref, shf_ref, scf_ref, gf_ref, lng_ref, lnb_ref,
                 wo_ref, w1_ref, w3_ref, w2_ref, o_ref, acc_ref):
    m = _dot(a_ref[...], wo_ref[...])
    y1 = _layernorm(alpha * x_ref[...] + (1.0 + gm_ref[...]) * m, lng_ref[0:1, :], lnb_ref[0:1, :])
    h = (y1 * (1.0 + scf_ref[...]) + shf_ref[...]).astype(BF16)
    acc_ref[...] = jnp.zeros_like(acc_ref)

    def body(c, carry):
        t1 = _dot(h, w1_ref[c])
        t3 = _dot(h, w3_ref[c])
        acc_ref[...] += _dot((_silu(t1) * t3).astype(BF16), w2_ref[c])
        return carry

    lax.fori_loop(0, w1_ref.shape[0], body, 0)
    o_ref[...] = _layernorm(alpha * y1 + (1.0 + gf_ref[...]) * acc_ref[...],
                            lng_ref[1:2, :], lnb_ref[1:2, :])


def _post_attn(attn, x, gm, shf, scf, gf, ln_g, ln_b, wo, w1c, w3c, w2c, tm, alpha):
    t = x.shape[0]
    d = D_MODEL
    row = pl.BlockSpec((tm, d), lambda i: (i, 0))
    return pl.pallas_call(
        functools.partial(_post_kernel, alpha),
        grid=(t // tm,),
        in_specs=[row, row, gm[1], shf[1], scf[1], gf[1], _resident((2, d)), _resident((2, d)),
                  _resident(wo.shape), _resident(w1c.shape), _resident(w3c.shape), _resident(w2c.shape)],
        out_specs=row,
        out_shape=jax.ShapeDtypeStruct((t, d), F32),
        scratch_shapes=[pltpu.VMEM((tm, d), F32)],
        compiler_params=_params(1),
        name="post_attn_ffn",
    )(attn, x, gm[0], shf[0], scf[0], gf[0], ln_g, ln_b, wo, w1c, w3c, w2c)


def _glu(y, shm_ref, scm_ref, wpw1_ref, bpw1_ref):
    d = D_MODEL
    h = (y * (1.0 + scm_ref[...]) + shm_ref[...]).astype(BF16)
    a = _dot(h, wpw1_ref[:, 0:d]) + bpw1_ref[:, 0:d]
    g = _dot(h, wpw1_ref[:, d:2 * d]) + bpw1_ref[:, d:2 * d]
    return a * jax.nn.sigmoid(g)


def _conv_tail(alpha, y, conv, gm_ref, shf_ref, scf_ref, lng_ref, lnb_ref, clg_ref, clb_ref,
               wpw2_ref, bpw2_ref, wr_ref, br_ref, y3_ref, h2_ref, lg_ref):
    z = _silu(_layernorm(conv, clg_ref[...], clb_ref[...])).astype(BF16)
    m = _dot(z, wpw2_ref[...]) + bpw2_ref[...]
    y3 = _layernorm(alpha * y + (1.0 + gm_ref[...]) * m, lng_ref[0:1, :], lnb_ref[0:1, :])
    y3_ref[...] = y3
    h2 = y3 * (1.0 + scf_ref[...]) + shf_ref[...]
    _rows_to_tiles(h2_ref, h2)
    lg_ref[...] = _dot(h2, wr_ref[...], precision=HIGHEST) + br_ref[...]


def _conv_prompt_kernel(alpha, y_ref, shm_ref, scm_ref, gm_ref, shf_ref, scf_ref, lng_ref, lnb_ref,
                        wpw1_ref, bpw1_ref, wdw_ref, bdw_ref, clg_ref, clb_ref, wpw2_ref, bpw2_ref,
                        wr_ref, br_ref, y3_ref, h2_ref, lg_ref, tail_ref, ext_ref, shift_ref, conv_ref):
    tm = y_ref.shape[0]
    d = D_MODEL
    rows_ext = tm + HIST_PAD

    @pl.when(pl.program_id(1) == 0)
    def _():
        ext_ref[0:HIST_PAD, :] = jnp.zeros((HIST_PAD, d), F32)

    y = y_ref[...]
    u = _glu(y, shm_ref, scm_ref, wpw1_ref, bpw1_ref)
    ext_ref[HIST_PAD:rows_ext, :] = u
    tail_ref[...] = u[tm - HIST_PAD:tm, :]
    span = rows_ext - SUBLANES
    for b in range(1, SUBLANES):
        shift_ref[b - 1, :, :] = ext_ref[b:b + span, :]
    base = HIST_PAD - (CONV_WIDTH - 1)
    rb = 4 * SUBLANES

    def body(r, carry):
        r0 = pl.multiple_of(r * rb, rb)
        acc = jnp.broadcast_to(bdw_ref[...], (rb, d))
        for k in range(CONV_WIDTH):
            off = base + k
            start = r0 + (off // SUBLANES) * SUBLANES
            if off % SUBLANES == 0:
                x = ext_ref[pl.ds(start, rb), :]
            else:
                x = shift_ref[off % SUBLANES - 1, pl.ds(start, rb), :]
            acc = acc + wdw_ref[k:k + 1, :] * x
        conv_ref[pl.ds(r0, rb), :] = acc
        return carry

    lax.fori_loop(0, tm // rb, body, 0)
    ext_ref[0:HIST_PAD, :] = u[tm - HIST_PAD:tm, :]
    _conv_tail(alpha, y, conv_ref[...], gm_ref, shf_ref, scf_ref, lng_ref, lnb_ref, clg_ref, clb_ref,
               wpw2_ref, bpw2_ref, wr_ref, br_ref, y3_ref, h2_ref, lg_ref)


def _conv_sample_kernel(alpha, y_ref, shm_ref, scm_ref, gm_ref, shf_ref, scf_ref, lng_ref, lnb_ref,
                        wpw1_ref, bpw1_ref, wdw_ref, bdw_ref, clg_ref, clb_ref, wpw2_ref, bpw2_ref,
                        wr_ref, br_ref, hist_ref, y3_ref, h2_ref, lg_ref, u_ref):
    nh = CONV_WIDTH - 1
    y = y_ref[...]
    u = _glu(y, shm_ref, scm_ref, wpw1_ref, bpw1_ref)
    u_ref[...] = u
    w = wdw_ref[...]
    conv = bdw_ref[...] + w[nh:nh + 1, :] * u
    for k in range(nh):
        conv = conv + w[k:k + 1, :] * hist_ref[k]
    _conv_tail(alpha, y, conv, gm_ref, shf_ref, scf_ref, lng_ref, lnb_ref, clg_ref, clb_ref,
               wpw2_ref, bpw2_ref, wr_ref, br_ref, y3_ref, h2_ref, lg_ref)


def _conv_weight_specs(d):
    return [_resident((2, d)), _resident((2, d)), _resident((d, 2 * d)), _resident((1, 2 * d)),
            _resident((CONV_WIDTH, d)), _resident((1, d)), _resident((1, d)), _resident((1, d)),
            _resident((d, d)), _resident((1, d)), _resident((d, LANES)), _resident((1, LANES))]


def _conv_prompt(y, mods5, weights, n_seq, seq_len, alpha):
    d = D_MODEL
    tm = TM_CONV
    tps = seq_len // tm
    t = n_seq * seq_len
    row = pl.BlockSpec((tm, d), lambda b, i: (b * tps + i, 0))
    trow = pl.BlockSpec((tm * ROW_TILES, LANES), lambda b, i: (b * tps + i, 0))
    lgrow = pl.BlockSpec((tm, LANES), lambda b, i: (b * tps + i, 0))
    tail = pl.BlockSpec((None, HIST_PAD, d), lambda b, i: (b, 0, 0))
    return pl.pallas_call(
        functools.partial(_conv_prompt_kernel, alpha),
        grid=(n_seq, tps),
        in_specs=[row] + [m[1] for m in mods5] + _conv_weight_specs(d),
        out_specs=[row, trow, lgrow, tail],
        out_shape=[jax.ShapeDtypeStruct((t, d), F32),
                   jax.ShapeDtypeStruct((t * ROW_TILES, LANES), F32),
                   jax.ShapeDtypeStruct((t, LANES), F32),
                   jax.ShapeDtypeStruct((n_seq, HIST_PAD, d), F32)],
        scratch_shapes=[pltpu.VMEM((tm + HIST_PAD, d), F32),
                        pltpu.VMEM((SUBLANES - 1, tm + HIST_PAD - SUBLANES, d), F32),
                        pltpu.VMEM((tm, d), F32)],
        compiler_params=_params(2),
        name="conv_prompt",
    )(y, *[m[0] for m in mods5], *weights)


def _conv_sample(y, mods5, weights, hist, alpha):
    d = D_MODEL
    n = y.shape[0]
    tm = 32
    row = pl.BlockSpec((tm, d), lambda i: (i, 0))
    return pl.pallas_call(
        functools.partial(_conv_sample_kernel, alpha),
        grid=(n // tm,),
        in_specs=[row] + [m[1] for m in mods5] + _conv_weight_specs(d)
        + [pl.BlockSpec((CONV_WIDTH - 1, tm, d), lambda i: (0, i, 0))],
        out_specs=[row, pl.BlockSpec((tm * ROW_TILES, LANES), lambda i: (i, 0)),
                   pl.BlockSpec((tm, LANES), lambda i: (i, 0)), row],
        out_shape=[jax.ShapeDtypeStruct((n, d), F32),
                   jax.ShapeDtypeStruct((n * ROW_TILES, LANES), F32),
                   jax.ShapeDtypeStruct((n, LANES), F32),
                   jax.ShapeDtypeStruct((n, d), F32)],
        compiler_params=_params(1),
        name="conv_sample",
    )(y, *[m[0] for m in mods5], *weights, hist)


def _route_tile(lg_ref, meta_ref, meta_t_ref, carry_ref):
    tm = lg_ref.shape[0]
    lane = lax.broadcasted_iota(jnp.int32, (tm, LANES), 1)
    lg = jnp.where(lane < N_EXPERTS, lg_ref[...], NEG_INF)
    m1 = jnp.max(lg, axis=1, keepdims=True)
    i1 = jnp.min(jnp.where(lg == m1, lane, LANES), axis=1, keepdims=True)
    lg2 = jnp.where(lane == i1, NEG_INF, lg)
    m2 = jnp.max(lg2, axis=1, keepdims=True)
    i2 = jnp.min(jnp.where(lg2 == m2, lane, LANES), axis=1, keepdims=True)
    e = jnp.exp(m2 - m1)
    g1 = 1.0 / (1.0 + e)
    g2 = e / (1.0 + e)
    hot1 = lane == i1
    hot2 = lane == i2
    onehot = (hot1 | hot2).astype(BF16)
    ri = lax.broadcasted_iota(jnp.int32, (tm, tm), 0)
    ci = lax.broadcasted_iota(jnp.int32, (tm, tm), 1)
    before = (ci < ri).astype(BF16)
    rank = _dot(before, onehot) + carry_ref[...]
    r1 = jnp.sum(jnp.where(hot1, rank, 0.0), axis=1, keepdims=True)
    r2 = jnp.sum(jnp.where(hot2, rank, 0.0), axis=1, keepdims=True)
    cols = (i1.astype(F32), i2.astype(F32), g1, g2, r1, r2)
    meta = jnp.zeros((tm, LANES), F32)
    for c, val in enumerate(cols):
        meta = jnp.where(lane == c, val, meta)
    meta_ref[...] = meta
    meta_t_ref[...] = jnp.transpose(meta)[0:SUBLANES, :]
    carry_ref[...] += jnp.sum(onehot.astype(F32), axis=0, keepdims=True)


def _route_kernel(n_prompt_tiles, lgp_ref, lgs_ref, mp_ref, mpt_ref, ms_ref, mst_ref, cnt_ref, carry_ref):
    i = pl.program_id(0)

    @pl.when(i == 0)
    def _():
        carry_ref[...] = jnp.zeros_like(carry_ref)

    @pl.when(i < n_prompt_tiles)
    def _():
        _route_tile(lgp_ref, mp_ref, mpt_ref, carry_ref)

    @pl.when(i == n_prompt_tiles)
    def _():
        _route_tile(lgs_ref, ms_ref, mst_ref, carry_ref)
        cnt_ref[...] = carry_ref[...]


def _route(lg_p, lg_s):
    tp, ts = lg_p.shape[0], lg_s.shape[0]
    tm = TM_ROUTE
    npt = tp // tm
    last = npt - 1
    return pl.pallas_call(
        functools.partial(_route_kernel, npt),
        grid=(npt + 1,),
        in_specs=[pl.BlockSpec((tm, LANES), lambda i: (jnp.minimum(i, last), 0)),
                  pl.BlockSpec((ts, LANES), lambda i: (0, 0))],
        out_specs=[pl.BlockSpec((tm, LANES), lambda i: (jnp.minimum(i, last), 0)),
                   pl.BlockSpec((SUBLANES, tm), lambda i: (0, jnp.minimum(i, last))),
                   pl.BlockSpec((ts, LANES), lambda i: (0, 0)),
                   pl.BlockSpec((SUBLANES, ts), lambda i: (0, 0)),
                   pl.BlockSpec((1, LANES), lambda i: (0, 0))],
        out_shape=[jax.ShapeDtypeStruct((tp, LANES), F32),
                   jax.ShapeDtypeStruct((SUBLANES, tp), F32),
                   jax.ShapeDtypeStruct((ts, LANES), F32),
                   jax.ShapeDtypeStruct((SUBLANES, ts), F32),
                   jax.ShapeDtypeStruct((1, LANES), F32)],
        scratch_shapes=[pltpu.VMEM((1, LANES), F32)],
        compiler_params=_params(1),
        name="moe_route",
    )(lg_p, lg_s)


def _row_copy(src_ref, src_row, dst_ref, dst_row, sem, n=1):
    return pltpu.make_async_copy(
        src_ref.at[pl.ds(pl.multiple_of(src_row * ROW_TILES, ROW_TILES), n * ROW_TILES), :],
        dst_ref.at[pl.ds(pl.multiple_of(dst_row * ROW_TILES, ROW_TILES), n * ROW_TILES), :], sem)


def _scatter_rows(slots_ref, h_ref, xs_ref, sem):
    tm = h_ref.shape[0] // ROW_TILES

    def issue(r, carry):
        for c in range(MOE_TOPK):
            _row_copy(h_ref, r, xs_ref, slots_ref[0, c * tm + r], sem).start()
        return carry

    lax.fori_loop(0, tm, issue, 0)
    for c in range(MOE_TOPK):
        _row_copy(h_ref, 0, xs_ref, 0, sem, n=tm).wait()


def _scatter_kernel(n_prompt_tiles, zlo_ref, zhi_ref, sp_ref, ss_ref, hp_ref, hs_ref, xs_ref,
                    zero_ref, sem, zsem):
    i = pl.program_id(0)

    @pl.when(i < n_prompt_tiles)
    def _():
        _scatter_rows(sp_ref, hp_ref, xs_ref, sem)

    @pl.when(i == n_prompt_tiles)
    def _():
        _scatter_rows(ss_ref, hs_ref, xs_ref, sem)
        zero_ref[...] = jnp.zeros_like(zero_ref)
        for e in range(N_EXPERTS):
            lo = zlo_ref[e]
            n = zhi_ref[e] - lo
            nbig = n // ZERO_ROWS

            def big(k, carry, lo=lo):
                return _row_copy(zero_ref, 0, xs_ref, lo + k * ZERO_ROWS, zsem, n=ZERO_ROWS)

            def small(r, carry, lo=lo):
                return _row_copy(zero_ref, 0, xs_ref, lo + r, zsem)

            lax.fori_loop(0, nbig, lambda k, c: (big(k, c).start(), c)[1], 0)
            lax.fori_loop(nbig * ZERO_ROWS, n, lambda r, c: (small(r, c).start(), c)[1], 0)
            lax.fori_loop(0, nbig, lambda k, c: (big(k, c).wait(), c)[1], 0)
            lax.fori_loop(nbig * ZERO_ROWS, n, lambda r, c: (small(r, c).wait(), c)[1], 0)


def _scatter(h2t_p, h2t_s, slots_p, slots_s, zero_lo, zero_hi, n_rows):
    tp = h2t_p.shape[0] // ROW_TILES
    ts = h2t_s.shape[0] // ROW_TILES
    tm = TM_ROUTE
    npt = tp // tm
    last = npt - 1
    grid_spec = pltpu.PrefetchScalarGridSpec(
        num_scalar_prefetch=2,
        grid=(npt + 1,),
        in_specs=[pl.BlockSpec((None, 1, MOE_TOPK * tm), lambda i, lo, hi: (jnp.minimum(i, last), 0, 0),
                               memory_space=pltpu.SMEM),
                  pl.BlockSpec((None, 1, MOE_TOPK * ts), lambda i, lo, hi: (0, 0, 0), memory_space=pltpu.SMEM),
                  pl.BlockSpec((tm * ROW_TILES, LANES), lambda i, lo, hi: (jnp.minimum(i, last), 0)),
                  pl.BlockSpec((ts * ROW_TILES, LANES), lambda i, lo, hi: (0, 0))],
        out_specs=pl.BlockSpec(memory_space=pl.ANY),
        scratch_shapes=[pltpu.VMEM((ZERO_ROWS * ROW_TILES, LANES), F32), pltpu.SemaphoreType.DMA(()),
                        pltpu.SemaphoreType.DMA(())],
    )
    return pl.pallas_call(
        functools.partial(_scatter_kernel, npt),
        grid_spec=grid_spec,
        out_shape=jax.ShapeDtypeStruct((n_rows * ROW_TILES, LANES), F32),
        compiler_params=pltpu.CompilerParams(dimension_semantics=("arbitrary",),
                                             vmem_limit_bytes=VMEM_LIMIT, has_side_effects=True),
        name="moe_scatter",
    )(zero_lo, zero_hi, slots_p, slots_s, h2t_p, h2t_s)


def _gmm_kernel(exp_ref, nv_ref, x_ref, w1_ref, w3_ref, w2_ref, o_ref, xb_ref, acc_ref):
    del exp_ref
    m = pl.program_id(0)
    f = pl.program_id(1)
    nf = pl.num_programs(1)
    tg = xb_ref.shape[0]
    live = m < nv_ref[0]

    @pl.when(live)
    def _():
        @pl.when(f == 0)
        def _():
            xb_ref[...] = _tiles_to_rows(x_ref, tg).astype(BF16)

        x = xb_ref[...]
        t1 = _dot(x, w1_ref[...])
        t3 = _dot(x, w3_ref[...])
        part = _dot((_silu(t1) * t3).astype(BF16), w2_ref[...])

        @pl.when(f == 0)
        def _():
            acc_ref[...] = part

        @pl.when(f > 0)
        def _():
            acc_ref[...] += part

        @pl.when(f == nf - 1)
        def _():
            _rows_to_tiles(o_ref, acc_ref[...])

    @pl.when(jnp.logical_not(live) & (f == nf - 1))
    def _():
        o_ref[...] = jnp.zeros_like(o_ref)


def _gmm(xs, w1, w3, w2, tile_exp, n_valid):
    d = D_MODEL
    tg = TM_GMM
    tf = TF_GMM
    dfe = w1.shape[2]
    nf = dfe // tf
    n_tiles = tile_exp.shape[0]

    def fidx(m, f, nv):
        return jnp.where(m < nv[0], f, nf - 1)

    grid_spec = pltpu.PrefetchScalarGridSpec(
        num_scalar_prefetch=2,
        grid=(n_tiles, nf),
        in_specs=[
            pl.BlockSpec((tg * ROW_TILES, LANES), lambda m, f, ex, nv: (m, 0)),
            pl.BlockSpec((None, d, tf), lambda m, f, ex, nv: (ex[m], 0, fidx(m, f, nv))),
            pl.BlockSpec((None, d, tf), lambda m, f, ex, nv: (ex[m], 0, fidx(m, f, nv))),
            pl.BlockSpec((None, tf, d), lambda m, f, ex, nv: (ex[m], fidx(m, f, nv), 0)),
        ],
        out_specs=pl.BlockSpec((tg * ROW_TILES, LANES), lambda m, f, ex, nv: (m, 0)),
        scratch_shapes=[pltpu.VMEM((tg, d), BF16), pltpu.VMEM((tg, d), F32)],
    )
    return pl.pallas_call(
        _gmm_kernel,
        grid_spec=grid_spec,
        out_shape=jax.ShapeDtypeStruct(xs.shape, F32),
        compiler_params=_params(2),
        name="moe_experts",
    )(tile_exp, n_valid, xs, w1, w3, w2)


def _combine_kernel(alpha, slots_ref, ys_ref, y_ref, meta_ref, gf_ref, lng_ref, lnb_ref, o_ref,
                    buf_ref, sem):
    tm = y_ref.shape[0]

    def issue(r, carry):
        for c in range(MOE_TOPK):
            _row_copy(ys_ref, slots_ref[0, c * tm + r], buf_ref.at[c], r, sem).start()
        return carry

    lax.fori_loop(0, tm, issue, 0)
    for c in range(MOE_TOPK):
        _row_copy(ys_ref, 0, buf_ref.at[c], 0, sem, n=tm).wait()
    meta = meta_ref[...]
    f = jnp.zeros((tm, D_MODEL), F32)
    for c in range(MOE_TOPK):
        f = f + _tiles_to_rows(buf_ref.at[c], tm) * meta[:, META_GATE + c:META_GATE + c + 1]
    o_ref[...] = _layernorm(alpha * y_ref[...] + (1.0 + gf_ref[...]) * f,
                            lng_ref[1:2, :], lnb_ref[1:2, :])


def _combine(ys, y3, slots3, meta, gf, ln_g, ln_b, tm, alpha):
    t, d = y3.shape
    return pl.pallas_call(
        functools.partial(_combine_kernel, alpha),
        grid=(t // tm,),
        in_specs=[pl.BlockSpec((None, 1, MOE_TOPK * tm), lambda i: (i, 0, 0), memory_space=pltpu.SMEM),
                  pl.BlockSpec(memory_space=pl.ANY),
                  pl.BlockSpec((tm, d), lambda i: (i, 0)),
                  pl.BlockSpec((tm, LANES), lambda i: (i, 0)),
                  gf[1], _resident((2, d)), _resident((2, d))],
        out_specs=pl.BlockSpec((tm, d), lambda i: (i, 0)),
        out_shape=jax.ShapeDtypeStruct((t, d), F32),
        scratch_shapes=[pltpu.VMEM((MOE_TOPK, tm * ROW_TILES, LANES), F32), pltpu.SemaphoreType.DMA(())],
        compiler_params=_params(1),
        name="moe_combine",
    )(slots3, ys, y3, meta, gf[0], ln_g, ln_b)


def _tile_slots(slots, tm):
    k, t = slots.shape
    return slots.reshape(k, t // tm, tm).transpose(1, 0, 2).reshape(t // tm, 1, k * tm)


def kernel(x_prompt, x_sample, cache_k, cache_v, state_conv, page_table, c_prompt, c_sample,
           ada_w, ada_b, ln_g, ln_b, attn_w_qkv, attn_w_o,
           conv_w_pw1, conv_b_pw1, conv_w_dw, conv_b_dw, conv_ln_g, conv_ln_b, conv_w_pw2, conv_b_pw2,
           ffn_w1, ffn_w3, ffn_w2, moe_w_router, moe_b_router, moe_w1, moe_w3, moe_w2):
    b, s, d = x_prompt.shape
    db, ds, _ = x_sample.shape
    depth = ada_w.shape[0]
    assert d == D_MODEL and ds == 1 and depth == 2
    assert page_table.shape[1] * PAGE_SIZE % MOBA_BLOCK == 0
    alpha = (2 * depth) ** 0.25
    tp = b * s
    t_all = tp + db
    assert tp % TM_ROUTE == 0 and tp % TM_COMBINE == 0

    c_all = jnp.concatenate([c_sample, c_prompt], axis=0)
    mods = _adaln(c_all, ada_w, ada_b)
    mods = _Mods(mods.reshape(depth * 6, db + b, d), db)

    def pmod(layer, k, tm):
        tps = s // tm
        return mods.prompt(layer * 6 + k, lambda i: i // tps)

    def smod(layer, k, rows=None, block_of_step=lambda i: 0):
        return mods.sample(layer * 6 + k, db if rows is None else rows, block_of_step)

    xp = x_prompt.reshape(tp, d)
    xs_ = x_sample.reshape(db, d)

    wqkv = attn_w_qkv[0].astype(BF16)
    qp, kp, vp = _qkv(xp, pmod(0, 0, TM_QKV), pmod(0, 1, TM_QKV), wqkv, TM_QKV)
    qs, ks, vs = _qkv(xs_, smod(0, 0), smod(0, 1), wqkv, db)
    attn_p = _moba_prompt(qp.reshape(b, s, d), kp.reshape(b, s, d), vp.reshape(b, s, d)).reshape(tp, d)
    attn_s = _moba_sample(qs, ks, vs, cache_k, cache_v, 0, page_table)

    dff = ffn_w1.shape[2]
    nck = dff // FFN_CHUNK
    wo = attn_w_o[0].astype(BF16)
    w1c = ffn_w1[0].astype(BF16).reshape(d, nck, FFN_CHUNK).transpose(1, 0, 2)
    w3c = ffn_w3[0].astype(BF16).reshape(d, nck, FFN_CHUNK).transpose(1, 0, 2)
    w2c = ffn_w2[0].astype(BF16).reshape(nck, FFN_CHUNK, d)
    y_p = _post_attn(attn_p, xp, pmod(0, 2, TM_POST), pmod(0, 3, TM_POST), pmod(0, 4, TM_POST),
                     pmod(0, 5, TM_POST), ln_g[0], ln_b[0], wo, w1c, w3c, w2c, TM_POST, alpha)
    y_s = _post_attn(attn_s, xs_, smod(0, 2), smod(0, 3), smod(0, 4), smod(0, 5),
                     ln_g[0], ln_b[0], wo, w1c, w3c, w2c, db, alpha)

    wr = jnp.pad(moe_w_router[0], ((0, 0), (0, LANES - N_EXPERTS)))
    br = jnp.pad(moe_b_router[0], (0, LANES - N_EXPERTS)).reshape(1, LANES)
    conv_weights = (ln_g[1], ln_b[1], conv_w_pw1[0].astype(BF16), conv_b_pw1[0].reshape(1, 2 * d),
                    conv_w_dw[0], conv_b_dw[0].reshape(1, d), conv_ln_g[0].reshape(1, d),
                    conv_ln_b[0].reshape(1, d), conv_w_pw2[0].astype(BF16), conv_b_pw2[0].reshape(1, d),
                    wr, br)
    pm = [mods.prompt(6 + k, lambda bi, i: bi) for k in range(5)]
    y3_p, h2t_p, lg_p, tail = _conv_prompt(y_p, pm, conv_weights, b, s, alpha)
    sm = [smod(1, k, 32, lambda i: i) for k in range(5)]
    hist_t = jnp.transpose(state_conv[0], (1, 0, 2))
    y3_s, h2t_s, lg_s, u_s = _conv_sample(y_s, sm, conv_weights, hist_t, alpha)

    meta_p, meta_pt, meta_s, meta_st, cnt = _route(lg_p, lg_s)
    counts = cnt[0, 0:N_EXPERTS].astype(jnp.int32)
    padded = (counts + TM_GMM - 1) // TM_GMM * TM_GMM
    pad_end = jnp.cumsum(padded)
    pad_start = pad_end - padded
    n_tiles = -(-t_all * MOE_TOPK // TM_GMM) + N_EXPERTS
    n_rows = n_tiles * TM_GMM

    def slots_of(meta_t):
        e = meta_t[META_EXPERT:META_EXPERT + MOE_TOPK].astype(jnp.int32)
        r = meta_t[META_RANK:META_RANK + MOE_TOPK].astype(jnp.int32)
        return pad_start[e] + r

    slots_p, slots_s = slots_of(meta_pt), slots_of(meta_st)
    zero_lo = pad_start + counts
    zero_hi = jnp.concatenate([pad_start[1:], jnp.full((1,), n_rows, jnp.int32)])
    xs_buf = _scatter(h2t_p, h2t_s, _tile_slots(slots_p, TM_ROUTE), _tile_slots(slots_s, db),
                      zero_lo.astype(jnp.int32), zero_hi.astype(jnp.int32), n_rows)

    tile_ids = jnp.arange(n_tiles, dtype=jnp.int32)
    tile_exp = jnp.minimum(jnp.sum((pad_end // TM_GMM)[None, :] <= tile_ids[:, None], axis=1),
                           N_EXPERTS - 1).astype(jnp.int32)
    n_valid = (pad_end[-1] // TM_GMM).reshape(1).astype(jnp.int32)
    ys_buf = _gmm(xs_buf, moe_w1[0].astype(BF16), moe_w3[0].astype(BF16), moe_w2[0].astype(BF16),
                  tile_exp, n_valid)

    out_p = _combine(ys_buf, y3_p, _tile_slots(slots_p, TM_COMBINE), meta_p,
                     pmod(1, 5, TM_COMBINE), ln_g[1], ln_b[1], TM_COMBINE, alpha)
    out_s = _combine(ys_buf, y3_s, _tile_slots(slots_s, db), meta_s,
                     smod(1, 5), ln_g[1], ln_b[1], db, alpha)

    nh = CONV_WIDTH - 1
    new_conv_p = tail[:, HIST_PAD - nh:, :][None]
    new_conv_s = jnp.transpose(jnp.concatenate([hist_t[1:], u_s[None]], axis=0), (1, 0, 2))[None]
    hshape = (N_HEADS, HEAD_DIM)
    return (out_p.reshape(b, s, d), out_s.reshape(db, 1, d),
            kp.reshape(1, b, s, *hshape), vp.reshape(1, b, s, *hshape),
            ks.reshape(1, db, 1, *hshape), vs.reshape(1, db, 1, *hshape),
            new_conv_p, new_conv_s)
```

```python
import functools
import math

import jax
import jax.numpy as jnp
from jax import lax
from jax.experimental import pallas as pl
from jax.experimental.pallas import tpu as pltpu

F32 = jnp.float32
BF16 = jnp.bfloat16
HIGHEST = lax.Precision.HIGHEST
NEG_INF = float("-inf")
LOG2E = math.log2(math.e)

SUBLANES = 8
LANES = 128

D_MODEL = 1024
N_HEADS = 16
HEAD_DIM = D_MODEL // N_HEADS
HEADS_PER_STEP = 8
MOBA_BLOCK = 256
MOBA_TOPK = 3
PAGE_SIZE = 128
CONV_WIDTH = 31
HIST_PAD = 32
N_EXPERTS = 8
MOE_TOPK = 2
LN_EPS = 1e-5
ROW_TILES = D_MODEL // LANES

TM_QKV = 512
TM_POST = 512
TM_CONV = 512
TM_ROUTE = 512
TM_GMM = 512
TM_COMBINE = 256
TF_GMM = 512
FFN_CHUNK = 256
ZERO_ROWS = 64
VMEM_LIMIT = 56 * 1024 * 1024

META_EXPERT, META_GATE, META_RANK = 0, 2, 4

NT_DIMS = (((1,), (1,)), ((), ()))


def _dot(a, b, precision=None):
    return jnp.dot(a, b, preferred_element_type=F32, precision=precision)


def _dot_nt(a, b, precision=None):
    return lax.dot_general(a, b, NT_DIMS, preferred_element_type=F32, precision=precision)


def _silu(x):
    return x * jax.nn.sigmoid(x)


def _layernorm(z, g, b):
    mu = jnp.mean(z, axis=-1, keepdims=True)
    zc = z - mu
    var = jnp.mean(zc * zc, axis=-1, keepdims=True)
    return zc * lax.rsqrt(var + LN_EPS) * g + b


def _resident(shape):
    nd = len(shape)
    return pl.BlockSpec(shape, lambda *_: (0,) * nd, pipeline_mode=pl.Buffered(1))


def _params(n_grid_dims):
    return pltpu.CompilerParams(
        dimension_semantics=("arbitrary",) * n_grid_dims, vmem_limit_bytes=VMEM_LIMIT)


def _rows_to_tiles(ref, x):
    t = x.shape[0]
    for s in range(ROW_TILES):
        ref[pl.ds(s, t, stride=ROW_TILES), :] = x[:, s * LANES:(s + 1) * LANES]


def _tiles_to_rows(ref, t):
    return jnp.concatenate([ref[pl.ds(s, t, stride=ROW_TILES), :] for s in range(ROW_TILES)], axis=1)


def _ada_kernel(c_ref, w_ref, b_ref, o_ref):
    c = c_ref[...]
    o_ref[...] = _dot(_silu(c).astype(BF16), w_ref[...].astype(BF16)) + b_ref[...]


def _adaln(c_all, ada_w, ada_b):
    depth = ada_w.shape[0]
    n = c_all.shape[0]
    d = D_MODEL
    return pl.pallas_call(
        _ada_kernel,
        grid=(depth, 6),
        in_specs=[
            pl.BlockSpec((n, d), lambda l, k: (0, 0)),
            pl.BlockSpec((None, d, d), lambda l, k: (l, 0, k)),
            pl.BlockSpec((None, None, 1, d), lambda l, k: (l, k, 0, 0)),
        ],
        out_specs=pl.BlockSpec((None, None, n, d), lambda l, k: (l, k, 0, 0)),
        out_shape=jax.ShapeDtypeStruct((depth, 6, n, d), F32),
        compiler_params=_params(2),
        name="adaln",
    )(c_all, ada_w, ada_b.reshape(depth, 6, 1, d))


class _Mods:
    def __init__(self, mods, n_sample):
        self.n_sample = n_sample
        self.m3 = mods
        self.m4 = mods.reshape(mods.shape[0], mods.shape[1], 1, D_MODEL)

    def prompt(self, lk, seq_of_step):
        ns = self.n_sample
        return self.m4, pl.BlockSpec(
            (None, None, 1, D_MODEL), lambda *g: (lk, ns + seq_of_step(*g), 0, 0))

    def sample(self, lk, rows, block_of_step):
        return self.m3, pl.BlockSpec((None, rows, D_MODEL), lambda *g: (lk, block_of_step(*g), 0))


def _qkv_kernel(x_ref, sh_ref, sc_ref, w_ref, q_ref, k_ref, v_ref):
    d = D_MODEL
    h = (x_ref[...] * (1.0 + sc_ref[...]) + sh_ref[...]).astype(BF16)
    q_ref[...] = _dot(h, w_ref[:, 0:d])
    k_ref[...] = _dot(h, w_ref[:, d:2 * d])
    v_ref[...] = _dot(h, w_ref[:, 2 * d:3 * d])


def _qkv(x, sh, sc, w_bf16, tm):
    t = x.shape[0]
    d = D_MODEL
    row = pl.BlockSpec((tm, d), lambda i: (i, 0))
    out = jax.ShapeDtypeStruct((t, d), F32)
    return pl.pallas_call(
        _qkv_kernel,
        grid=(t // tm,),
        in_specs=[row, sh[1], sc[1], _resident((d, 3 * d))],
        out_specs=[row, row, row],
        out_shape=[out, out, out],
        compiler_params=_params(1),
        name="qkv",
    )(x, sh[0], sc[0], w_bf16)


def _select_blocks(gate_t, n_valid):
    nb = gate_t.shape[0]
    blk = lax.broadcasted_iota(jnp.int32, gate_t.shape, 0)
    valid = blk < n_valid
    rows = []
    for n in range(nb):
        gn = gate_t[n:n + 1, :]
        beats = ((gate_t > gn) | ((gate_t == gn) & (blk < n))) & valid
        rows.append(jnp.sum(beats.astype(F32), axis=0, keepdims=True))
    cnt = jnp.concatenate(rows, axis=0)
    return (valid & (cnt < MOBA_TOPK)).astype(F32)


def _attn_kernel(q_ref, k_ref, v_ref, bias_ref, o_ref, kb_ref, ve_ref, km_ref, mb_ref, acc_ref):
    blk = MOBA_BLOCK
    s_len = k_ref.shape[0]
    nb = s_len // blk
    nh = HEADS_PER_STEP
    cb = pl.program_id(1)
    own = pl.program_id(2)

    @pl.when(own == 0)
    def _():
        kb_ref[...] = k_ref[...].astype(BF16)
        lane_s = lax.broadcasted_iota(jnp.int32, (s_len, LANES), 1)
        lane_k = lax.broadcasted_iota(jnp.int32, (nb, LANES), 1) // HEAD_DIM
        for pp in range(nh // 2):
            cols = slice(pp * LANES, (pp + 1) * LANES)
            rows = [jnp.mean(k_ref[n * blk:(n + 1) * blk, cols], axis=0, keepdims=True) for n in range(nb)]
            km = jnp.concatenate(rows, axis=0)
            km_ref[pp] = jnp.concatenate([jnp.where(lane_k == 0, km, 0.0), jnp.where(lane_k == 1, km, 0.0)], axis=0)
            v = v_ref[:, cols]
            ve_ref[2 * pp] = jnp.where(lane_s < HEAD_DIM, v, jnp.where(lane_s == HEAD_DIM, 1.0, 0.0)).astype(BF16)
            ve_ref[2 * pp + 1] = jnp.where(lane_s >= HEAD_DIM, v, jnp.where(lane_s == 0, 1.0, 0.0)).astype(BF16)

    qi = lax.broadcasted_iota(jnp.int32, (blk, blk), 0)
    ki = lax.broadcasted_iota(jnp.int32, (blk, blk), 1)
    causal = qi >= ki
    eye = (qi == ki).astype(BF16)
    lane = lax.broadcasted_iota(jnp.int32, (blk, LANES), 1)
    lane_head = lane // HEAD_DIM
    blk_off = (own - lane % nb).astype(F32) * float(blk)
    own_start = pl.multiple_of(own * blk, blk)

    qh = []
    for pp in range(nh // 2):
        q = q_ref[:, pp * LANES:(pp + 1) * LANES]
        qs_all = (q * (HEAD_DIM ** -0.5 * LOG2E)).astype(BF16)
        gates = _dot_nt(km_ref[pp], q, precision=HIGHEST)
        sel_t = jnp.concatenate([_select_blocks(gates[hh * nb:(hh + 1) * nb, :], own) for hh in range(2)]
                                + [jnp.zeros((LANES - 2 * nb, blk), F32)], axis=0).astype(BF16)
        sel = _dot_nt(eye, sel_t)
        head_no = (cb * nh + 2 * pp + 1 + (lane >= nb).astype(jnp.int32)).astype(F32)
        slope2 = jnp.exp2(head_no * (-8.0 / N_HEADS)) * LOG2E
        mb_ref[pp] = jnp.where(sel > 0.5, -slope2 * blk_off, NEG_INF)
        for hh in range(2):
            qh.append(jnp.where(lane_head == hh, qs_all, jnp.zeros_like(qs_all)))

    ms = []
    for h in range(nh):
        pp = h // 2
        kd = kb_ref[pl.ds(own_start, blk), pp * LANES:(pp + 1) * LANES]
        s = jnp.where(causal, _dot_nt(qh[h], kd) + bias_ref[h], NEG_INF)
        m = jnp.max(s, axis=1, keepdims=True)
        p = jnp.exp2(s - m)
        acc_ref[h] = _dot(p.astype(BF16), ve_ref[h, pl.ds(own_start, blk), :])
        ms.append(m)

    def body(j, carry):
        start = pl.multiple_of(j * blk, blk)
        out = []
        for h in range(nh):
            pp, hh = h // 2, h % 2
            kj = kb_ref[pl.ds(start, blk), pp * LANES:(pp + 1) * LANES]
            m = carry[h]
            s = _dot_nt(qh[h], kj) + bias_ref[h]
            mb = jnp.sum(jnp.where(lane == j + hh * nb, mb_ref[pp], 0.0), axis=1, keepdims=True)
            m_new = jnp.maximum(m, jnp.max(s, axis=1, keepdims=True) + mb)
            p = jnp.exp2(s + (mb - m_new))
            a = jnp.exp2(m - m_new)
            acc_ref[h] = a * acc_ref[h] + _dot(p.astype(BF16), ve_ref[h, pl.ds(start, blk), :])
            out.append(m_new)
        return tuple(out)

    lax.fori_loop(0, own, body, tuple(ms))
    for pp in range(nh // 2):
        acc0 = acc_ref[2 * pp]
        acc1 = acc_ref[2 * pp + 1]
        out = jnp.where(lane < HEAD_DIM, acc0 / acc0[:, HEAD_DIM:HEAD_DIM + 1], acc1 / acc1[:, 0:1])
        o_ref[:, pp * LANES:(pp + 1) * LANES] = out.astype(o_ref.dtype)


def _alibi_bias():
    slopes = 2.0 ** (-8.0 * jnp.arange(1, N_HEADS + 1, dtype=F32) / N_HEADS)
    qi = jnp.arange(MOBA_BLOCK, dtype=F32)[:, None]
    ki = jnp.arange(MOBA_BLOCK, dtype=F32)[None, :]
    return -(slopes * LOG2E)[:, None, None] * (qi - ki)[None]


def _moba_prompt(q, k, v):
    b, s, d = q.shape
    nb = s // MOBA_BLOCK
    nh = HEADS_PER_STEP
    w = nh * HEAD_DIM
    qspec = pl.BlockSpec((None, MOBA_BLOCK, w), lambda bi, cb, qb: (bi, qb, cb))
    kvspec = pl.BlockSpec((None, s, w), lambda bi, cb, qb: (bi, 0, cb))
    bspec = pl.BlockSpec((nh, MOBA_BLOCK, MOBA_BLOCK), lambda bi, cb, qb: (cb, 0, 0))
    return pl.pallas_call(
        _attn_kernel,
        grid=(b, d // w, nb),
        in_specs=[qspec, kvspec, kvspec, bspec],
        out_specs=qspec,
        out_shape=jax.ShapeDtypeStruct((b, s, d), BF16),
        scratch_shapes=[pltpu.VMEM((s, w), BF16), pltpu.VMEM((nh, s, LANES), BF16),
                        pltpu.VMEM((nh // 2, 2 * nb, LANES), F32),
                        pltpu.VMEM((nh // 2, MOBA_BLOCK, LANES), F32),
                        pltpu.VMEM((nh, MOBA_BLOCK, LANES), F32)],
        compiler_params=_params(3),
        name="moba_prompt",
    )(q, k, v, _alibi_bias())


def _attn_sample_kernel(n_pages, pt_ref, q_ref, qt_ref, kn_ref, vnt_ref, *refs):
    del pt_ref
    kp = refs[:n_pages]
    vp = refs[n_pages:2 * n_pages]
    o_ref = refs[2 * n_pages]
    s_ref, p_ref = refs[2 * n_pages + 1:]
    past = n_pages * PAGE_SIZE
    nb = past // MOBA_BLOCK
    scale = HEAD_DIM ** -0.5

    q = q_ref[...]
    qt = qt_ref[...]
    for h in range(N_HEADS):
        qb = jnp.broadcast_to(qt[:, h:h + 1], (HEAD_DIM, PAGE_SIZE))
        for p in range(n_pages):
            s_ref[h:h + 1, p * PAGE_SIZE:(p + 1) * PAGE_SIZE] = jnp.sum(kp[p][h] * qb, axis=0, keepdims=True)
    raw = s_ref[...]
    gates = [jnp.sum(raw[:, n * MOBA_BLOCK:(n + 1) * MOBA_BLOCK], axis=1, keepdims=True) * (1.0 / MOBA_BLOCK)
             for n in range(nb)]
    lane_blk = lax.broadcasted_iota(jnp.int32, (N_HEADS, past), 1) // MOBA_BLOCK
    keep = jnp.zeros((N_HEADS, past), jnp.bool_)
    for n in range(nb):
        cnt = jnp.zeros((N_HEADS, 1), F32)
        for j in range(nb):
            if j != n:
                beats = (gates[j] > gates[n]) | ((gates[j] == gates[n]) & (j < n))
                cnt = cnt + beats.astype(F32)
        keep = keep | ((lane_blk == n) & (cnt < MOBA_TOPK))
    head = lax.broadcasted_iota(jnp.int32, (N_HEADS, 1), 0)
    slope = jnp.exp2((head + 1).astype(F32) * (-8.0 / N_HEADS))
    kpos = lax.broadcasted_iota(jnp.int32, (1, past), 1).astype(F32)
    s = jnp.where(keep, raw * scale - slope * (float(past) - kpos), NEG_INF)
    s_own = jnp.sum(q * kn_ref[...], axis=1, keepdims=True) * scale
    m = jnp.maximum(jnp.max(s, axis=1, keepdims=True), s_own)
    e = jnp.exp(s - m)
    w_own = jnp.exp(s_own - m)
    inv = 1.0 / (jnp.sum(e, axis=1, keepdims=True) + w_own)
    p_ref[...] = e * inv
    w_own = w_own * inv
    lane_h = lax.broadcasted_iota(jnp.int32, (HEAD_DIM, N_HEADS), 1)
    out_t = jnp.zeros((HEAD_DIM, N_HEADS), F32)
    for h in range(N_HEADS):
        acc = jnp.zeros((HEAD_DIM, PAGE_SIZE), F32)
        for p in range(n_pages):
            acc = acc + vp[p][h] * p_ref[h:h + 1, p * PAGE_SIZE:(p + 1) * PAGE_SIZE]
        col = jnp.sum(acc, axis=1, keepdims=True) + w_own[h:h + 1, :] * vnt_ref[:, h:h + 1]
        out_t = jnp.where(lane_h == h, col, out_t)
    o_ref[...] = out_t


def _moba_sample(q, k_new, v_new, cache_k, cache_v, layer, page_table):
    n_seq, n_pages = page_table.shape
    hd = (N_HEADS, HEAD_DIM)
    dh = (HEAD_DIM, N_HEADS)
    ck = jnp.transpose(cache_k, (0, 1, 3, 4, 2))
    cv = jnp.transpose(cache_v, (0, 1, 3, 4, 2))
    pt = page_table.reshape(-1).astype(jnp.int32)
    q3 = q.reshape((n_seq,) + hd)
    hd_spec = pl.BlockSpec((None,) + hd, lambda i, pt: (i, 0, 0))
    dh_spec = pl.BlockSpec((None,) + dh, lambda i, pt: (i, 0, 0))

    def page_spec(p):
        return pl.BlockSpec((None, None) + hd + (PAGE_SIZE,),
                            lambda i, pt: (layer, pt[i * n_pages + p], 0, 0, 0))

    grid_spec = pltpu.PrefetchScalarGridSpec(
        num_scalar_prefetch=1,
        grid=(n_seq,),
        in_specs=[hd_spec, dh_spec, hd_spec, dh_spec] + [page_spec(p) for p in range(n_pages)] * 2,
        out_specs=dh_spec,
        scratch_shapes=[pltpu.VMEM((N_HEADS, n_pages * PAGE_SIZE), F32)] * 2,
    )
    out_t = pl.pallas_call(
        functools.partial(_attn_sample_kernel, n_pages),
        grid_spec=grid_spec,
        out_shape=jax.ShapeDtypeStruct((n_seq,) + dh, F32),
        compiler_params=_params(1),
        name="moba_sample",
    )(pt, q3, q3.transpose(0, 2, 1), k_new.reshape((n_seq,) + hd),
      v_new.reshape((n_seq,) + hd).transpose(0, 2, 1), *([ck] * n_pages), *([cv] * n_pages))
    return out_t.transpose(0, 2, 1).reshape(n_seq, D_MODEL).astype(BF16)


def _post_kernel(alpha, a_ref, x_ref, gm_ref, shf_ref, scf_ref, gf_ref, lng_ref, lnb_ref,
                 wo_ref, w1_ref, w3_ref, w2_ref, o_ref, g_ref):
    m = _dot(a_ref[...], wo_ref[...])
    y1 = _layernorm(alpha * x_ref[...] + (1.0 + gm_ref[...]) * m, lng_ref[0:1, :], lnb_ref[0:1, :])
    h = (y1 * (1.0 + scf_ref[...]) + shf_ref[...]).astype(BF16)
    _swiglu_hidden(h, w1_ref, w3_ref, g_ref, FFN_CHUNK)
    f = _dot(g_ref[...], w2_ref[...])
    o_ref[...] = _layernorm(alpha * y1 + (1.0 + gf_ref[...]) * f, lng_ref[1:2, :], lnb_ref[1:2, :])


def _swiglu_hidden(h, w1_ref, w3_ref, g_ref, chunk):
    for c in range(w1_ref.shape[1] // chunk):
        cols = slice(c * chunk, (c + 1) * chunk)
        t1 = _dot(h, w1_ref[:, cols])
        t3 = _dot(h, w3_ref[:, cols])
        g_ref[:, cols] = (_silu(t1) * t3).astype(BF16)


def _post_attn(attn, x, gm, shf, scf, gf, ln_g, ln_b, wo, w1, w3, w2, tm, alpha):
    t = x.shape[0]
    d = D_MODEL
    row = pl.BlockSpec((tm, d), lambda i: (i, 0))
    return pl.pallas_call(
        functools.partial(_post_kernel, alpha),
        grid=(t // tm,),
        in_specs=[row, row, gm[1], shf[1], scf[1], gf[1], _resident((2, d)), _resident((2, d)),
                  _resident(wo.shape), _resident(w1.shape), _resident(w3.shape), _resident(w2.shape)],
        out_specs=row,
        out_shape=jax.ShapeDtypeStruct((t, d), F32),
        scratch_shapes=[pltpu.VMEM((tm, w1.shape[1]), BF16)],
        compiler_params=_params(1),
        name="post_attn_ffn",
    )(attn, x, gm[0], shf[0], scf[0], gf[0], ln_g, ln_b, wo, w1, w3, w2)


def _glu(y, shm_ref, scm_ref, wpw1_ref, bpw1_ref):
    d = D_MODEL
    h = (y * (1.0 + scm_ref[...]) + shm_ref[...]).astype(BF16)
    a = _dot(h, wpw1_ref[:, 0:d]) + bpw1_ref[:, 0:d]
    g = _dot(h, wpw1_ref[:, d:2 * d]) + bpw1_ref[:, d:2 * d]
    return a * jax.nn.sigmoid(g)


def _conv_tail(alpha, y, conv, gm_ref, shf_ref, scf_ref, lng_ref, lnb_ref, clg_ref, clb_ref,
               wpw2_ref, bpw2_ref, wr_ref, br_ref, y3_ref, h2_ref, lg_ref):
    z = _silu(_layernorm(conv, clg_ref[...], clb_ref[...])).astype(BF16)
    m = _dot(z, wpw2_ref[...]) + bpw2_ref[...]
    y3 = _layernorm(alpha * y + (1.0 + gm_ref[...]) * m, lng_ref[0:1, :], lnb_ref[0:1, :])
    y3_ref[...] = y3
    h2 = y3 * (1.0 + scf_ref[...]) + shf_ref[...]
    _rows_to_tiles(h2_ref, h2)
    h2_hi = h2.astype(BF16)
    h2_lo = (h2 - h2_hi.astype(F32)).astype(BF16)
    lg_ref[...] = (_dot(h2_hi, wr_ref[0]) + _dot(h2_lo, wr_ref[0]) + _dot(h2_hi, wr_ref[1])) + br_ref[...]


def _conv_prompt_kernel(alpha, y_ref, shm_ref, scm_ref, gm_ref, shf_ref, scf_ref, lng_ref, lnb_ref,
                        wpw1_ref, bpw1_ref, wdw_ref, bdw_ref, clg_ref, clb_ref, wpw2_ref, bpw2_ref,
                        wr_ref, br_ref, y3_ref, h2_ref, lg_ref, tail_ref, ext_ref, shift_ref, conv_ref):
    tm = y_ref.shape[0]
    d = D_MODEL
    rows_ext = tm + HIST_PAD

    @pl.when(pl.program_id(1) == 0)
    def _():
        ext_ref[0:HIST_PAD, :] = jnp.zeros((HIST_PAD, d), F32)

    y = y_ref[...]
    u = _glu(y, shm_ref, scm_ref, wpw1_ref, bpw1_ref)
    ext_ref[HIST_PAD:rows_ext, :] = u
    tail_ref[...] = u[tm - HIST_PAD:tm, :]
    span = rows_ext - SUBLANES
    for b in range(1, SUBLANES):
        shift_ref[b - 1, :, :] = ext_ref[b:b + span, :]
    base = HIST_PAD - (CONV_WIDTH - 1)
    rb = 4 * SUBLANES

    def body(r, carry):
        r0 = pl.multiple_of(r * rb, rb)
        acc = jnp.broadcast_to(bdw_ref[...], (rb, d))
        for k in range(CONV_WIDTH):
            off = base + k
            start = r0 + (off // SUBLANES) * SUBLANES
            if off % SUBLANES == 0:
                x = ext_ref[pl.ds(start, rb), :]
            else:
                x = shift_ref[off % SUBLANES - 1, pl.ds(start, rb), :]
            acc = acc + wdw_ref[k:k + 1, :] * x
        conv_ref[pl.ds(r0, rb), :] = acc
        return carry

    lax.fori_loop(0, tm // rb, body, 0)
    ext_ref[0:HIST_PAD, :] = u[tm - HIST_PAD:tm, :]
    _conv_tail(alpha, y, conv_ref[...], gm_ref, shf_ref, scf_ref, lng_ref, lnb_ref, clg_ref, clb_ref,
               wpw2_ref, bpw2_ref, wr_ref, br_ref, y3_ref, h2_ref, lg_ref)


def _conv_sample_kernel(alpha, y_ref, shm_ref, scm_ref, gm_ref, shf_ref, scf_ref, lng_ref, lnb_ref,
                        wpw1_ref, bpw1_ref, wdw_ref, bdw_ref, clg_ref, clb_ref, wpw2_ref, bpw2_ref,
                        wr_ref, br_ref, hist_ref, y3_ref, h2_ref, lg_ref, u_ref):
    nh = CONV_WIDTH - 1
    y = y_ref[...]
    u = _glu(y, shm_ref, scm_ref, wpw1_ref, bpw1_ref)
    u_ref[...] = u
    w = wdw_ref[...]
    conv = bdw_ref[...] + w[nh:nh + 1, :] * u
    for k in range(nh):
        conv = conv + w[k:k + 1, :] * hist_ref[k]
    _conv_tail(alpha, y, conv, gm_ref, shf_ref, scf_ref, lng_ref, lnb_ref, clg_ref, clb_ref,
               wpw2_ref, bpw2_ref, wr_ref, br_ref, y3_ref, h2_ref, lg_ref)


def _conv_weight_specs(d):
    return [_resident((2, d)), _resident((2, d)), _resident((d, 2 * d)), _resident((1, 2 * d)),
            _resident((CONV_WIDTH, d)), _resident((1, d)), _resident((1, d)), _resident((1, d)),
            _resident((d, d)), _resident((1, d)), _resident((2, d, LANES)), _resident((1, LANES))]


def _conv_prompt(y, mods5, weights, n_seq, seq_len, alpha):
    d = D_MODEL
    tm = TM_CONV
    tps = seq_len // tm
    t = n_seq * seq_len
    row = pl.BlockSpec((tm, d), lambda b, i: (b * tps + i, 0))
    trow = pl.BlockSpec((tm * ROW_TILES, LANES), lambda b, i: (b * tps + i, 0))
    lgrow = pl.BlockSpec((tm, LANES), lambda b, i: (b * tps + i, 0))
    tail = pl.BlockSpec((None, HIST_PAD, d), lambda b, i: (b, 0, 0))
    return pl.pallas_call(
        functools.partial(_conv_prompt_kernel, alpha),
        grid=(n_seq, tps),
        in_specs=[row] + [m[1] for m in mods5] + _conv_weight_specs(d),
        out_specs=[row, trow, lgrow, tail],
        out_shape=[jax.ShapeDtypeStruct((t, d), F32),
                   jax.ShapeDtypeStruct((t * ROW_TILES, LANES), F32),
                   jax.ShapeDtypeStruct((t, LANES), F32),
                   jax.ShapeDtypeStruct((n_seq, HIST_PAD, d), F32)],
        scratch_shapes=[pltpu.VMEM((tm + HIST_PAD, d), F32),
                        pltpu.VMEM((SUBLANES - 1, tm + HIST_PAD - SUBLANES, d), F32),
                        pltpu.VMEM((tm, d), F32)],
        compiler_params=_params(2),
        name="conv_prompt",
    )(y, *[m[0] for m in mods5], *weights)


def _conv_sample(y, mods5, weights, hist, alpha):
    d = D_MODEL
    n = y.shape[0]
    tm = 32
    row = pl.BlockSpec((tm, d), lambda i: (i, 0))
    return pl.pallas_call(
        functools.partial(_conv_sample_kernel, alpha),
        grid=(n // tm,),
        in_specs=[row] + [m[1] for m in mods5] + _conv_weight_specs(d)
        + [pl.BlockSpec((CONV_WIDTH - 1, tm, d), lambda i: (0, i, 0))],
        out_specs=[row, pl.BlockSpec((tm * ROW_TILES, LANES), lambda i: (i, 0)),
                   pl.BlockSpec((tm, LANES), lambda i: (i, 0)), row],
        out_shape=[jax.ShapeDtypeStruct((n, d), F32),
                   jax.ShapeDtypeStruct((n * ROW_TILES, LANES), F32),
                   jax.ShapeDtypeStruct((n, LANES), F32),
                   jax.ShapeDtypeStruct((n, d), F32)],
        compiler_params=_params(1),
        name="conv_sample",
    )(y, *[m[0] for m in mods5], *weights, hist)


def _route_tile(lg_ref, meta_ref, meta_t_ref, carry_ref):
    tm = lg_ref.shape[0]
    lane = lax.broadcasted_iota(jnp.int32, (tm, LANES), 1)
    lg = jnp.where(lane < N_EXPERTS, lg_ref[...], NEG_INF)
    m1 = jnp.max(lg, axis=1, keepdims=True)
    i1 = jnp.min(jnp.where(lg == m1, lane, LANES), axis=1, keepdims=True)
    lg2 = jnp.where(lane == i1, NEG_INF, lg)
    m2 = jnp.max(lg2, axis=1, keepdims=True)
    i2 = jnp.min(jnp.where(lg2 == m2, lane, LANES), axis=1, keepdims=True)
    e = jnp.exp(m2 - m1)
    g1 = 1.0 / (1.0 + e)
    g2 = e / (1.0 + e)
    hot1 = lane == i1
    hot2 = lane == i2
    onehot = (hot1 | hot2).astype(BF16)
    ri = lax.broadcasted_iota(jnp.int32, (tm, tm), 0)
    ci = lax.broadcasted_iota(jnp.int32, (tm, tm), 1)
    before = (ci < ri).astype(BF16)
    rank = _dot(before, onehot) + carry_ref[...]
    r1 = jnp.sum(jnp.where(hot1, rank, 0.0), axis=1, keepdims=True)
    r2 = jnp.sum(jnp.where(hot2, rank, 0.0), axis=1, keepdims=True)
    cols = (i1.astype(F32), i2.astype(F32), g1, g2, r1, r2)
    meta = jnp.zeros((tm, LANES), F32)
    for c, val in enumerate(cols):
        meta = jnp.where(lane == c, val, meta)
    meta_ref[...] = meta
    meta_t_ref[...] = jnp.transpose(meta)[0:SUBLANES, :]
    carry_ref[...] += jnp.sum(onehot.astype(F32), axis=0, keepdims=True)


def _route_kernel(n_prompt_tiles, lgp_ref, lgs_ref, mp_ref, mpt_ref, ms_ref, mst_ref, cnt_ref, carry_ref):
    i = pl.program_id(0)

    @pl.when(i == 0)
    def _():
        carry_ref[...] = jnp.zeros_like(carry_ref)

    @pl.when(i < n_prompt_tiles)
    def _():
        _route_tile(lgp_ref, mp_ref, mpt_ref, carry_ref)

    @pl.when(i == n_prompt_tiles)
    def _():
        _route_tile(lgs_ref, ms_ref, mst_ref, carry_ref)
        cnt_ref[...] = carry_ref[...]


def _route(lg_p, lg_s):
    tp, ts = lg_p.shape[0], lg_s.shape[0]
    tm = TM_ROUTE
    npt = tp // tm
    last = npt - 1
    return pl.pallas_call(
        functools.partial(_route_kernel, npt),
        grid=(npt + 1,),
        in_specs=[pl.BlockSpec((tm, LANES), lambda i: (jnp.minimum(i, last), 0)),
                  pl.BlockSpec((ts, LANES), lambda i: (0, 0))],
        out_specs=[pl.BlockSpec((tm, LANES), lambda i: (jnp.minimum(i, last), 0)),
                   pl.BlockSpec((SUBLANES, tm), lambda i: (0, jnp.minimum(i, last))),
                   pl.BlockSpec((ts, LANES), lambda i: (0, 0)),
                   pl.BlockSpec((SUBLANES, ts), lambda i: (0, 0)),
                   pl.BlockSpec((1, LANES), lambda i: (0, 0))],
        out_shape=[jax.ShapeDtypeStruct((tp, LANES), F32),
                   jax.ShapeDtypeStruct((SUBLANES, tp), F32),
                   jax.ShapeDtypeStruct((ts, LANES), F32),
                   jax.ShapeDtypeStruct((SUBLANES, ts), F32),
                   jax.ShapeDtypeStruct((1, LANES), F32)],
        scratch_shapes=[pltpu.VMEM((1, LANES), F32)],
        compiler_params=_params(1),
        name="moe_route",
    )(lg_p, lg_s)


def _row_copy(src_ref, src_row, dst_ref, dst_row, sem, n=1):
    return pltpu.make_async_copy(
        src_ref.at[pl.ds(pl.multiple_of(src_row * ROW_TILES, ROW_TILES), n * ROW_TILES), :],
        dst_ref.at[pl.ds(pl.multiple_of(dst_row * ROW_TILES, ROW_TILES), n * ROW_TILES), :], sem)


def _scatter_rows(slots_ref, h_ref, xs_ref, sem):
    tm = h_ref.shape[0] // ROW_TILES

    def issue(r, carry):
        for c in range(MOE_TOPK):
            _row_copy(h_ref, r, xs_ref, slots_ref[0, c * tm + r], sem).start()
        return carry

    lax.fori_loop(0, tm, issue, 0)
    for c in range(MOE_TOPK):
        _row_copy(h_ref, 0, xs_ref, 0, sem, n=tm).wait()


def _scatter_kernel(n_prompt_tiles, zlo_ref, zhi_ref, sp_ref, ss_ref, hp_ref, hs_ref, xs_ref,
                    zero_ref, sem, zsem):
    i = pl.program_id(0)

    @pl.when(i < n_prompt_tiles)
    def _():
        _scatter_rows(sp_ref, hp_ref, xs_ref, sem)

    @pl.when(i == n_prompt_tiles)
    def _():
        _scatter_rows(ss_ref, hs_ref, xs_ref, sem)
        zero_ref[...] = jnp.zeros_like(zero_ref)
        for e in range(N_EXPERTS):
            lo = zlo_ref[e]
            n = zhi_ref[e] - lo
            nbig = n // ZERO_ROWS

            def big(k, carry, lo=lo):
                return _row_copy(zero_ref, 0, xs_ref, lo + k * ZERO_ROWS, zsem, n=ZERO_ROWS)

            def small(r, carry, lo=lo):
                return _row_copy(zero_ref, 0, xs_ref, lo + r, zsem)

            lax.fori_loop(0, nbig, lambda k, c: (big(k, c).start(), c)[1], 0)
            lax.fori_loop(nbig * ZERO_ROWS, n, lambda r, c: (small(r, c).start(), c)[1], 0)
            lax.fori_loop(0, nbig, lambda k, c: (big(k, c).wait(), c)[1], 0)
            lax.fori_loop(nbig * ZERO_ROWS, n, lambda r, c: (small(r, c).wait(), c)[1], 0)


def _scatter(h2t_p, h2t_s, slots_p, slots_s, zero_lo, zero_hi, n_rows):
    tp = h2t_p.shape[0] // ROW_TILES
    ts = h2t_s.shape[0] // ROW_TILES
    tm = TM_ROUTE
    npt = tp // tm
    last = npt - 1
    grid_spec = pltpu.PrefetchScalarGridSpec(
        num_scalar_prefetch=2,
        grid=(npt + 1,),
        in_specs=[pl.BlockSpec((None, 1, MOE_TOPK * tm), lambda i, lo, hi: (jnp.minimum(i, last), 0, 0),
                               memory_space=pltpu.SMEM),
                  pl.BlockSpec((None, 1, MOE_TOPK * ts), lambda i, lo, hi: (0, 0, 0), memory_space=pltpu.SMEM),
                  pl.BlockSpec((tm * ROW_TILES, LANES), lambda i, lo, hi: (jnp.minimum(i, last), 0)),
                  pl.BlockSpec((ts * ROW_TILES, LANES), lambda i, lo, hi: (0, 0))],
        out_specs=pl.BlockSpec(memory_space=pl.ANY),
        scratch_shapes=[pltpu.VMEM((ZERO_ROWS * ROW_TILES, LANES), F32), pltpu.SemaphoreType.DMA(()),
                        pltpu.SemaphoreType.DMA(())],
    )
    return pl.pallas_call(
        functools.partial(_scatter_kernel, npt),
        grid_spec=grid_spec,
        out_shape=jax.ShapeDtypeStruct((n_rows * ROW_TILES, LANES), F32),
        compiler_params=pltpu.CompilerParams(dimension_semantics=("arbitrary",),
                                             vmem_limit_bytes=VMEM_LIMIT, has_side_effects=True),
        name="moe_scatter",
    )(zero_lo, zero_hi, slots_p, slots_s, h2t_p, h2t_s)


def _gmm_kernel(exp_ref, nv_ref, x_ref, w1_ref, w3_ref, w2_ref, o_ref, g_ref):
    del exp_ref
    tg = g_ref.shape[0]
    live = pl.program_id(0) < nv_ref[0]

    @pl.when(live)
    def _():
        x = _tiles_to_rows(x_ref, tg).astype(BF16)
        _swiglu_hidden(x, w1_ref, w3_ref, g_ref, TF_GMM)
        _rows_to_tiles(o_ref, _dot(g_ref[...], w2_ref[...]))

    @pl.when(jnp.logical_not(live))
    def _():
        o_ref[...] = jnp.zeros_like(o_ref)


def _gmm(xs, w1, w3, w2, tile_exp, n_valid):
    d = D_MODEL
    tg = TM_GMM
    dfe = w1.shape[2]
    n_tiles = tile_exp.shape[0]
    one = pl.Buffered(1)
    grid_spec = pltpu.PrefetchScalarGridSpec(
        num_scalar_prefetch=2,
        grid=(n_tiles,),
        in_specs=[
            pl.BlockSpec((tg * ROW_TILES, LANES), lambda m, ex, nv: (m, 0)),
            pl.BlockSpec((None, d, dfe), lambda m, ex, nv: (ex[m], 0, 0), pipeline_mode=one),
            pl.BlockSpec((None, d, dfe), lambda m, ex, nv: (ex[m], 0, 0), pipeline_mode=one),
            pl.BlockSpec((None, dfe, d), lambda m, ex, nv: (ex[m], 0, 0), pipeline_mode=one),
        ],
        out_specs=pl.BlockSpec((tg * ROW_TILES, LANES), lambda m, ex, nv: (m, 0)),
        scratch_shapes=[pltpu.VMEM((tg, dfe), BF16)],
    )
    return pl.pallas_call(
        _gmm_kernel,
        grid_spec=grid_spec,
        out_shape=jax.ShapeDtypeStruct(xs.shape, F32),
        compiler_params=_params(1),
        name="moe_experts",
    )(tile_exp, n_valid, xs, w1, w3, w2)


def _combine_kernel(alpha, slots_ref, ys_ref, y_ref, meta_ref, gf_ref, lng_ref, lnb_ref, o_ref,
                    buf_ref, sem):
    tm = y_ref.shape[0]

    def issue(r, carry):
        for c in range(MOE_TOPK):
            _row_copy(ys_ref, slots_ref[0, c * tm + r], buf_ref.at[c], r, sem).start()
        return carry

    lax.fori_loop(0, tm, issue, 0)
    for c in range(MOE_TOPK):
        _row_copy(ys_ref, 0, buf_ref.at[c], 0, sem, n=tm).wait()
    meta = meta_ref[...]
    f = jnp.zeros((tm, D_MODEL), F32)
    for c in range(MOE_TOPK):
        f = f + _tiles_to_rows(buf_ref.at[c], tm) * meta[:, META_GATE + c:META_GATE + c + 1]
    o_ref[...] = _layernorm(alpha * y_ref[...] + (1.0 + gf_ref[...]) * f,
                            lng_ref[1:2, :], lnb_ref[1:2, :])


def _combine(ys, y3, slots3, meta, gf, ln_g, ln_b, tm, alpha):
    t, d = y3.shape
    return pl.pallas_call(
        functools.partial(_combine_kernel, alpha),
        grid=(t // tm,),
        in_specs=[pl.BlockSpec((None, 1, MOE_TOPK * tm), lambda i: (i, 0, 0), memory_space=pltpu.SMEM),
                  pl.BlockSpec(memory_space=pl.ANY),
                  pl.BlockSpec((tm, d), lambda i: (i, 0)),
                  pl.BlockSpec((tm, LANES), lambda i: (i, 0)),
                  gf[1], _resident((2, d)), _resident((2, d))],
        out_specs=pl.BlockSpec((tm, d), lambda i: (i, 0)),
        out_shape=jax.ShapeDtypeStruct((t, d), F32),
        scratch_shapes=[pltpu.VMEM((MOE_TOPK, tm * ROW_TILES, LANES), F32), pltpu.SemaphoreType.DMA(())],
        compiler_params=_params(1),
        name="moe_combine",
    )(slots3, ys, y3, meta, gf[0], ln_g, ln_b)


def _tile_slots(slots, tm):
    k, t = slots.shape
    return slots.reshape(k, t // tm, tm).transpose(1, 0, 2).reshape(t // tm, 1, k * tm)


def kernel(x_prompt, x_sample, cache_k, cache_v, state_conv, page_table, c_prompt, c_sample,
           ada_w, ada_b, ln_g, ln_b, attn_w_qkv, attn_w_o,
           conv_w_pw1, conv_b_pw1, conv_w_dw, conv_b_dw, conv_ln_g, conv_ln_b, conv_w_pw2, conv_b_pw2,
           ffn_w1, ffn_w3, ffn_w2, moe_w_router, moe_b_router, moe_w1, moe_w3, moe_w2):
    b, s, d = x_prompt.shape
    db, ds, _ = x_sample.shape
    depth = ada_w.shape[0]
    assert d == D_MODEL and ds == 1 and depth == 2
    assert page_table.shape[1] * PAGE_SIZE % MOBA_BLOCK == 0
    alpha = (2 * depth) ** 0.25
    tp = b * s
    t_all = tp + db
    assert tp % TM_ROUTE == 0 and tp % TM_COMBINE == 0

    c_all = jnp.concatenate([c_sample, c_prompt], axis=0)
    mods = _adaln(c_all, ada_w, ada_b)
    mods = _Mods(mods.reshape(depth * 6, db + b, d), db)

    def pmod(layer, k, tm):
        tps = s // tm
        return mods.prompt(layer * 6 + k, lambda i: i // tps)

    def smod(layer, k, rows=None, block_of_step=lambda i: 0):
        return mods.sample(layer * 6 + k, db if rows is None else rows, block_of_step)

    xp = x_prompt.reshape(tp, d)
    xs_ = x_sample.reshape(db, d)

    wqkv = attn_w_qkv[0].astype(BF16)
    qp, kp, vp = _qkv(xp, pmod(0, 0, TM_QKV), pmod(0, 1, TM_QKV), wqkv, TM_QKV)
    qs, ks, vs = _qkv(xs_, smod(0, 0), smod(0, 1), wqkv, db)
    attn_p = _moba_prompt(qp.reshape(b, s, d), kp.reshape(b, s, d), vp.reshape(b, s, d)).reshape(tp, d)
    attn_s = _moba_sample(qs, ks, vs, cache_k, cache_v, 0, page_table)

    assert ffn_w1.shape[2] % FFN_CHUNK == 0 and moe_w1.shape[3] % TF_GMM == 0
    wo = attn_w_o[0].astype(BF16)
    ffn_w = (ffn_w1[0].astype(BF16), ffn_w3[0].astype(BF16), ffn_w2[0].astype(BF16))
    y_p = _post_attn(attn_p, xp, pmod(0, 2, TM_POST), pmod(0, 3, TM_POST), pmod(0, 4, TM_POST),
                     pmod(0, 5, TM_POST), ln_g[0], ln_b[0], wo, *ffn_w, TM_POST, alpha)
    y_s = _post_attn(attn_s, xs_, smod(0, 2), smod(0, 3), smod(0, 4), smod(0, 5),
                     ln_g[0], ln_b[0], wo, *ffn_w, db, alpha)

    wr = jnp.pad(moe_w_router[0], ((0, 0), (0, LANES - N_EXPERTS)))
    wr_hi = wr.astype(BF16)
    wr = jnp.stack([wr_hi, (wr - wr_hi.astype(F32)).astype(BF16)])
    br = jnp.pad(moe_b_router[0], (0, LANES - N_EXPERTS)).reshape(1, LANES)
    conv_weights = (ln_g[1], ln_b[1], conv_w_pw1[0].astype(BF16), conv_b_pw1[0].reshape(1, 2 * d),
                    conv_w_dw[0], conv_b_dw[0].reshape(1, d), conv_ln_g[0].reshape(1, d),
                    conv_ln_b[0].reshape(1, d), conv_w_pw2[0].astype(BF16), conv_b_pw2[0].reshape(1, d),
                    wr, br)
    pm = [mods.prompt(6 + k, lambda bi, i: bi) for k in range(5)]
    y3_p, h2t_p, lg_p, tail = _conv_prompt(y_p, pm, conv_weights, b, s, alpha)
    sm = [smod(1, k, 32, lambda i: i) for k in range(5)]
    hist_t = jnp.transpose(state_conv[0], (1, 0, 2))
    y3_s, h2t_s, lg_s, u_s = _conv_sample(y_s, sm, conv_weights, hist_t, alpha)

    meta_p, meta_pt, meta_s, meta_st, cnt = _route(lg_p, lg_s)
    counts = cnt[0, 0:N_EXPERTS].astype(jnp.int32)
    padded = (counts + TM_GMM - 1) // TM_GMM * TM_GMM
    pad_end = jnp.cumsum(padded)
    pad_start = pad_end - padded
    n_tiles = -(-t_all * MOE_TOPK // TM_GMM) + N_EXPERTS
    n_rows = n_tiles * TM_GMM

    def slots_of(meta_t):
        e = meta_t[META_EXPERT:META_EXPERT + MOE_TOPK].astype(jnp.int32)
        slot = meta_t[META_RANK:META_RANK + MOE_TOPK].astype(jnp.int32)
        for x in range(N_EXPERTS):
            slot = slot + jnp.where(e == x, pad_start[x], 0)
        return slot

    slots_p, slots_s = slots_of(meta_pt), slots_of(meta_st)
    zero_lo = pad_start + counts
    zero_hi = jnp.concatenate([pad_start[1:], jnp.full((1,), n_rows, jnp.int32)])
    xs_buf = _scatter(h2t_p, h2t_s, _tile_slots(slots_p, TM_ROUTE), _tile_slots(slots_s, db),
                      zero_lo.astype(jnp.int32), zero_hi.astype(jnp.int32), n_rows)

    tile_ids = jnp.arange(n_tiles, dtype=jnp.int32)
    tile_exp = jnp.minimum(jnp.sum((pad_end // TM_GMM)[None, :] <= tile_ids[:, None], axis=1),
                           N_EXPERTS - 1).astype(jnp.int32)
    n_valid = (pad_end[-1] // TM_GMM).reshape(1).astype(jnp.int32)
    ys_buf = _gmm(xs_buf, moe_w1[0].astype(BF16), moe_w3[0].astype(BF16), moe_w2[0].astype(BF16),
                  tile_exp, n_valid)

    out_p = _combine(ys_buf, y3_p, _tile_slots(slots_p, TM_COMBINE), meta_p,
                     pmod(1, 5, TM_COMBINE), ln_g[1], ln_b[1], TM_COMBINE, alpha)
    out_s = _combine(ys_buf, y3_s, _tile_slots(slots_s, db), meta_s,
                     smod(1, 5), ln_g[1], ln_b[1], db, alpha)

    nh = CONV_WIDTH - 1
    new_conv_p = tail[:, HIST_PAD - nh:, :][None]
    new_conv_s = jnp.transpose(jnp.concatenate([hist_t[1:], u_s[None]], axis=0), (1, 0, 2))[None]
    hshape = (N_HEADS, HEAD_DIM)
    return (out_p.reshape(b, s, d), out_s.reshape(db, 1, d),
            kp.reshape(1, b, s, *hshape), vp.reshape(1, b, s, *hshape),
            ks.reshape(1, db, 1, *hshape), vs.reshape(1, db, 1, *hshape),
            new_conv_p, new_conv_s)
```

```python
import functools
import math

import jax
import jax.numpy as jnp
from jax import lax
from jax.experimental import pallas as pl
from jax.experimental.pallas import tpu as pltpu

F32 = jnp.float32
BF16 = jnp.bfloat16
HIGHEST = lax.Precision.HIGHEST
NEG_INF = float("-inf")
LOG2E = math.log2(math.e)

SUBLANES = 8
LANES = 128

D_MODEL = 1024
N_HEADS = 16
HEAD_DIM = D_MODEL // N_HEADS
HEADS_PER_STEP = 8
MOBA_BLOCK = 256
MOBA_TOPK = 3
PAGE_SIZE = 128
CONV_WIDTH = 31
HIST_PAD = 32
N_EXPERTS = 8
MOE_TOPK = 2
LN_EPS = 1e-5
ROW_TILES = D_MODEL // LANES

TM_QKV = 512
TM_POST = 512
TM_CONV = 512
TM_ROUTE = 512
TM_GMM = 512
TM_COMBINE = 256
TF_GMM = 512
FFN_CHUNK = 256
ZERO_ROWS = 64
VMEM_LIMIT = 56 * 1024 * 1024

META_EXPERT, META_GATE, META_RANK = 0, 2, 4

NT_DIMS = (((1,), (1,)), ((), ()))


def _dot(a, b, precision=None):
    return jnp.dot(a, b, preferred_element_type=F32, precision=precision)


def _dot_nt(a, b, precision=None):
    return lax.dot_general(a, b, NT_DIMS, preferred_element_type=F32, precision=precision)


def _silu(x):
    return x * jax.nn.sigmoid(x)


def _layernorm(z, g, b):
    mu = jnp.mean(z, axis=-1, keepdims=True)
    zc = z - mu
    var = jnp.mean(zc * zc, axis=-1, keepdims=True)
    return zc * lax.rsqrt(var + LN_EPS) * g + b


def _resident(shape):
    nd = len(shape)
    return pl.BlockSpec(shape, lambda *_: (0,) * nd, pipeline_mode=pl.Buffered(1))


def _params(n_grid_dims):
    return pltpu.CompilerParams(
        dimension_semantics=("arbitrary",) * n_grid_dims, vmem_limit_bytes=VMEM_LIMIT)


def _rows_to_tiles(ref, x):
    t = x.shape[0]
    for s in range(ROW_TILES):
        ref[pl.ds(s, t, stride=ROW_TILES), :] = x[:, s * LANES:(s + 1) * LANES]


def _tiles_to_rows(ref, t):
    return jnp.concatenate([ref[pl.ds(s, t, stride=ROW_TILES), :] for s in range(ROW_TILES)], axis=1)


def _ada_kernel(c_ref, w_ref, b_ref, o_ref):
    c = c_ref[...]
    o_ref[...] = _dot(_silu(c).astype(BF16), w_ref[...].astype(BF16)) + b_ref[...]


def _adaln(c_all, ada_w, ada_b):
    depth = ada_w.shape[0]
    n = c_all.shape[0]
    d = D_MODEL
    return pl.pallas_call(
        _ada_kernel,
        grid=(depth, 6),
        in_specs=[
            pl.BlockSpec((n, d), lambda l, k: (0, 0)),
            pl.BlockSpec((None, d, d), lambda l, k: (l, 0, k)),
            pl.BlockSpec((None, None, 1, d), lambda l, k: (l, k, 0, 0)),
        ],
        out_specs=pl.BlockSpec((None, None, n, d), lambda l, k: (l, k, 0, 0)),
        out_shape=jax.ShapeDtypeStruct((depth, 6, n, d), F32),
        compiler_params=_params(2),
        name="adaln",
    )(c_all, ada_w, ada_b.reshape(depth, 6, 1, d))


class _Mods:
    def __init__(self, mods, n_sample):
        self.n_sample = n_sample
        self.m3 = mods
        self.m4 = mods.reshape(mods.shape[0], mods.shape[1], 1, D_MODEL)

    def prompt(self, lk, seq_of_step):
        ns = self.n_sample
        return self.m4, pl.BlockSpec(
            (None, None, 1, D_MODEL), lambda *g: (lk, ns + seq_of_step(*g), 0, 0))

    def sample(self, lk, rows, block_of_step):
        return self.m3, pl.BlockSpec((None, rows, D_MODEL), lambda *g: (lk, block_of_step(*g), 0))


def _qkv_kernel(x_ref, sh_ref, sc_ref, w_ref, q_ref, k_ref, v_ref):
    d = D_MODEL
    h = (x_ref[...] * (1.0 + sc_ref[...]) + sh_ref[...]).astype(BF16)
    q_ref[...] = _dot(h, w_ref[:, 0:d])
    k_ref[...] = _dot(h, w_ref[:, d:2 * d])
    v_ref[...] = _dot(h, w_ref[:, 2 * d:3 * d])


def _qkv(x, sh, sc, w_bf16, tm):
    t = x.shape[0]
    d = D_MODEL
    row = pl.BlockSpec((tm, d), lambda i: (i, 0))
    out = jax.ShapeDtypeStruct((t, d), F32)
    return pl.pallas_call(
        _qkv_kernel,
        grid=(t // tm,),
        in_specs=[row, sh[1], sc[1], _resident((d, 3 * d))],
        out_specs=[row, row, row],
        out_shape=[out, out, out],
        compiler_params=_params(1),
        name="qkv",
    )(x, sh[0], sc[0], w_bf16)


def _select_blocks(gate_t, n_valid):
    nb = gate_t.shape[0]
    blk = lax.broadcasted_iota(jnp.int32, gate_t.shape, 0)
    valid = blk < n_valid
    rows = []
    for n in range(nb):
        gn = gate_t[n:n + 1, :]
        beats = ((gate_t > gn) | ((gate_t == gn) & (blk < n))) & valid
        rows.append(jnp.sum(beats.astype(F32), axis=0, keepdims=True))
    cnt = jnp.concatenate(rows, axis=0)
    return (valid & (cnt < MOBA_TOPK)).astype(F32)


def _attn_kernel(q_ref, k_ref, v_ref, bias_ref, o_ref, kt_ref, vt_ref, kb_ref, ve_ref, km_ref, mb_ref, acc_ref):
    blk = MOBA_BLOCK
    s_len = k_ref.shape[0]
    nb = s_len // blk
    nh = HEADS_PER_STEP
    cb = pl.program_id(1)
    own = pl.program_id(2)

    @pl.when(own == 0)
    def _():
        kb_ref[...] = k_ref[...].astype(BF16)
        lane_s = lax.broadcasted_iota(jnp.int32, (s_len, LANES), 1)
        lane_k = lax.broadcasted_iota(jnp.int32, (nb, LANES), 1) // HEAD_DIM
        for pp in range(nh // 2):
            cols = slice(pp * LANES, (pp + 1) * LANES)
            kt_ref[cols, :] = jnp.transpose(k_ref[:, cols])
            vt_ref[cols, :] = jnp.transpose(v_ref[:, cols])
            rows = [jnp.mean(k_ref[n * blk:(n + 1) * blk, cols], axis=0, keepdims=True) for n in range(nb)]
            km = jnp.concatenate(rows, axis=0)
            km_ref[pp] = jnp.concatenate([jnp.where(lane_k == 0, km, 0.0), jnp.where(lane_k == 1, km, 0.0)], axis=0)
            v = v_ref[:, cols]
            ve_ref[2 * pp] = jnp.where(lane_s < HEAD_DIM, v, jnp.where(lane_s == HEAD_DIM, 1.0, 0.0)).astype(BF16)
            ve_ref[2 * pp + 1] = jnp.where(lane_s >= HEAD_DIM, v, jnp.where(lane_s == 0, 1.0, 0.0)).astype(BF16)

    qi = lax.broadcasted_iota(jnp.int32, (blk, blk), 0)
    ki = lax.broadcasted_iota(jnp.int32, (blk, blk), 1)
    causal = qi >= ki
    eye = (qi == ki).astype(BF16)
    lane = lax.broadcasted_iota(jnp.int32, (blk, LANES), 1)
    lane_head = lane // HEAD_DIM
    blk_off = (own - lane % nb).astype(F32) * float(blk)
    own_start = pl.multiple_of(own * blk, blk)

    qh = []
    for pp in range(nh // 2):
        q = q_ref[:, pp * LANES:(pp + 1) * LANES]
        qs_all = (q * (HEAD_DIM ** -0.5 * LOG2E)).astype(BF16)
        gates = _dot_nt(km_ref[pp], q, precision=HIGHEST)
        sel_t = jnp.concatenate([_select_blocks(gates[hh * nb:(hh + 1) * nb, :], own) for hh in range(2)]
                                + [jnp.zeros((LANES - 2 * nb, blk), F32)], axis=0).astype(BF16)
        sel = _dot_nt(eye, sel_t)
        head_no = (cb * nh + 2 * pp + 1 + (lane >= nb).astype(jnp.int32)).astype(F32)
        slope2 = jnp.exp2(head_no * (-8.0 / N_HEADS)) * LOG2E
        mb_ref[pp] = jnp.where(sel > 0.5, -slope2 * blk_off, NEG_INF)
        for hh in range(2):
            qh.append(jnp.where(lane_head == hh, qs_all, jnp.zeros_like(qs_all)))

    ms = []
    for h in range(nh):
        pp = h // 2
        kd = kb_ref[pl.ds(own_start, blk), pp * LANES:(pp + 1) * LANES]
        s = jnp.where(causal, _dot_nt(qh[h], kd) + bias_ref[h], NEG_INF)
        m = jnp.max(s, axis=1, keepdims=True)
        p = jnp.exp2(s - m)
        acc_ref[h] = _dot(p.astype(BF16), ve_ref[h, pl.ds(own_start, blk), :])
        ms.append(m)

    def body(j, carry):
        start = pl.multiple_of(j * blk, blk)
        out = []
        for h in range(nh):
            pp, hh = h // 2, h % 2
            kj = kb_ref[pl.ds(start, blk), pp * LANES:(pp + 1) * LANES]
            m = carry[h]
            s = _dot_nt(qh[h], kj) + bias_ref[h]
            mb = jnp.sum(jnp.where(lane == j + hh * nb, mb_ref[pp], 0.0), axis=1, keepdims=True)
            m_new = jnp.maximum(m, jnp.max(s, axis=1, keepdims=True) + mb)
            p = jnp.exp2(s + (mb - m_new))
            a = jnp.exp2(m - m_new)
            acc_ref[h] = a * acc_ref[h] + _dot(p.astype(BF16), ve_ref[h, pl.ds(start, blk), :])
            out.append(m_new)
        return tuple(out)

    lax.fori_loop(0, own, body, tuple(ms))
    for pp in range(nh // 2):
        acc0 = acc_ref[2 * pp]
        acc1 = acc_ref[2 * pp + 1]
        out = jnp.where(lane < HEAD_DIM, acc0 / acc0[:, HEAD_DIM:HEAD_DIM + 1], acc1 / acc1[:, 0:1])
        o_ref[:, pp * LANES:(pp + 1) * LANES] = out.astype(o_ref.dtype)


def _alibi_bias():
    slopes = 2.0 ** (-8.0 * jnp.arange(1, N_HEADS + 1, dtype=F32) / N_HEADS)
    qi = jnp.arange(MOBA_BLOCK, dtype=F32)[:, None]
    ki = jnp.arange(MOBA_BLOCK, dtype=F32)[None, :]
    return -(slopes * LOG2E)[:, None, None] * (qi - ki)[None]


def _moba_prompt(q, k, v):
    b, s, d = q.shape
    nb = s // MOBA_BLOCK
    nh = HEADS_PER_STEP
    w = nh * HEAD_DIM
    qspec = pl.BlockSpec((None, MOBA_BLOCK, w), lambda bi, cb, qb: (bi, qb, cb))
    kvspec = pl.BlockSpec((None, s, w), lambda bi, cb, qb: (bi, 0, cb))
    bspec = pl.BlockSpec((nh, MOBA_BLOCK, MOBA_BLOCK), lambda bi, cb, qb: (cb, 0, 0))
    tspec = pl.BlockSpec((None, w, s), lambda bi, cb, qb: (bi, cb, 0))
    return pl.pallas_call(
        _attn_kernel,
        grid=(b, d // w, nb),
        in_specs=[qspec, kvspec, kvspec, bspec],
        out_specs=[qspec, tspec, tspec],
        out_shape=[jax.ShapeDtypeStruct((b, s, d), BF16), jax.ShapeDtypeStruct((b, d, s), F32),
                   jax.ShapeDtypeStruct((b, d, s), F32)],
        scratch_shapes=[pltpu.VMEM((s, w), BF16), pltpu.VMEM((nh, s, LANES), BF16),
                        pltpu.VMEM((nh // 2, 2 * nb, LANES), F32),
                        pltpu.VMEM((nh // 2, MOBA_BLOCK, LANES), F32),
                        pltpu.VMEM((nh, MOBA_BLOCK, LANES), F32)],
        compiler_params=_params(3),
        name="moba_prompt",
    )(q, k, v, _alibi_bias())


def _attn_sample_kernel(n_pages, pt_ref, q_ref, qt_ref, kn_ref, vnt_ref, *refs):
    del pt_ref
    kp = refs[:n_pages]
    vp = refs[n_pages:2 * n_pages]
    o_ref = refs[2 * n_pages]
    s_ref, p_ref = refs[2 * n_pages + 1:]
    past = n_pages * PAGE_SIZE
    nb = past // MOBA_BLOCK
    scale = HEAD_DIM ** -0.5

    q = q_ref[...]
    qt = qt_ref[...]
    for h in range(N_HEADS):
        qb = jnp.broadcast_to(qt[:, h:h + 1], (HEAD_DIM, PAGE_SIZE))
        for p in range(n_pages):
            s_ref[h:h + 1, p * PAGE_SIZE:(p + 1) * PAGE_SIZE] = jnp.sum(kp[p][h] * qb, axis=0, keepdims=True)
    raw = s_ref[...]
    gates = [jnp.sum(raw[:, n * MOBA_BLOCK:(n + 1) * MOBA_BLOCK], axis=1, keepdims=True) * (1.0 / MOBA_BLOCK)
             for n in range(nb)]
    lane_blk = lax.broadcasted_iota(jnp.int32, (N_HEADS, past), 1) // MOBA_BLOCK
    keep = jnp.zeros((N_HEADS, past), jnp.bool_)
    for n in range(nb):
        cnt = jnp.zeros((N_HEADS, 1), F32)
        for j in range(nb):
            if j != n:
                beats = (gates[j] > gates[n]) | ((gates[j] == gates[n]) & (j < n))
                cnt = cnt + beats.astype(F32)
        keep = keep | ((lane_blk == n) & (cnt < MOBA_TOPK))
    head = lax.broadcasted_iota(jnp.int32, (N_HEADS, 1), 0)
    slope = jnp.exp2((head + 1).astype(F32) * (-8.0 / N_HEADS))
    kpos = lax.broadcasted_iota(jnp.int32, (1, past), 1).astype(F32)
    s = jnp.where(keep, raw * scale - slope * (float(past) - kpos), NEG_INF)
    s_own = jnp.sum(q * kn_ref[...], axis=1, keepdims=True) * scale
    m = jnp.maximum(jnp.max(s, axis=1, keepdims=True), s_own)
    e = jnp.exp(s - m)
    w_own = jnp.exp(s_own - m)
    inv = 1.0 / (jnp.sum(e, axis=1, keepdims=True) + w_own)
    p_ref[...] = e * inv
    w_own = w_own * inv
    lane_h = lax.broadcasted_iota(jnp.int32, (HEAD_DIM, N_HEADS), 1)
    out_t = jnp.zeros((HEAD_DIM, N_HEADS), F32)
    for h in range(N_HEADS):
        acc = jnp.zeros((HEAD_DIM, PAGE_SIZE), F32)
        for p in range(n_pages):
            acc = acc + vp[p][h] * p_ref[h:h + 1, p * PAGE_SIZE:(p + 1) * PAGE_SIZE]
        col = jnp.sum(acc, axis=1, keepdims=True) + w_own[h:h + 1, :] * vnt_ref[:, h:h + 1]
        out_t = jnp.where(lane_h == h, col, out_t)
    o_ref[...] = out_t


def _moba_sample(q, k_new, v_new, cache_k, cache_v, layer, page_table):
    n_seq, n_pages = page_table.shape
    hd = (N_HEADS, HEAD_DIM)
    dh = (HEAD_DIM, N_HEADS)
    ck = jnp.transpose(cache_k, (0, 1, 3, 4, 2))
    cv = jnp.transpose(cache_v, (0, 1, 3, 4, 2))
    pt = page_table.reshape(-1).astype(jnp.int32)
    q3 = q.reshape((n_seq,) + hd)
    hd_spec = pl.BlockSpec((None,) + hd, lambda i, pt: (i, 0, 0))
    dh_spec = pl.BlockSpec((None,) + dh, lambda i, pt: (i, 0, 0))

    def page_spec(p):
        return pl.BlockSpec((None, None) + hd + (PAGE_SIZE,),
                            lambda i, pt: (layer, pt[i * n_pages + p], 0, 0, 0))

    grid_spec = pltpu.PrefetchScalarGridSpec(
        num_scalar_prefetch=1,
        grid=(n_seq,),
        in_specs=[hd_spec, dh_spec, hd_spec, dh_spec] + [page_spec(p) for p in range(n_pages)] * 2,
        out_specs=dh_spec,
        scratch_shapes=[pltpu.VMEM((N_HEADS, n_pages * PAGE_SIZE), F32)] * 2,
    )
    out_t = pl.pallas_call(
        functools.partial(_attn_sample_kernel, n_pages),
        grid_spec=grid_spec,
        out_shape=jax.ShapeDtypeStruct((n_seq,) + dh, F32),
        compiler_params=_params(1),
        name="moba_sample",
    )(pt, q3, q3.transpose(0, 2, 1), k_new.reshape((n_seq,) + hd),
      v_new.reshape((n_seq,) + hd).transpose(0, 2, 1), *([ck] * n_pages), *([cv] * n_pages))
    return out_t.transpose(0, 2, 1).reshape(n_seq, D_MODEL).astype(BF16)


def _post_kernel(alpha, a_ref, x_ref, gm_ref, shf_ref, scf_ref, gf_ref, lng_ref, lnb_ref,
                 wo_ref, w1_ref, w3_ref, w2_ref, o_ref, g_ref):
    m = _dot(a_ref[...], wo_ref[...])
    y1 = _layernorm(alpha * x_ref[...] + (1.0 + gm_ref[...]) * m, lng_ref[0:1, :], lnb_ref[0:1, :])
    h = (y1 * (1.0 + scf_ref[...]) + shf_ref[...]).astype(BF16)
    _swiglu_hidden(h, w1_ref, w3_ref, g_ref, FFN_CHUNK)
    f = _dot(g_ref[...], w2_ref[...])
    o_ref[...] = _layernorm(alpha * y1 + (1.0 + gf_ref[...]) * f, lng_ref[1:2, :], lnb_ref[1:2, :])


def _swiglu_hidden(h, w1_ref, w3_ref, g_ref, chunk):
    for c in range(w1_ref.shape[1] // chunk):
        cols = slice(c * chunk, (c + 1) * chunk)
        t1 = _dot(h, w1_ref[:, cols])
        t3 = _dot(h, w3_ref[:, cols])
        g_ref[:, cols] = (_silu(t1) * t3).astype(BF16)


def _post_attn(attn, x, gm, shf, scf, gf, ln_g, ln_b, wo, w1, w3, w2, tm, alpha):
    t = x.shape[0]
    d = D_MODEL
    row = pl.BlockSpec((tm, d), lambda i: (i, 0))
    return pl.pallas_call(
        functools.partial(_post_kernel, alpha),
        grid=(t // tm,),
        in_specs=[row, row, gm[1], shf[1], scf[1], gf[1], _resident((2, d)), _resident((2, d)),
                  _resident(wo.shape), _resident(w1.shape), _resident(w3.shape), _resident(w2.shape)],
        out_specs=row,
        out_shape=jax.ShapeDtypeStruct((t, d), F32),
        scratch_shapes=[pltpu.VMEM((tm, w1.shape[1]), BF16)],
        compiler_params=_params(1),
        name="post_attn_ffn",
    )(attn, x, gm[0], shf[0], scf[0], gf[0], ln_g, ln_b, wo, w1, w3, w2)


def _glu(y, shm_ref, scm_ref, wpw1_ref, bpw1_ref):
    d = D_MODEL
    h = (y * (1.0 + scm_ref[...]) + shm_ref[...]).astype(BF16)
    a = _dot(h, wpw1_ref[:, 0:d]) + bpw1_ref[:, 0:d]
    g = _dot(h, wpw1_ref[:, d:2 * d]) + bpw1_ref[:, d:2 * d]
    return a * jax.nn.sigmoid(g)


def _conv_tail(alpha, y, conv, gm_ref, shf_ref, scf_ref, lng_ref, lnb_ref, clg_ref, clb_ref,
               wpw2_ref, bpw2_ref, wr_ref, br_ref, y3_ref, h2_ref, lg_ref):
    z = _silu(_layernorm(conv, clg_ref[...], clb_ref[...])).astype(BF16)
    m = _dot(z, wpw2_ref[...]) + bpw2_ref[...]
    y3 = _layernorm(alpha * y + (1.0 + gm_ref[...]) * m, lng_ref[0:1, :], lnb_ref[0:1, :])
    y3_ref[...] = y3
    h2 = y3 * (1.0 + scf_ref[...]) + shf_ref[...]
    _rows_to_tiles(h2_ref, h2)
    h2_hi = h2.astype(BF16)
    h2_lo = (h2 - h2_hi.astype(F32)).astype(BF16)
    lg_ref[...] = (_dot(h2_hi, wr_ref[0]) + _dot(h2_lo, wr_ref[0]) + _dot(h2_hi, wr_ref[1])) + br_ref[...]


def _conv_prompt_kernel(alpha, y_ref, shm_ref, scm_ref, gm_ref, shf_ref, scf_ref, lng_ref, lnb_ref,
                        wpw1_ref, bpw1_ref, wdw_ref, bdw_ref, clg_ref, clb_ref, wpw2_ref, bpw2_ref,
                        wr_ref, br_ref, y3_ref, h2_ref, lg_ref, tail_ref, ext_ref, shift_ref, conv_ref):
    tm = y_ref.shape[0]
    d = D_MODEL
    rows_ext = tm + HIST_PAD

    @pl.when(pl.program_id(1) == 0)
    def _():
        ext_ref[0:HIST_PAD, :] = jnp.zeros((HIST_PAD, d), F32)

    y = y_ref[...]
    u = _glu(y, shm_ref, scm_ref, wpw1_ref, bpw1_ref)
    ext_ref[HIST_PAD:rows_ext, :] = u
    tail_ref[...] = u[tm - HIST_PAD:tm, :]
    span = rows_ext - SUBLANES
    for b in range(1, SUBLANES):
        shift_ref[b - 1, :, :] = ext_ref[b:b + span, :]
    base = HIST_PAD - (CONV_WIDTH - 1)
    rb = 4 * SUBLANES

    def body(r, carry):
        r0 = pl.multiple_of(r * rb, rb)
        acc = jnp.broadcast_to(bdw_ref[...], (rb, d))
        for k in range(CONV_WIDTH):
            off = base + k
            start = r0 + (off // SUBLANES) * SUBLANES
            if off % SUBLANES == 0:
                x = ext_ref[pl.ds(start, rb), :]
            else:
                x = shift_ref[off % SUBLANES - 1, pl.ds(start, rb), :]
            acc = acc + wdw_ref[k:k + 1, :] * x
        conv_ref[pl.ds(r0, rb), :] = acc
        return carry

    lax.fori_loop(0, tm // rb, body, 0)
    ext_ref[0:HIST_PAD, :] = u[tm - HIST_PAD:tm, :]
    _conv_tail(alpha, y, conv_ref[...], gm_ref, shf_ref, scf_ref, lng_ref, lnb_ref, clg_ref, clb_ref,
               wpw2_ref, bpw2_ref, wr_ref, br_ref, y3_ref, h2_ref, lg_ref)


def _conv_sample_kernel(alpha, y_ref, shm_ref, scm_ref, gm_ref, shf_ref, scf_ref, lng_ref, lnb_ref,
                        wpw1_ref, bpw1_ref, wdw_ref, bdw_ref, clg_ref, clb_ref, wpw2_ref, bpw2_ref,
                        wr_ref, br_ref, hist_ref, y3_ref, h2_ref, lg_ref, u_ref):
    nh = CONV_WIDTH - 1
    y = y_ref[...]
    u = _glu(y, shm_ref, scm_ref, wpw1_ref, bpw1_ref)
    u_ref[...] = u
    w = wdw_ref[...]
    conv = bdw_ref[...] + w[nh:nh + 1, :] * u
    for k in range(nh):
        conv = conv + w[k:k + 1, :] * hist_ref[k]
    _conv_tail(alpha, y, conv, gm_ref, shf_ref, scf_ref, lng_ref, lnb_ref, clg_ref, clb_ref,
               wpw2_ref, bpw2_ref, wr_ref, br_ref, y3_ref, h2_ref, lg_ref)


def _conv_weight_specs(d):
    return [_resident((2, d)), _resident((2, d)), _resident((d, 2 * d)), _resident((1, 2 * d)),
            _resident((CONV_WIDTH, d)), _resident((1, d)), _resident((1, d)), _resident((1, d)),
            _resident((d, d)), _resident((1, d)), _resident((2, d, LANES)), _resident((1, LANES))]


def _conv_prompt(y, mods5, weights, n_seq, seq_len, alpha):
    d = D_MODEL
    tm = TM_CONV
    tps = seq_len // tm
    t = n_seq * seq_len
    row = pl.BlockSpec((tm, d), lambda b, i: (b * tps + i, 0))
    trow = pl.BlockSpec((tm * ROW_TILES, LANES), lambda b, i: (b * tps + i, 0))
    lgrow = pl.BlockSpec((tm, LANES), lambda b, i: (b * tps + i, 0))
    tail = pl.BlockSpec((None, HIST_PAD, d), lambda b, i: (b, 0, 0))
    return pl.pallas_call(
        functools.partial(_conv_prompt_kernel, alpha),
        grid=(n_seq, tps),
        in_specs=[row] + [m[1] for m in mods5] + _conv_weight_specs(d),
        out_specs=[row, trow, lgrow, tail],
        out_shape=[jax.ShapeDtypeStruct((t, d), F32),
                   jax.ShapeDtypeStruct((t * ROW_TILES, LANES), F32),
                   jax.ShapeDtypeStruct((t, LANES), F32),
                   jax.ShapeDtypeStruct((n_seq, HIST_PAD, d), F32)],
        scratch_shapes=[pltpu.VMEM((tm + HIST_PAD, d), F32),
                        pltpu.VMEM((SUBLANES - 1, tm + HIST_PAD - SUBLANES, d), F32),
                        pltpu.VMEM((tm, d), F32)],
        compiler_params=_params(2),
        name="conv_prompt",
    )(y, *[m[0] for m in mods5], *weights)


def _conv_sample(y, mods5, weights, hist, alpha):
    d = D_MODEL
    n = y.shape[0]
    tm = 32
    row = pl.BlockSpec((tm, d), lambda i: (i, 0))
    return pl.pallas_call(
        functools.partial(_conv_sample_kernel, alpha),
        grid=(n // tm,),
        in_specs=[row] + [m[1] for m in mods5] + _conv_weight_specs(d)
        + [pl.BlockSpec((CONV_WIDTH - 1, tm, d), lambda i: (0, i, 0))],
        out_specs=[row, pl.BlockSpec((tm * ROW_TILES, LANES), lambda i: (i, 0)),
                   pl.BlockSpec((tm, LANES), lambda i: (i, 0)), row],
        out_shape=[jax.ShapeDtypeStruct((n, d), F32),
                   jax.ShapeDtypeStruct((n * ROW_TILES, LANES), F32),
                   jax.ShapeDtypeStruct((n, LANES), F32),
                   jax.ShapeDtypeStruct((n, d), F32)],
        compiler_params=_params(1),
        name="conv_sample",
    )(y, *[m[0] for m in mods5], *weights, hist)


def _route_tile(lg_ref, meta_ref, meta_t_ref, carry_ref):
    tm = lg_ref.shape[0]
    lane = lax.broadcasted_iota(jnp.int32, (tm, LANES), 1)
    lg = jnp.where(lane < N_EXPERTS, lg_ref[...], NEG_INF)
    m1 = jnp.max(lg, axis=1, keepdims=True)
    i1 = jnp.min(jnp.where(lg == m1, lane, LANES), axis=1, keepdims=True)
    lg2 = jnp.where(lane == i1, NEG_INF, lg)
    m2 = jnp.max(lg2, axis=1, keepdims=True)
    i2 = jnp.min(jnp.where(lg2 == m2, lane, LANES), axis=1, keepdims=True)
    e = jnp.exp(m2 - m1)
    g1 = 1.0 / (1.0 + e)
    g2 = e / (1.0 + e)
    hot1 = lane == i1
    hot2 = lane == i2
    onehot = (hot1 | hot2).astype(BF16)
    ri = lax.broadcasted_iota(jnp.int32, (tm, tm), 0)
    ci = lax.broadcasted_iota(jnp.int32, (tm, tm), 1)
    before = (ci < ri).astype(BF16)
    rank = _dot(before, onehot) + carry_ref[...]
    r1 = jnp.sum(jnp.where(hot1, rank, 0.0), axis=1, keepdims=True)
    r2 = jnp.sum(jnp.where(hot2, rank, 0.0), axis=1, keepdims=True)
    cols = (i1.astype(F32), i2.astype(F32), g1, g2, r1, r2)
    meta = jnp.zeros((tm, LANES), F32)
    for c, val in enumerate(cols):
        meta = jnp.where(lane == c, val, meta)
    meta_ref[...] = meta
    meta_t_ref[...] = jnp.transpose(meta)[0:SUBLANES, :]
    carry_ref[...] += jnp.sum(onehot.astype(F32), axis=0, keepdims=True)


def _route_kernel(n_prompt_tiles, lgp_ref, lgs_ref, mp_ref, mpt_ref, ms_ref, mst_ref, cnt_ref, carry_ref):
    i = pl.program_id(0)

    @pl.when(i == 0)
    def _():
        carry_ref[...] = jnp.zeros_like(carry_ref)

    @pl.when(i < n_prompt_tiles)
    def _():
        _route_tile(lgp_ref, mp_ref, mpt_ref, carry_ref)

    @pl.when(i == n_prompt_tiles)
    def _():
        _route_tile(lgs_ref, ms_ref, mst_ref, carry_ref)
        cnt_ref[...] = carry_ref[...]


def _route(lg_p, lg_s):
    tp, ts = lg_p.shape[0], lg_s.shape[0]
    tm = TM_ROUTE
    npt = tp // tm
    last = npt - 1
    return pl.pallas_call(
        functools.partial(_route_kernel, npt),
        grid=(npt + 1,),
        in_specs=[pl.BlockSpec((tm, LANES), lambda i: (jnp.minimum(i, last), 0)),
                  pl.BlockSpec((ts, LANES), lambda i: (0, 0))],
        out_specs=[pl.BlockSpec((tm, LANES), lambda i: (jnp.minimum(i, last), 0)),
                   pl.BlockSpec((SUBLANES, tm), lambda i: (0, jnp.minimum(i, last))),
                   pl.BlockSpec((ts, LANES), lambda i: (0, 0)),
                   pl.BlockSpec((SUBLANES, ts), lambda i: (0, 0)),
                   pl.BlockSpec((1, LANES), lambda i: (0, 0))],
        out_shape=[jax.ShapeDtypeStruct((tp, LANES), F32),
                   jax.ShapeDtypeStruct((SUBLANES, tp), F32),
                   jax.ShapeDtypeStruct((ts, LANES), F32),
                   jax.ShapeDtypeStruct((SUBLANES, ts), F32),
                   jax.ShapeDtypeStruct((1, LANES), F32)],
        scratch_shapes=[pltpu.VMEM((1, LANES), F32)],
        compiler_params=_params(1),
        name="moe_route",
    )(lg_p, lg_s)


def _row_copy(src_ref, src_row, dst_ref, dst_row, sem, n=1):
    return pltpu.make_async_copy(
        src_ref.at[pl.ds(pl.multiple_of(src_row * ROW_TILES, ROW_TILES), n * ROW_TILES), :],
        dst_ref.at[pl.ds(pl.multiple_of(dst_row * ROW_TILES, ROW_TILES), n * ROW_TILES), :], sem)


def _scatter_rows(slots_ref, h_ref, xs_ref, sem):
    tm = h_ref.shape[0] // ROW_TILES

    def issue(r, carry):
        for c in range(MOE_TOPK):
            _row_copy(h_ref, r, xs_ref, slots_ref[0, c * tm + r], sem).start()
        return carry

    lax.fori_loop(0, tm, issue, 0)
    for c in range(MOE_TOPK):
        _row_copy(h_ref, 0, xs_ref, 0, sem, n=tm).wait()


def _scatter_kernel(n_prompt_tiles, zlo_ref, zhi_ref, sp_ref, ss_ref, hp_ref, hs_ref, xs_ref,
                    zero_ref, sem, zsem):
    i = pl.program_id(0)

    @pl.when(i < n_prompt_tiles)
    def _():
        _scatter_rows(sp_ref, hp_ref, xs_ref, sem)

    @pl.when(i == n_prompt_tiles)
    def _():
        _scatter_rows(ss_ref, hs_ref, xs_ref, sem)
        zero_ref[...] = jnp.zeros_like(zero_ref)
        for e in range(N_EXPERTS):
            lo = zlo_ref[e]
            n = zhi_ref[e] - lo
            nbig = n // ZERO_ROWS

            def big(k, carry, lo=lo):
                return _row_copy(zero_ref, 0, xs_ref, lo + k * ZERO_ROWS, zsem, n=ZERO_ROWS)

            def small(r, carry, lo=lo):
                return _row_copy(zero_ref, 0, xs_ref, lo + r, zsem)

            lax.fori_loop(0, nbig, lambda k, c: (big(k, c).start(), c)[1], 0)
            lax.fori_loop(nbig * ZERO_ROWS, n, lambda r, c: (small(r, c).start(), c)[1], 0)
            lax.fori_loop(0, nbig, lambda k, c: (big(k, c).wait(), c)[1], 0)
            lax.fori_loop(nbig * ZERO_ROWS, n, lambda r, c: (small(r, c).wait(), c)[1], 0)


def _scatter(h2t_p, h2t_s, slots_p, slots_s, zero_lo, zero_hi, n_rows):
    tp = h2t_p.shape[0] // ROW_TILES
    ts = h2t_s.shape[0] // ROW_TILES
    tm = TM_ROUTE
    npt = tp // tm
    last = npt - 1
    grid_spec = pltpu.PrefetchScalarGridSpec(
        num_scalar_prefetch=2,
        grid=(npt + 1,),
        in_specs=[pl.BlockSpec((None, 1, MOE_TOPK * tm), lambda i, lo, hi: (jnp.minimum(i, last), 0, 0),
                               memory_space=pltpu.SMEM),
                  pl.BlockSpec((None, 1, MOE_TOPK * ts), lambda i, lo, hi: (0, 0, 0), memory_space=pltpu.SMEM),
                  pl.BlockSpec((tm * ROW_TILES, LANES), lambda i, lo, hi: (jnp.minimum(i, last), 0)),
                  pl.BlockSpec((ts * ROW_TILES, LANES), lambda i, lo, hi: (0, 0))],
        out_specs=pl.BlockSpec(memory_space=pl.ANY),
        scratch_shapes=[pltpu.VMEM((ZERO_ROWS * ROW_TILES, LANES), F32), pltpu.SemaphoreType.DMA(()),
                        pltpu.SemaphoreType.DMA(())],
    )
    return pl.pallas_call(
        functools.partial(_scatter_kernel, npt),
        grid_spec=grid_spec,
        out_shape=jax.ShapeDtypeStruct((n_rows * ROW_TILES, LANES), F32),
        compiler_params=pltpu.CompilerParams(dimension_semantics=("arbitrary",),
                                             vmem_limit_bytes=VMEM_LIMIT, has_side_effects=True),
        name="moe_scatter",
    )(zero_lo, zero_hi, slots_p, slots_s, h2t_p, h2t_s)


def _gmm_kernel(exp_ref, nv_ref, x_ref, w1_ref, w3_ref, w2_ref, o_ref, g_ref):
    del exp_ref
    tg = g_ref.shape[0]
    live = pl.program_id(0) < nv_ref[0]

    @pl.when(live)
    def _():
        x = _tiles_to_rows(x_ref, tg).astype(BF16)
        _swiglu_hidden(x, w1_ref, w3_ref, g_ref, TF_GMM)
        _rows_to_tiles(o_ref, _dot(g_ref[...], w2_ref[...]))

    @pl.when(jnp.logical_not(live))
    def _():
        o_ref[...] = jnp.zeros_like(o_ref)


def _gmm(xs, w1, w3, w2, tile_exp, n_valid):
    d = D_MODEL
    tg = TM_GMM
    dfe = w1.shape[2]
    n_tiles = tile_exp.shape[0]
    one = pl.Buffered(1)
    grid_spec = pltpu.PrefetchScalarGridSpec(
        num_scalar_prefetch=2,
        grid=(n_tiles,),
        in_specs=[
            pl.BlockSpec((tg * ROW_TILES, LANES), lambda m, ex, nv: (m, 0)),
            pl.BlockSpec((None, d, dfe), lambda m, ex, nv: (ex[m], 0, 0), pipeline_mode=one),
            pl.BlockSpec((None, d, dfe), lambda m, ex, nv: (ex[m], 0, 0), pipeline_mode=one),
            pl.BlockSpec((None, dfe, d), lambda m, ex, nv: (ex[m], 0, 0), pipeline_mode=one),
        ],
        out_specs=pl.BlockSpec((tg * ROW_TILES, LANES), lambda m, ex, nv: (m, 0)),
        scratch_shapes=[pltpu.VMEM((tg, dfe), BF16)],
    )
    return pl.pallas_call(
        _gmm_kernel,
        grid_spec=grid_spec,
        out_shape=jax.ShapeDtypeStruct(xs.shape, F32),
        compiler_params=_params(1),
        name="moe_experts",
    )(tile_exp, n_valid, xs, w1, w3, w2)


def _combine_kernel(alpha, slots_ref, next_slots_ref, ys_ref, y_ref, meta_ref, gf_ref, lng_ref, lnb_ref,
                    o_ref, buf_ref, sem):
    tm = y_ref.shape[0]
    i = pl.program_id(0)
    cur = i % 2

    def gather(sl_ref, slot):
        def issue(r, carry):
            for c in range(MOE_TOPK):
                _row_copy(ys_ref, sl_ref[0, c * tm + r], buf_ref.at[slot, c], r, sem.at[slot]).start()
            return carry

        lax.fori_loop(0, tm, issue, 0)

    @pl.when(i == 0)
    def _():
        gather(slots_ref, 0)

    for slot in range(2):
        @pl.when((i + 1 < pl.num_programs(0)) & (cur != slot))
        def _(slot=slot):
            gather(next_slots_ref, slot)

    for c in range(MOE_TOPK):
        _row_copy(ys_ref, 0, buf_ref.at[cur, c], 0, sem.at[cur], n=tm).wait()
    meta = meta_ref[...]
    f = jnp.zeros((tm, D_MODEL), F32)
    for c in range(MOE_TOPK):
        f = f + _tiles_to_rows(buf_ref.at[cur, c], tm) * meta[:, META_GATE + c:META_GATE + c + 1]
    o_ref[...] = _layernorm(alpha * y_ref[...] + (1.0 + gf_ref[...]) * f,
                            lng_ref[1:2, :], lnb_ref[1:2, :])


def _combine(ys, y3, slots3, meta, gf, ln_g, ln_b, tm, alpha):
    t, d = y3.shape
    last = t // tm - 1

    def slot_spec(ahead):
        return pl.BlockSpec((None, 1, MOE_TOPK * tm), lambda i: (jnp.minimum(i + ahead, last), 0, 0),
                            memory_space=pltpu.SMEM)

    return pl.pallas_call(
        functools.partial(_combine_kernel, alpha),
        grid=(t // tm,),
        in_specs=[slot_spec(0), slot_spec(1),
                  pl.BlockSpec(memory_space=pl.ANY),
                  pl.BlockSpec((tm, d), lambda i: (i, 0)),
                  pl.BlockSpec((tm, LANES), lambda i: (i, 0)),
                  gf[1], _resident((2, d)), _resident((2, d))],
        out_specs=pl.BlockSpec((tm, d), lambda i: (i, 0)),
        out_shape=jax.ShapeDtypeStruct((t, d), F32),
        scratch_shapes=[pltpu.VMEM((2, MOE_TOPK, tm * ROW_TILES, LANES), F32), pltpu.SemaphoreType.DMA((2,))],
        compiler_params=_params(1),
        name="moe_combine",
    )(slots3, slots3, ys, y3, meta, gf[0], ln_g, ln_b)


def _tile_slots(slots, tm):
    k, t = slots.shape
    return slots.reshape(k, t // tm, tm).transpose(1, 0, 2).reshape(t // tm, 1, k * tm)


def kernel(x_prompt, x_sample, cache_k, cache_v, state_conv, page_table, c_prompt, c_sample,
           ada_w, ada_b, ln_g, ln_b, attn_w_qkv, attn_w_o,
           conv_w_pw1, conv_b_pw1, conv_w_dw, conv_b_dw, conv_ln_g, conv_ln_b, conv_w_pw2, conv_b_pw2,
           ffn_w1, ffn_w3, ffn_w2, moe_w_router, moe_b_router, moe_w1, moe_w3, moe_w2):
    b, s, d = x_prompt.shape
    db, ds, _ = x_sample.shape
    depth = ada_w.shape[0]
    assert d == D_MODEL and ds == 1 and depth == 2
    assert page_table.shape[1] * PAGE_SIZE % MOBA_BLOCK == 0
    alpha = (2 * depth) ** 0.25
    tp = b * s
    t_all = tp + db
    assert tp % TM_ROUTE == 0 and tp % TM_COMBINE == 0

    c_all = jnp.concatenate([c_sample, c_prompt], axis=0)
    mods = _adaln(c_all, ada_w, ada_b)
    mods = _Mods(mods.reshape(depth * 6, db + b, d), db)

    def pmod(layer, k, tm):
        tps = s // tm
        return mods.prompt(layer * 6 + k, lambda i: i // tps)

    def smod(layer, k, rows=None, block_of_step=lambda i: 0):
        return mods.sample(layer * 6 + k, db if rows is None else rows, block_of_step)

    xp = x_prompt.reshape(tp, d)
    xs_ = x_sample.reshape(db, d)

    wqkv = attn_w_qkv[0].astype(BF16)
    qp, kp, vp = _qkv(xp, pmod(0, 0, TM_QKV), pmod(0, 1, TM_QKV), wqkv, TM_QKV)
    qs, ks, vs = _qkv(xs_, smod(0, 0), smod(0, 1), wqkv, db)
    attn_p, kt_p, vt_p = _moba_prompt(qp.reshape(b, s, d), kp.reshape(b, s, d), vp.reshape(b, s, d))
    attn_p = attn_p.reshape(tp, d)
    attn_s = _moba_sample(qs, ks, vs, cache_k, cache_v, 0, page_table)

    assert ffn_w1.shape[2] % FFN_CHUNK == 0 and moe_w1.shape[3] % TF_GMM == 0
    wo = attn_w_o[0].astype(BF16)
    ffn_w = (ffn_w1[0].astype(BF16), ffn_w3[0].astype(BF16), ffn_w2[0].astype(BF16))
    y_p = _post_attn(attn_p, xp, pmod(0, 2, TM_POST), pmod(0, 3, TM_POST), pmod(0, 4, TM_POST),
                     pmod(0, 5, TM_POST), ln_g[0], ln_b[0], wo, *ffn_w, TM_POST, alpha)
    y_s = _post_attn(attn_s, xs_, smod(0, 2), smod(0, 3), smod(0, 4), smod(0, 5),
                     ln_g[0], ln_b[0], wo, *ffn_w, db, alpha)

    wr = jnp.pad(moe_w_router[0], ((0, 0), (0, LANES - N_EXPERTS)))
    wr_hi = wr.astype(BF16)
    wr = jnp.stack([wr_hi, (wr - wr_hi.astype(F32)).astype(BF16)])
    br = jnp.pad(moe_b_router[0], (0, LANES - N_EXPERTS)).reshape(1, LANES)
    conv_weights = (ln_g[1], ln_b[1], conv_w_pw1[0].astype(BF16), conv_b_pw1[0].reshape(1, 2 * d),
                    conv_w_dw[0], conv_b_dw[0].reshape(1, d), conv_ln_g[0].reshape(1, d),
                    conv_ln_b[0].reshape(1, d), conv_w_pw2[0].astype(BF16), conv_b_pw2[0].reshape(1, d),
                    wr, br)
    pm = [mods.prompt(6 + k, lambda bi, i: bi) for k in range(5)]
    y3_p, h2t_p, lg_p, tail = _conv_prompt(y_p, pm, conv_weights, b, s, alpha)
    sm = [smod(1, k, 32, lambda i: i) for k in range(5)]
    hist_t = jnp.transpose(state_conv[0], (1, 0, 2))
    y3_s, h2t_s, lg_s, u_s = _conv_sample(y_s, sm, conv_weights, hist_t, alpha)

    meta_p, meta_pt, meta_s, meta_st, cnt = _route(lg_p, lg_s)
    counts = cnt[0, 0:N_EXPERTS].astype(jnp.int32)
    padded = (counts + TM_GMM - 1) // TM_GMM * TM_GMM
    pad_end = jnp.cumsum(padded)
    pad_start = pad_end - padded
    n_tiles = -(-t_all * MOE_TOPK // TM_GMM) + N_EXPERTS
    n_rows = n_tiles * TM_GMM

    def slots_of(meta_t):
        e = meta_t[META_EXPERT:META_EXPERT + MOE_TOPK].astype(jnp.int32)
        slot = meta_t[META_RANK:META_RANK + MOE_TOPK].astype(jnp.int32)
        for x in range(N_EXPERTS):
            slot = slot + jnp.where(e == x, pad_start[x], 0)
        return slot

    slots_p, slots_s = slots_of(meta_pt), slots_of(meta_st)
    zero_lo = pad_start + counts
    zero_hi = jnp.concatenate([pad_start[1:], jnp.full((1,), n_rows, jnp.int32)])
    xs_buf = _scatter(h2t_p, h2t_s, _tile_slots(slots_p, TM_ROUTE), _tile_slots(slots_s, db),
                      zero_lo.astype(jnp.int32), zero_hi.astype(jnp.int32), n_rows)

    tile_ids = jnp.arange(n_tiles, dtype=jnp.int32)
    tile_exp = jnp.minimum(jnp.sum((pad_end // TM_GMM)[None, :] <= tile_ids[:, None], axis=1),
                           N_EXPERTS - 1).astype(jnp.int32)
    n_valid = (pad_end[-1] // TM_GMM).reshape(1).astype(jnp.int32)
    ys_buf = _gmm(xs_buf, moe_w1[0].astype(BF16), moe_w3[0].astype(BF16), moe_w2[0].astype(BF16),
                  tile_exp, n_valid)

    out_p = _combine(ys_buf, y3_p, _tile_slots(slots_p, TM_COMBINE), meta_p,
                     pmod(1, 5, TM_COMBINE), ln_g[1], ln_b[1], TM_COMBINE, alpha)
    out_s = _combine(ys_buf, y3_s, _tile_slots(slots_s, db), meta_s,
                     smod(1, 5), ln_g[1], ln_b[1], db, alpha)

    nh = CONV_WIDTH - 1
    new_conv_p = tail[:, HIST_PAD - nh:, :][None]
    new_conv_s = jnp.transpose(jnp.concatenate([hist_t[1:], u_s[None]], axis=0), (1, 0, 2))[None]
    hshape = (N_HEADS, HEAD_DIM)

    def seq_major(t):
        return jnp.transpose(t.reshape(1, b, *hshape, s), (0, 1, 4, 2, 3))

    return (out_p.reshape(b, s, d), out_s.reshape(db, 1, d), seq_major(kt_p), seq_major(vt_p),
            ks.reshape(1, db, 1, *hshape), vs.reshape(1, db, 1, *hshape),
            new_conv_p, new_conv_s)
```

```python
import functools
import math

import jax
import jax.numpy as jnp
from jax import lax
from jax.experimental import pallas as pl
from jax.experimental.pallas import tpu as pltpu

F32 = jnp.float32
BF16 = jnp.bfloat16
HIGHEST = lax.Precision.HIGHEST
NEG_INF = float("-inf")
LOG2E = math.log2(math.e)

SUBLANES = 8
LANES = 128

D_MODEL = 1024
N_HEADS = 16
HEAD_DIM = D_MODEL // N_HEADS
HEADS_PER_STEP = 8
MOBA_BLOCK = 256
MOBA_TOPK = 3
PAGE_SIZE = 128
CONV_WIDTH = 31
HIST_PAD = 32
N_EXPERTS = 8
MOE_TOPK = 2
LN_EPS = 1e-5
ROW_TILES = D_MODEL // LANES

TM_QKV = 512
TM_POST = 512
TM_CONV = 512
TM_ROUTE = 512
TM_GMM = 512
TM_COMBINE = 256
TF_GMM = 512
FFN_CHUNK = 256
ZERO_ROWS = 64
VMEM_LIMIT = 56 * 1024 * 1024

META_EXPERT, META_GATE, META_RANK = 0, 2, 4

NT_DIMS = (((1,), (1,)), ((), ()))


def _dot(a, b, precision=None):
    return jnp.dot(a, b, preferred_element_type=F32, precision=precision)


def _dot_nt(a, b, precision=None):
    return lax.dot_general(a, b, NT_DIMS, preferred_element_type=F32, precision=precision)


def _silu(x):
    return x * jax.nn.sigmoid(x)


def _layernorm(z, g, b):
    mu = jnp.mean(z, axis=-1, keepdims=True)
    zc = z - mu
    var = jnp.mean(zc * zc, axis=-1, keepdims=True)
    return zc * lax.rsqrt(var + LN_EPS) * g + b


def _resident(shape):
    nd = len(shape)
    return pl.BlockSpec(shape, lambda *_: (0,) * nd, pipeline_mode=pl.Buffered(1))


def _params(n_grid_dims):
    return pltpu.CompilerParams(
        dimension_semantics=("arbitrary",) * n_grid_dims, vmem_limit_bytes=VMEM_LIMIT)


def _rows_to_tiles(ref, x):
    t = x.shape[0]
    for s in range(ROW_TILES):
        ref[pl.ds(s, t, stride=ROW_TILES), :] = x[:, s * LANES:(s + 1) * LANES]


def _tiles_to_rows(ref, t):
    return jnp.concatenate([ref[pl.ds(s, t, stride=ROW_TILES), :] for s in range(ROW_TILES)], axis=1)


def _ada_kernel(c_ref, w_ref, b_ref, o_ref):
    c = c_ref[...]
    o_ref[...] = _dot(_silu(c).astype(BF16), w_ref[...].astype(BF16)) + b_ref[...]


def _adaln(c_all, ada_w, ada_b):
    depth = ada_w.shape[0]
    n = c_all.shape[0]
    d = D_MODEL
    return pl.pallas_call(
        _ada_kernel,
        grid=(depth, 6),
        in_specs=[
            pl.BlockSpec((n, d), lambda l, k: (0, 0)),
            pl.BlockSpec((None, d, d), lambda l, k: (l, 0, k)),
            pl.BlockSpec((None, None, 1, d), lambda l, k: (l, k, 0, 0)),
        ],
        out_specs=pl.BlockSpec((None, None, n, d), lambda l, k: (l, k, 0, 0)),
        out_shape=jax.ShapeDtypeStruct((depth, 6, n, d), F32),
        compiler_params=_params(2),
        name="adaln",
    )(c_all, ada_w, ada_b.reshape(depth, 6, 1, d))


class _Mods:
    def __init__(self, mods, n_sample):
        self.n_sample = n_sample
        self.m3 = mods
        self.m4 = mods.reshape(mods.shape[0], mods.shape[1], 1, D_MODEL)

    def prompt(self, lk, seq_of_step):
        ns = self.n_sample
        return self.m4, pl.BlockSpec(
            (None, None, 1, D_MODEL), lambda *g: (lk, ns + seq_of_step(*g), 0, 0))

    def sample(self, lk, rows, block_of_step):
        return self.m3, pl.BlockSpec((None, rows, D_MODEL), lambda *g: (lk, block_of_step(*g), 0))


def _qkv_kernel(x_ref, sh_ref, sc_ref, w_ref, q_ref, k_ref, v_ref):
    d = D_MODEL
    h = (x_ref[...] * (1.0 + sc_ref[...]) + sh_ref[...]).astype(BF16)
    q_ref[...] = _dot(h, w_ref[:, 0:d])
    k_ref[...] = _dot(h, w_ref[:, d:2 * d])
    v_ref[...] = _dot(h, w_ref[:, 2 * d:3 * d])


def _qkv(x, sh, sc, w_bf16, tm):
    t = x.shape[0]
    d = D_MODEL
    row = pl.BlockSpec((tm, d), lambda i: (i, 0))
    out = jax.ShapeDtypeStruct((t, d), F32)
    return pl.pallas_call(
        _qkv_kernel,
        grid=(t // tm,),
        in_specs=[row, sh[1], sc[1], _resident((d, 3 * d))],
        out_specs=[row, row, row],
        out_shape=[out, out, out],
        compiler_params=_params(1),
        name="qkv",
    )(x, sh[0], sc[0], w_bf16)


def _select_blocks(gate_t, n_valid):
    nb = gate_t.shape[0]
    blk = lax.broadcasted_iota(jnp.int32, gate_t.shape, 0)
    valid = blk < n_valid
    rows = []
    for n in range(nb):
        gn = gate_t[n:n + 1, :]
        beats = ((gate_t > gn) | ((gate_t == gn) & (blk < n))) & valid
        rows.append(jnp.sum(beats.astype(F32), axis=0, keepdims=True))
    cnt = jnp.concatenate(rows, axis=0)
    return (valid & (cnt < MOBA_TOPK)).astype(F32)


def _attn_kernel(q_ref, k_ref, v_ref, bias_ref, o_ref, kt_ref, vt_ref, kb_ref, ve_ref, km_ref, mb_ref, acc_ref):
    blk = MOBA_BLOCK
    s_len = k_ref.shape[0]
    nb = s_len // blk
    nh = HEADS_PER_STEP
    cb = pl.program_id(1)
    own = pl.program_id(2)

    @pl.when(own == 0)
    def _():
        kb_ref[...] = k_ref[...].astype(BF16)
        lane_s = lax.broadcasted_iota(jnp.int32, (s_len, LANES), 1)
        lane_k = lax.broadcasted_iota(jnp.int32, (nb, LANES), 1) // HEAD_DIM
        for pp in range(nh // 2):
            cols = slice(pp * LANES, (pp + 1) * LANES)
            kt_ref[cols, :] = jnp.transpose(k_ref[:, cols])
            vt_ref[cols, :] = jnp.transpose(v_ref[:, cols])
            rows = [jnp.mean(k_ref[n * blk:(n + 1) * blk, cols], axis=0, keepdims=True) for n in range(nb)]
            km = jnp.concatenate(rows, axis=0)
            km_ref[pp] = jnp.concatenate([jnp.where(lane_k == 0, km, 0.0), jnp.where(lane_k == 1, km, 0.0)], axis=0)
            v = v_ref[:, cols]
            ve_ref[2 * pp] = jnp.where(lane_s < HEAD_DIM, v, jnp.where(lane_s == HEAD_DIM, 1.0, 0.0)).astype(BF16)
            ve_ref[2 * pp + 1] = jnp.where(lane_s >= HEAD_DIM, v, jnp.where(lane_s == 0, 1.0, 0.0)).astype(BF16)

    qi = lax.broadcasted_iota(jnp.int32, (blk, blk), 0)
    ki = lax.broadcasted_iota(jnp.int32, (blk, blk), 1)
    causal = qi >= ki
    eye = (qi == ki).astype(BF16)
    lane = lax.broadcasted_iota(jnp.int32, (blk, LANES), 1)
    lane_head = lane // HEAD_DIM
    blk_off = (own - lane % nb).astype(F32) * float(blk)
    own_start = pl.multiple_of(own * blk, blk)

    qh = []
    for pp in range(nh // 2):
        q = q_ref[:, pp * LANES:(pp + 1) * LANES]
        qs_all = (q * (HEAD_DIM ** -0.5 * LOG2E)).astype(BF16)
        gates = _dot_nt(km_ref[pp], q, precision=HIGHEST)
        sel_t = jnp.concatenate([_select_blocks(gates[hh * nb:(hh + 1) * nb, :], own) for hh in range(2)]
                                + [jnp.zeros((LANES - 2 * nb, blk), F32)], axis=0).astype(BF16)
        sel = _dot_nt(eye, sel_t)
        head_no = (cb * nh + 2 * pp + 1 + (lane >= nb).astype(jnp.int32)).astype(F32)
        slope2 = jnp.exp2(head_no * (-8.0 / N_HEADS)) * LOG2E
        mb_ref[pp] = jnp.where(sel > 0.5, -slope2 * blk_off, NEG_INF)
        for hh in range(2):
            qh.append(jnp.where(lane_head == hh, qs_all, jnp.zeros_like(qs_all)))

    ms = []
    for h in range(nh):
        pp = h // 2
        kd = kb_ref[pl.ds(own_start, blk), pp * LANES:(pp + 1) * LANES]
        s = jnp.where(causal, _dot_nt(qh[h], kd) + bias_ref[h], NEG_INF)
        m = jnp.max(s, axis=1, keepdims=True)
        p = jnp.exp2(s - m)
        acc_ref[h] = _dot(p.astype(BF16), ve_ref[h, pl.ds(own_start, blk), :])
        ms.append(m)

    def body(j, carry):
        start = pl.multiple_of(j * blk, blk)
        out = []
        for h in range(nh):
            pp, hh = h // 2, h % 2
            kj = kb_ref[pl.ds(start, blk), pp * LANES:(pp + 1) * LANES]
            m = carry[h]
            s = _dot_nt(qh[h], kj) + bias_ref[h]
            mb = jnp.sum(jnp.where(lane == j + hh * nb, mb_ref[pp], 0.0), axis=1, keepdims=True)
            m_new = jnp.maximum(m, jnp.max(s, axis=1, keepdims=True) + mb)
            p = jnp.exp2(s + (mb - m_new))
            a = jnp.exp2(m - m_new)
            acc_ref[h] = a * acc_ref[h] + _dot(p.astype(BF16), ve_ref[h, pl.ds(start, blk), :])
            out.append(m_new)
        return tuple(out)

    lax.fori_loop(0, own, body, tuple(ms))
    for pp in range(nh // 2):
        acc0 = acc_ref[2 * pp]
        acc1 = acc_ref[2 * pp + 1]
        out = jnp.where(lane < HEAD_DIM, acc0 / acc0[:, HEAD_DIM:HEAD_DIM + 1], acc1 / acc1[:, 0:1])
        o_ref[:, pp * LANES:(pp + 1) * LANES] = out.astype(o_ref.dtype)


def _alibi_bias():
    slopes = 2.0 ** (-8.0 * jnp.arange(1, N_HEADS + 1, dtype=F32) / N_HEADS)
    qi = jnp.arange(MOBA_BLOCK, dtype=F32)[:, None]
    ki = jnp.arange(MOBA_BLOCK, dtype=F32)[None, :]
    return -(slopes * LOG2E)[:, None, None] * (qi - ki)[None]


def _moba_prompt(q, k, v):
    b, s, d = q.shape
    nb = s // MOBA_BLOCK
    nh = HEADS_PER_STEP
    w = nh * HEAD_DIM
    qspec = pl.BlockSpec((None, MOBA_BLOCK, w), lambda bi, cb, qb: (bi, qb, cb))
    kvspec = pl.BlockSpec((None, s, w), lambda bi, cb, qb: (bi, 0, cb))
    bspec = pl.BlockSpec((nh, MOBA_BLOCK, MOBA_BLOCK), lambda bi, cb, qb: (cb, 0, 0))
    tspec = pl.BlockSpec((None, w, s), lambda bi, cb, qb: (bi, cb, 0))
    return pl.pallas_call(
        _attn_kernel,
        grid=(b, d // w, nb),
        in_specs=[qspec, kvspec, kvspec, bspec],
        out_specs=[qspec, tspec, tspec],
        out_shape=[jax.ShapeDtypeStruct((b, s, d), BF16), jax.ShapeDtypeStruct((b, d, s), F32),
                   jax.ShapeDtypeStruct((b, d, s), F32)],
        scratch_shapes=[pltpu.VMEM((s, w), BF16), pltpu.VMEM((nh, s, LANES), BF16),
                        pltpu.VMEM((nh // 2, 2 * nb, LANES), F32),
                        pltpu.VMEM((nh // 2, MOBA_BLOCK, LANES), F32),
                        pltpu.VMEM((nh, MOBA_BLOCK, LANES), F32)],
        compiler_params=_params(3),
        name="moba_prompt",
    )(q, k, v, _alibi_bias())


def _attn_sample_probs_kernel(n_pages, pt_ref, q_ref, qt_ref, kn_ref, *refs):
    del pt_ref
    kp = refs[:n_pages]
    p_ref, misc_ref, s_ref = refs[n_pages:]
    past = n_pages * PAGE_SIZE
    nb = past // MOBA_BLOCK
    scale = HEAD_DIM ** -0.5

    q = q_ref[...]
    qt = qt_ref[...]
    for h in range(N_HEADS):
        qb = jnp.broadcast_to(qt[:, h:h + 1], (HEAD_DIM, PAGE_SIZE))
        for p in range(n_pages):
            s_ref[h:h + 1, p * PAGE_SIZE:(p + 1) * PAGE_SIZE] = jnp.sum(kp[p][h] * qb, axis=0, keepdims=True)
    raw = s_ref[...]
    gates = [jnp.sum(raw[:, n * MOBA_BLOCK:(n + 1) * MOBA_BLOCK], axis=1, keepdims=True) * (1.0 / MOBA_BLOCK)
             for n in range(nb)]
    lane_blk = lax.broadcasted_iota(jnp.int32, (N_HEADS, past), 1) // MOBA_BLOCK
    keep = jnp.zeros((N_HEADS, past), jnp.bool_)
    n_kept = jnp.zeros((N_HEADS, 1), F32)
    kept_ids = [jnp.zeros((N_HEADS, 1), F32) for _ in range(MOBA_TOPK)]
    for n in range(nb):
        cnt = jnp.zeros((N_HEADS, 1), F32)
        for j in range(nb):
            if j != n:
                beats = (gates[j] > gates[n]) | ((gates[j] == gates[n]) & (j < n))
                cnt = cnt + beats.astype(F32)
        keep_n = cnt < MOBA_TOPK
        keep = keep | ((lane_blk == n) & keep_n)
        for j in range(MOBA_TOPK):
            kept_ids[j] = jnp.where(keep_n & (n_kept == j), float(n), kept_ids[j])
        n_kept = n_kept + keep_n.astype(F32)
    head = lax.broadcasted_iota(jnp.int32, (N_HEADS, 1), 0)
    slope = jnp.exp2((head + 1).astype(F32) * (-8.0 / N_HEADS))
    kpos = lax.broadcasted_iota(jnp.int32, (1, past), 1).astype(F32)
    s = jnp.where(keep, raw * scale - slope * (float(past) - kpos), NEG_INF)
    s_own = jnp.sum(q * kn_ref[...], axis=1, keepdims=True) * scale
    m = jnp.maximum(jnp.max(s, axis=1, keepdims=True), s_own)
    e = jnp.exp(s - m)
    w_own = jnp.exp(s_own - m)
    inv = 1.0 / (jnp.sum(e, axis=1, keepdims=True) + w_own)
    p = e * inv
    for pg in range(n_pages):
        p_ref[:, pg, :] = p[:, pg * PAGE_SIZE:(pg + 1) * PAGE_SIZE]
    lane = lax.broadcasted_iota(jnp.int32, (N_HEADS, LANES), 1)
    misc = jnp.where(lane == MOBA_TOPK, w_own * inv, 0.0)
    for j in range(MOBA_TOPK):
        misc = jnp.where(lane == j, kept_ids[j], misc)
    misc_ref[...] = misc


def _attn_sample_values_kernel(n_pages, layer, pt_ref, kept_ref, p_ref, misc_ref, vnt_ref, cv_ref, o_ref,
                               vbuf_ref, sem):
    ppb = MOBA_BLOCK // PAGE_SIZE
    per_head = MOBA_TOPK * ppb
    n_chunks = N_HEADS * per_head
    i = pl.program_id(0)
    cur = i % 2

    def chunk_copy(seq, c, slot):
        h, j, pg = c // per_head, (c % per_head) // ppb, c % ppb
        page = pt_ref[seq * n_pages + kept_ref[(seq * N_HEADS + h) * MOBA_TOPK + j] * ppb + pg]
        return pltpu.make_async_copy(cv_ref.at[layer, page, h], vbuf_ref.at[slot, c], sem.at[slot])

    def fetch(seq, slot):
        for c in range(n_chunks):
            chunk_copy(seq, c, slot).start()

    @pl.when(i == 0)
    def _():
        fetch(0, 0)

    for slot in range(2):
        @pl.when((i + 1 < pl.num_programs(0)) & (cur != slot))
        def _(slot=slot):
            fetch(i + 1, slot)

    for slot in range(2):
        @pl.when(cur == slot)
        def _(slot=slot):
            for c in range(n_chunks):
                chunk_copy(i, c, slot).wait()

    misc = misc_ref[...]
    lane_h = lax.broadcasted_iota(jnp.int32, (HEAD_DIM, N_HEADS), 1)
    out_t = jnp.zeros((HEAD_DIM, N_HEADS), F32)
    for h in range(N_HEADS):
        acc = jnp.zeros((HEAD_DIM, PAGE_SIZE), F32)
        for j in range(MOBA_TOPK):
            blk = kept_ref[(i * N_HEADS + h) * MOBA_TOPK + j]
            for pg in range(ppb):
                w = p_ref[h, pl.ds(blk * ppb + pg, 1), :]
                acc = acc + vbuf_ref[cur, h * per_head + j * ppb + pg] * w
        w_own = misc[h:h + 1, MOBA_TOPK:MOBA_TOPK + 1]
        col = jnp.sum(acc, axis=1, keepdims=True) + w_own * vnt_ref[:, h:h + 1]
        out_t = jnp.where(lane_h == h, col, out_t)
    o_ref[...] = out_t


def _moba_sample(q, k_new, v_new, cache_k, cache_v, layer, page_table):
    n_seq, n_pages = page_table.shape
    assert n_pages * PAGE_SIZE // MOBA_BLOCK >= MOBA_TOPK
    hd = (N_HEADS, HEAD_DIM)
    dh = (HEAD_DIM, N_HEADS)
    ck = jnp.transpose(cache_k, (0, 1, 3, 4, 2))
    cv = jnp.transpose(cache_v, (0, 1, 3, 4, 2))
    pt = page_table.reshape(-1).astype(jnp.int32)
    q3 = q.reshape((n_seq,) + hd)
    hd_spec = pl.BlockSpec((None,) + hd, lambda i, *_: (i, 0, 0))
    dh_spec = pl.BlockSpec((None,) + dh, lambda i, *_: (i, 0, 0))
    p_spec = pl.BlockSpec((None, N_HEADS, n_pages, PAGE_SIZE), lambda i, *_: (i, 0, 0, 0))
    misc_spec = pl.BlockSpec((None, N_HEADS, LANES), lambda i, *_: (i, 0, 0))

    def page_spec(p):
        return pl.BlockSpec((None, None) + hd + (PAGE_SIZE,),
                            lambda i, pt: (layer, pt[i * n_pages + p], 0, 0, 0))

    probs, misc = pl.pallas_call(
        functools.partial(_attn_sample_probs_kernel, n_pages),
        grid_spec=pltpu.PrefetchScalarGridSpec(
            num_scalar_prefetch=1,
            grid=(n_seq,),
            in_specs=[hd_spec, dh_spec, hd_spec] + [page_spec(p) for p in range(n_pages)],
            out_specs=[p_spec, misc_spec],
            scratch_shapes=[pltpu.VMEM((N_HEADS, n_pages * PAGE_SIZE), F32)],
        ),
        out_shape=[jax.ShapeDtypeStruct((n_seq, N_HEADS, n_pages, PAGE_SIZE), F32),
                   jax.ShapeDtypeStruct((n_seq, N_HEADS, LANES), F32)],
        compiler_params=_params(1),
        name="moba_sample_probs",
    )(pt, q3, q3.transpose(0, 2, 1), k_new.reshape((n_seq,) + hd), *([ck] * n_pages))

    kept = misc[:, :, 0:MOBA_TOPK].astype(jnp.int32).reshape(-1)
    n_chunks = N_HEADS * MOBA_TOPK * (MOBA_BLOCK // PAGE_SIZE)
    out_t = pl.pallas_call(
        functools.partial(_attn_sample_values_kernel, n_pages, layer),
        grid_spec=pltpu.PrefetchScalarGridSpec(
            num_scalar_prefetch=2,
            grid=(n_seq,),
            in_specs=[p_spec, misc_spec, dh_spec, pl.BlockSpec(memory_space=pl.ANY)],
            out_specs=dh_spec,
            scratch_shapes=[pltpu.VMEM((2, n_chunks, HEAD_DIM, PAGE_SIZE), F32), pltpu.SemaphoreType.DMA((2,))],
        ),
        out_shape=jax.ShapeDtypeStruct((n_seq,) + dh, F32),
        compiler_params=_params(1),
        name="moba_sample_values",
    )(pt, kept, probs, misc, v_new.reshape((n_seq,) + hd).transpose(0, 2, 1), cv)
    return out_t.transpose(0, 2, 1).reshape(n_seq, D_MODEL).astype(BF16)


def _post_kernel(alpha, a_ref, x_ref, gm_ref, shf_ref, scf_ref, gf_ref, lng_ref, lnb_ref,
                 wo_ref, w1_ref, w3_ref, w2_ref, o_ref, g_ref):
    m = _dot(a_ref[...], wo_ref[...])
    y1 = _layernorm(alpha * x_ref[...] + (1.0 + gm_ref[...]) * m, lng_ref[0:1, :], lnb_ref[0:1, :])
    h = (y1 * (1.0 + scf_ref[...]) + shf_ref[...]).astype(BF16)
    _swiglu_hidden(h, w1_ref, w3_ref, g_ref, FFN_CHUNK)
    f = _dot(g_ref[...], w2_ref[...])
    o_ref[...] = _layernorm(alpha * y1 + (1.0 + gf_ref[...]) * f, lng_ref[1:2, :], lnb_ref[1:2, :])


def _swiglu_hidden(h, w1_ref, w3_ref, g_ref, chunk):
    for c in range(w1_ref.shape[1] // chunk):
        cols = slice(c * chunk, (c + 1) * chunk)
        t1 = _dot(h, w1_ref[:, cols])
        t3 = _dot(h, w3_ref[:, cols])
        g_ref[:, cols] = (_silu(t1) * t3).astype(BF16)


def _post_attn(attn, x, gm, shf, scf, gf, ln_g, ln_b, wo, w1, w3, w2, tm, alpha):
    t = x.shape[0]
    d = D_MODEL
    row = pl.BlockSpec((tm, d), lambda i: (i, 0))
    return pl.pallas_call(
        functools.partial(_post_kernel, alpha),
        grid=(t // tm,),
        in_specs=[row, row, gm[1], shf[1], scf[1], gf[1], _resident((2, d)), _resident((2, d)),
                  _resident(wo.shape), _resident(w1.shape), _resident(w3.shape), _resident(w2.shape)],
        out_specs=row,
        out_shape=jax.ShapeDtypeStruct((t, d), F32),
        scratch_shapes=[pltpu.VMEM((tm, w1.shape[1]), BF16)],
        compiler_params=_params(1),
        name="post_attn_ffn",
    )(attn, x, gm[0], shf[0], scf[0], gf[0], ln_g, ln_b, wo, w1, w3, w2)


def _glu(y, shm_ref, scm_ref, wpw1_ref, bpw1_ref):
    d = D_MODEL
    h = (y * (1.0 + scm_ref[...]) + shm_ref[...]).astype(BF16)
    a = _dot(h, wpw1_ref[:, 0:d]) + bpw1_ref[:, 0:d]
    g = _dot(h, wpw1_ref[:, d:2 * d]) + bpw1_ref[:, d:2 * d]
    return a * jax.nn.sigmoid(g)


def _conv_tail(alpha, y, conv, gm_ref, shf_ref, scf_ref, lng_ref, lnb_ref, clg_ref, clb_ref,
               wpw2_ref, bpw2_ref, wr_ref, br_ref, y3_ref, h2_ref, lg_ref):
    z = _silu(_layernorm(conv, clg_ref[...], clb_ref[...])).astype(BF16)
    m = _dot(z, wpw2_ref[...]) + bpw2_ref[...]
    y3 = _layernorm(alpha * y + (1.0 + gm_ref[...]) * m, lng_ref[0:1, :], lnb_ref[0:1, :])
    y3_ref[...] = y3
    h2 = y3 * (1.0 + scf_ref[...]) + shf_ref[...]
    _rows_to_tiles(h2_ref, h2)
    h2_hi = h2.astype(BF16)
    h2_lo = (h2 - h2_hi.astype(F32)).astype(BF16)
    lg_ref[...] = (_dot(h2_hi, wr_ref[0]) + _dot(h2_lo, wr_ref[0]) + _dot(h2_hi, wr_ref[1])) + br_ref[...]


def _conv_prompt_kernel(alpha, y_ref, shm_ref, scm_ref, gm_ref, shf_ref, scf_ref, lng_ref, lnb_ref,
                        wpw1_ref, bpw1_ref, wdw_ref, bdw_ref, clg_ref, clb_ref, wpw2_ref, bpw2_ref,
                        wr_ref, br_ref, y3_ref, h2_ref, lg_ref, tail_ref, ext_ref, shift_ref, conv_ref):
    tm = y_ref.shape[0]
    d = D_MODEL
    rows_ext = tm + HIST_PAD

    @pl.when(pl.program_id(1) == 0)
    def _():
        ext_ref[0:HIST_PAD, :] = jnp.zeros((HIST_PAD, d), F32)

    y = y_ref[...]
    u = _glu(y, shm_ref, scm_ref, wpw1_ref, bpw1_ref)
    ext_ref[HIST_PAD:rows_ext, :] = u
    tail_ref[...] = u[tm - HIST_PAD:tm, :]
    span = rows_ext - SUBLANES
    for b in range(1, SUBLANES):
        shift_ref[b - 1, :, :] = ext_ref[b:b + span, :]
    base = HIST_PAD - (CONV_WIDTH - 1)
    rb = 4 * SUBLANES

    def body(r, carry):
        r0 = pl.multiple_of(r * rb, rb)
        acc = jnp.broadcast_to(bdw_ref[...], (rb, d))
        for k in range(CONV_WIDTH):
            off = base + k
            start = r0 + (off // SUBLANES) * SUBLANES
            if off % SUBLANES == 0:
                x = ext_ref[pl.ds(start, rb), :]
            else:
                x = shift_ref[off % SUBLANES - 1, pl.ds(start, rb), :]
            acc = acc + wdw_ref[k:k + 1, :] * x
        conv_ref[pl.ds(r0, rb), :] = acc
        return carry

    lax.fori_loop(0, tm // rb, body, 0)
    ext_ref[0:HIST_PAD, :] = u[tm - HIST_PAD:tm, :]
    _conv_tail(alpha, y, conv_ref[...], gm_ref, shf_ref, scf_ref, lng_ref, lnb_ref, clg_ref, clb_ref,
               wpw2_ref, bpw2_ref, wr_ref, br_ref, y3_ref, h2_ref, lg_ref)


def _conv_sample_kernel(alpha, y_ref, shm_ref, scm_ref, gm_ref, shf_ref, scf_ref, lng_ref, lnb_ref,
                        wpw1_ref, bpw1_ref, wdw_ref, bdw_ref, clg_ref, clb_ref, wpw2_ref, bpw2_ref,
                        wr_ref, br_ref, hist_ref, y3_ref, h2_ref, lg_ref, u_ref):
    nh = CONV_WIDTH - 1
    y = y_ref[...]
    u = _glu(y, shm_ref, scm_ref, wpw1_ref, bpw1_ref)
    u_ref[...] = u
    w = wdw_ref[...]
    conv = bdw_ref[...] + w[nh:nh + 1, :] * u
    for k in range(nh):
        conv = conv + w[k:k + 1, :] * hist_ref[k]
    _conv_tail(alpha, y, conv, gm_ref, shf_ref, scf_ref, lng_ref, lnb_ref, clg_ref, clb_ref,
               wpw2_ref, bpw2_ref, wr_ref, br_ref, y3_ref, h2_ref, lg_ref)


def _conv_weight_specs(d):
    return [_resident((2, d)), _resident((2, d)), _resident((d, 2 * d)), _resident((1, 2 * d)),
            _resident((CONV_WIDTH, d)), _resident((1, d)), _resident((1, d)), _resident((1, d)),
            _resident((d, d)), _resident((1, d)), _resident((2, d, LANES)), _resident((1, LANES))]


def _conv_prompt(y, mods5, weights, n_seq, seq_len, alpha):
    d = D_MODEL
    tm = TM_CONV
    tps = seq_len // tm
    t = n_seq * seq_len
    row = pl.BlockSpec((tm, d), lambda b, i: (b * tps + i, 0))
    trow = pl.BlockSpec((tm * ROW_TILES, LANES), lambda b, i: (b * tps + i, 0))
    lgrow = pl.BlockSpec((tm, LANES), lambda b, i: (b * tps + i, 0))
    tail = pl.BlockSpec((None, HIST_PAD, d), lambda b, i: (b, 0, 0))
    return pl.pallas_call(
        functools.partial(_conv_prompt_kernel, alpha),
        grid=(n_seq, tps),
        in_specs=[row] + [m[1] for m in mods5] + _conv_weight_specs(d),
        out_specs=[row, trow, lgrow, tail],
        out_shape=[jax.ShapeDtypeStruct((t, d), F32),
                   jax.ShapeDtypeStruct((t * ROW_TILES, LANES), F32),
                   jax.ShapeDtypeStruct((t, LANES), F32),
                   jax.ShapeDtypeStruct((n_seq, HIST_PAD, d), F32)],
        scratch_shapes=[pltpu.VMEM((tm + HIST_PAD, d), F32),
                        pltpu.VMEM((SUBLANES - 1, tm + HIST_PAD - SUBLANES, d), F32),
                        pltpu.VMEM((tm, d), F32)],
        compiler_params=_params(2),
        name="conv_prompt",
    )(y, *[m[0] for m in mods5], *weights)


def _conv_sample(y, mods5, weights, hist, alpha):
    d = D_MODEL
    n = y.shape[0]
    tm = 32
    row = pl.BlockSpec((tm, d), lambda i: (i, 0))
    return pl.pallas_call(
        functools.partial(_conv_sample_kernel, alpha),
        grid=(n // tm,),
        in_specs=[row] + [m[1] for m in mods5] + _conv_weight_specs(d)
        + [pl.BlockSpec((CONV_WIDTH - 1, tm, d), lambda i: (0, i, 0))],
        out_specs=[row, pl.BlockSpec((tm * ROW_TILES, LANES), lambda i: (i, 0)),
                   pl.BlockSpec((tm, LANES), lambda i: (i, 0)), row],
        out_shape=[jax.ShapeDtypeStruct((n, d), F32),
                   jax.ShapeDtypeStruct((n * ROW_TILES, LANES), F32),
                   jax.ShapeDtypeStruct((n, LANES), F32),
                   jax.ShapeDtypeStruct((n, d), F32)],
        compiler_params=_params(1),
        name="conv_sample",
    )(y, *[m[0] for m in mods5], *weights, hist)


def _route_tile(lg_ref, meta_ref, meta_t_ref, carry_ref):
    tm = lg_ref.shape[0]
    lane = lax.broadcasted_iota(jnp.int32, (tm, LANES), 1)
    lg = jnp.where(lane < N_EXPERTS, lg_ref[...], NEG_INF)
    m1 = jnp.max(lg, axis=1, keepdims=True)
    i1 = jnp.min(jnp.where(lg == m1, lane, LANES), axis=1, keepdims=True)
    lg2 = jnp.where(lane == i1, NEG_INF, lg)
    m2 = jnp.max(lg2, axis=1, keepdims=True)
    i2 = jnp.min(jnp.where(lg2 == m2, lane, LANES), axis=1, keepdims=True)
    e = jnp.exp(m2 - m1)
    g1 = 1.0 / (1.0 + e)
    g2 = e / (1.0 + e)
    hot1 = lane == i1
    hot2 = lane == i2
    onehot = (hot1 | hot2).astype(BF16)
    ri = lax.broadcasted_iota(jnp.int32, (tm, tm), 0)
    ci = lax.broadcasted_iota(jnp.int32, (tm, tm), 1)
    before = (ci < ri).astype(BF16)
    rank = _dot(before, onehot) + carry_ref[...]
    r1 = jnp.sum(jnp.where(hot1, rank, 0.0), axis=1, keepdims=True)
    r2 = jnp.sum(jnp.where(hot2, rank, 0.0), axis=1, keepdims=True)
    cols = (i1.astype(F32), i2.astype(F32), g1, g2, r1, r2)
    meta = jnp.zeros((tm, LANES), F32)
    for c, val in enumerate(cols):
        meta = jnp.where(lane == c, val, meta)
    meta_ref[...] = meta
    meta_t_ref[...] = jnp.transpose(meta)[0:SUBLANES, :]
    carry_ref[...] += jnp.sum(onehot.astype(F32), axis=0, keepdims=True)


def _route_kernel(n_prompt_tiles, lgp_ref, lgs_ref, mp_ref, mpt_ref, ms_ref, mst_ref, cnt_ref, carry_ref):
    i = pl.program_id(0)

    @pl.when(i == 0)
    def _():
        carry_ref[...] = jnp.zeros_like(carry_ref)

    @pl.when(i < n_prompt_tiles)
    def _():
        _route_tile(lgp_ref, mp_ref, mpt_ref, carry_ref)

    @pl.when(i == n_prompt_tiles)
    def _():
        _route_tile(lgs_ref, ms_ref, mst_ref, carry_ref)
        cnt_ref[...] = carry_ref[...]


def _route(lg_p, lg_s):
    tp, ts = lg_p.shape[0], lg_s.shape[0]
    tm = TM_ROUTE
    npt = tp // tm
    last = npt - 1
    return pl.pallas_call(
        functools.partial(_route_kernel, npt),
        grid=(npt + 1,),
        in_specs=[pl.BlockSpec((tm, LANES), lambda i: (jnp.minimum(i, last), 0)),
                  pl.BlockSpec((ts, LANES), lambda i: (0, 0))],
        out_specs=[pl.BlockSpec((tm, LANES), lambda i: (jnp.minimum(i, last), 0)),
                   pl.BlockSpec((SUBLANES, tm), lambda i: (0, jnp.minimum(i, last))),
                   pl.BlockSpec((ts, LANES), lambda i: (0, 0)),
                   pl.BlockSpec((SUBLANES, ts), lambda i: (0, 0)),
                   pl.BlockSpec((1, LANES), lambda i: (0, 0))],
        out_shape=[jax.ShapeDtypeStruct((tp, LANES), F32),
                   jax.ShapeDtypeStruct((SUBLANES, tp), F32),
                   jax.ShapeDtypeStruct((ts, LANES), F32),
                   jax.ShapeDtypeStruct((SUBLANES, ts), F32),
                   jax.ShapeDtypeStruct((1, LANES), F32)],
        scratch_shapes=[pltpu.VMEM((1, LANES), F32)],
        compiler_params=_params(1),
        name="moe_route",
    )(lg_p, lg_s)


def _row_copy(src_ref, src_row, dst_ref, dst_row, sem, n=1):
    return pltpu.make_async_copy(
        src_ref.at[pl.ds(pl.multiple_of(src_row * ROW_TILES, ROW_TILES), n * ROW_TILES), :],
        dst_ref.at[pl.ds(pl.multiple_of(dst_row * ROW_TILES, ROW_TILES), n * ROW_TILES), :], sem)


def _scatter_rows(slots_ref, h_ref, xs_ref, sem):
    tm = h_ref.shape[0] // ROW_TILES

    def issue(r, carry):
        for c in range(MOE_TOPK):
            _row_copy(h_ref, r, xs_ref, slots_ref[0, c * tm + r], sem).start()
        return carry

    lax.fori_loop(0, tm, issue, 0)
    for c in range(MOE_TOPK):
        _row_copy(h_ref, 0, xs_ref, 0, sem, n=tm).wait()


def _scatter_kernel(n_prompt_tiles, zlo_ref, zhi_ref, sp_ref, ss_ref, hp_ref, hs_ref, xs_ref,
                    zero_ref, sem, zsem):
    i = pl.program_id(0)

    @pl.when(i < n_prompt_tiles)
    def _():
        _scatter_rows(sp_ref, hp_ref, xs_ref, sem)

    @pl.when(i == n_prompt_tiles)
    def _():
        _scatter_rows(ss_ref, hs_ref, xs_ref, sem)
        zero_ref[...] = jnp.zeros_like(zero_ref)
        for e in range(N_EXPERTS):
            lo = zlo_ref[e]
            n = zhi_ref[e] - lo
            nbig = n // ZERO_ROWS

            def big(k, carry, lo=lo):
                return _row_copy(zero_ref, 0, xs_ref, lo + k * ZERO_ROWS, zsem, n=ZERO_ROWS)

            def small(r, carry, lo=lo):
                return _row_copy(zero_ref, 0, xs_ref, lo + r, zsem)

            lax.fori_loop(0, nbig, lambda k, c: (big(k, c).start(), c)[1], 0)
            lax.fori_loop(nbig * ZERO_ROWS, n, lambda r, c: (small(r, c).start(), c)[1], 0)
            lax.fori_loop(0, nbig, lambda k, c: (big(k, c).wait(), c)[1], 0)
            lax.fori_loop(nbig * ZERO_ROWS, n, lambda r, c: (small(r, c).wait(), c)[1], 0)


def _scatter(h2t_p, h2t_s, slots_p, slots_s, zero_lo, zero_hi, n_rows):
    tp = h2t_p.shape[0] // ROW_TILES
    ts = h2t_s.shape[0] // ROW_TILES
    tm = TM_ROUTE
    npt = tp // tm
    last = npt - 1
    grid_spec = pltpu.PrefetchScalarGridSpec(
        num_scalar_prefetch=2,
        grid=(npt + 1,),
        in_specs=[pl.BlockSpec((None, 1, MOE_TOPK * tm), lambda i, lo, hi: (jnp.minimum(i, last), 0, 0),
                               memory_space=pltpu.SMEM),
                  pl.BlockSpec((None, 1, MOE_TOPK * ts), lambda i, lo, hi: (0, 0, 0), memory_space=pltpu.SMEM),
                  pl.BlockSpec((tm * ROW_TILES, LANES), lambda i, lo, hi: (jnp.minimum(i, last), 0)),
                  pl.BlockSpec((ts * ROW_TILES, LANES), lambda i, lo, hi: (0, 0))],
        out_specs=pl.BlockSpec(memory_space=pl.ANY),
        scratch_shapes=[pltpu.VMEM((ZERO_ROWS * ROW_TILES, LANES), F32), pltpu.SemaphoreType.DMA(()),
                        pltpu.SemaphoreType.DMA(())],
    )
    return pl.pallas_call(
        functools.partial(_scatter_kernel, npt),
        grid_spec=grid_spec,
        out_shape=jax.ShapeDtypeStruct((n_rows * ROW_TILES, LANES), F32),
        compiler_params=pltpu.CompilerParams(dimension_semantics=("arbitrary",),
                                             vmem_limit_bytes=VMEM_LIMIT, has_side_effects=True),
        name="moe_scatter",
    )(zero_lo, zero_hi, slots_p, slots_s, h2t_p, h2t_s)


def _gmm_kernel(exp_ref, nv_ref, x_ref, w1_ref, w3_ref, w2_ref, o_ref, g_ref):
    del exp_ref
    tg = g_ref.shape[0]
    live = pl.program_id(0) < nv_ref[0]

    @pl.when(live)
    def _():
        x = _tiles_to_rows(x_ref, tg).astype(BF16)
        _swiglu_hidden(x, w1_ref, w3_ref, g_ref, TF_GMM)
        _rows_to_tiles(o_ref, _dot(g_ref[...], w2_ref[...]))

    @pl.when(jnp.logical_not(live))
    def _():
        o_ref[...] = jnp.zeros_like(o_ref)


def _gmm(xs, w1, w3, w2, tile_exp, n_valid):
    d = D_MODEL
    tg = TM_GMM
    dfe = w1.shape[2]
    n_tiles = tile_exp.shape[0]
    one = pl.Buffered(1)
    grid_spec = pltpu.PrefetchScalarGridSpec(
        num_scalar_prefetch=2,
        grid=(n_tiles,),
        in_specs=[
            pl.BlockSpec((tg * ROW_TILES, LANES), lambda m, ex, nv: (m, 0)),
            pl.BlockSpec((None, d, dfe), lambda m, ex, nv: (ex[m], 0, 0), pipeline_mode=one),
            pl.BlockSpec((None, d, dfe), lambda m, ex, nv: (ex[m], 0, 0), pipeline_mode=one),
            pl.BlockSpec((None, dfe, d), lambda m, ex, nv: (ex[m], 0, 0), pipeline_mode=one),
        ],
        out_specs=pl.BlockSpec((tg * ROW_TILES, LANES), lambda m, ex, nv: (m, 0)),
        scratch_shapes=[pltpu.VMEM((tg, dfe), BF16)],
    )
    return pl.pallas_call(
        _gmm_kernel,
        grid_spec=grid_spec,
        out_shape=jax.ShapeDtypeStruct(xs.shape, F32),
        compiler_params=_params(1),
        name="moe_experts",
    )(tile_exp, n_valid, xs, w1, w3, w2)


def _combine_kernel(alpha, slots_ref, next_slots_ref, ys_ref, y_ref, meta_ref, gf_ref, lng_ref, lnb_ref,
                    o_ref, buf_ref, sem):
    tm = y_ref.shape[0]
    i = pl.program_id(0)
    cur = i % 2

    def gather(sl_ref, slot):
        def issue(r, carry):
            for c in range(MOE_TOPK):
                _row_copy(ys_ref, sl_ref[0, c * tm + r], buf_ref.at[slot, c], r, sem.at[slot]).start()
            return carry

        lax.fori_loop(0, tm, issue, 0)

    @pl.when(i == 0)
    def _():
        gather(slots_ref, 0)

    for slot in range(2):
        @pl.when((i + 1 < pl.num_programs(0)) & (cur != slot))
        def _(slot=slot):
            gather(next_slots_ref, slot)

    for c in range(MOE_TOPK):
        _row_copy(ys_ref, 0, buf_ref.at[cur, c], 0, sem.at[cur], n=tm).wait()
    meta = meta_ref[...]
    f = jnp.zeros((tm, D_MODEL), F32)
    for c in range(MOE_TOPK):
        f = f + _tiles_to_rows(buf_ref.at[cur, c], tm) * meta[:, META_GATE + c:META_GATE + c + 1]
    o_ref[...] = _layernorm(alpha * y_ref[...] + (1.0 + gf_ref[...]) * f,
                            lng_ref[1:2, :], lnb_ref[1:2, :])


def _combine(ys, y3, slots3, meta, gf, ln_g, ln_b, tm, alpha):
    t, d = y3.shape
    last = t // tm - 1

    def slot_spec(ahead):
        return pl.BlockSpec((None, 1, MOE_TOPK * tm), lambda i: (jnp.minimum(i + ahead, last), 0, 0),
                            memory_space=pltpu.SMEM)

    return pl.pallas_call(
        functools.partial(_combine_kernel, alpha),
        grid=(t // tm,),
        in_specs=[slot_spec(0), slot_spec(1),
                  pl.BlockSpec(memory_space=pl.ANY),
                  pl.BlockSpec((tm, d), lambda i: (i, 0)),
                  pl.BlockSpec((tm, LANES), lambda i: (i, 0)),
                  gf[1], _resident((2, d)), _resident((2, d))],
        out_specs=pl.BlockSpec((tm, d), lambda i: (i, 0)),
        out_shape=jax.ShapeDtypeStruct((t, d), F32),
        scratch_shapes=[pltpu.VMEM((2, MOE_TOPK, tm * ROW_TILES, LANES), F32), pltpu.SemaphoreType.DMA((2,))],
        compiler_params=_params(1),
        name="moe_combine",
    )(slots3, slots3, ys, y3, meta, gf[0], ln_g, ln_b)


def _tile_slots(slots, tm):
    k, t = slots.shape
    return slots.reshape(k, t // tm, tm).transpose(1, 0, 2).reshape(t // tm, 1, k * tm)


def kernel(x_prompt, x_sample, cache_k, cache_v, state_conv, page_table, c_prompt, c_sample,
           ada_w, ada_b, ln_g, ln_b, attn_w_qkv, attn_w_o,
           conv_w_pw1, conv_b_pw1, conv_w_dw, conv_b_dw, conv_ln_g, conv_ln_b, conv_w_pw2, conv_b_pw2,
           ffn_w1, ffn_w3, ffn_w2, moe_w_router, moe_b_router, moe_w1, moe_w3, moe_w2):
    b, s, d = x_prompt.shape
    db, ds, _ = x_sample.shape
    depth = ada_w.shape[0]
    assert d == D_MODEL and ds == 1 and depth == 2
    assert page_table.shape[1] * PAGE_SIZE % MOBA_BLOCK == 0
    alpha = (2 * depth) ** 0.25
    tp = b * s
    t_all = tp + db
    assert tp % TM_ROUTE == 0 and tp % TM_COMBINE == 0

    c_all = jnp.concatenate([c_sample, c_prompt], axis=0)
    mods = _adaln(c_all, ada_w, ada_b)
    mods = _Mods(mods.reshape(depth * 6, db + b, d), db)

    def pmod(layer, k, tm):
        tps = s // tm
        return mods.prompt(layer * 6 + k, lambda i: i // tps)

    def smod(layer, k, rows=None, block_of_step=lambda i: 0):
        return mods.sample(layer * 6 + k, db if rows is None else rows, block_of_step)

    xp = x_prompt.reshape(tp, d)
    xs_ = x_sample.reshape(db, d)

    wqkv = attn_w_qkv[0].astype(BF16)
    qp, kp, vp = _qkv(xp, pmod(0, 0, TM_QKV), pmod(0, 1, TM_QKV), wqkv, TM_QKV)
    qs, ks, vs = _qkv(xs_, smod(0, 0), smod(0, 1), wqkv, db)
    attn_p, kt_p, vt_p = _moba_prompt(qp.reshape(b, s, d), kp.reshape(b, s, d), vp.reshape(b, s, d))
    attn_p = attn_p.reshape(tp, d)
    attn_s = _moba_sample(qs, ks, vs, cache_k, cache_v, 0, page_table)

    assert ffn_w1.shape[2] % FFN_CHUNK == 0 and moe_w1.shape[3] % TF_GMM == 0
    wo = attn_w_o[0].astype(BF16)
    ffn_w = (ffn_w1[0].astype(BF16), ffn_w3[0].astype(BF16), ffn_w2[0].astype(BF16))
    y_p = _post_attn(attn_p, xp, pmod(0, 2, TM_POST), pmod(0, 3, TM_POST), pmod(0, 4, TM_POST),
                     pmod(0, 5, TM_POST), ln_g[0], ln_b[0], wo, *ffn_w, TM_POST, alpha)
    y_s = _post_attn(attn_s, xs_, smod(0, 2), smod(0, 3), smod(0, 4), smod(0, 5),
                     ln_g[0], ln_b[0], wo, *ffn_w, db, alpha)

    wr = jnp.pad(moe_w_router[0], ((0, 0), (0, LANES - N_EXPERTS)))
    wr_hi = wr.astype(BF16)
    wr = jnp.stack([wr_hi, (wr - wr_hi.astype(F32)).astype(BF16)])
    br = jnp.pad(moe_b_router[0], (0, LANES - N_EXPERTS)).reshape(1, LANES)
    conv_weights = (ln_g[1], ln_b[1], conv_w_pw1[0].astype(BF16), conv_b_pw1[0].reshape(1, 2 * d),
                    conv_w_dw[0], conv_b_dw[0].reshape(1, d), conv_ln_g[0].reshape(1, d),
                    conv_ln_b[0].reshape(1, d), conv_w_pw2[0].astype(BF16), conv_b_pw2[0].reshape(1, d),
                    wr, br)
    pm = [mods.prompt(6 + k, lambda bi, i: bi) for k in range(5)]
    y3_p, h2t_p, lg_p, tail = _conv_prompt(y_p, pm, conv_weights, b, s, alpha)
    sm = [smod(1, k, 32, lambda i: i) for k in range(5)]
    hist_t = jnp.transpose(state_conv[0], (1, 0, 2))
    y3_s, h2t_s, lg_s, u_s = _conv_sample(y_s, sm, conv_weights, hist_t, alpha)

    meta_p, meta_pt, meta_s, meta_st, cnt = _route(lg_p, lg_s)
    counts = cnt[0, 0:N_EXPERTS].astype(jnp.int32)
    padded = (counts + TM_GMM - 1) // TM_GMM * TM_GMM
    pad_end = jnp.cumsum(padded)
    pad_start = pad_end - padded
    n_tiles = -(-t_all * MOE_TOPK // TM_GMM) + N_EXPERTS
    n_rows = n_tiles * TM_GMM

    def slots_of(meta_t):
        e = meta_t[META_EXPERT:META_EXPERT + MOE_TOPK].astype(jnp.int32)
        slot = meta_t[META_RANK:META_RANK + MOE_TOPK].astype(jnp.int32)
        for x in range(N_EXPERTS):
            slot = slot + jnp.where(e == x, pad_start[x], 0)
        return slot

    slots_p, slots_s = slots_of(meta_pt), slots_of(meta_st)
    zero_lo = pad_start + counts
    zero_hi = jnp.concatenate([pad_start[1:], jnp.full((1,), n_rows, jnp.int32)])
    xs_buf = _scatter(h2t_p, h2t_s, _tile_slots(slots_p, TM_ROUTE), _tile_slots(slots_s, db),
                      zero_lo.astype(jnp.int32), zero_hi.astype(jnp.int32), n_rows)

    tile_ids = jnp.arange(n_tiles, dtype=jnp.int32)
    tile_exp = jnp.minimum(jnp.sum((pad_end // TM_GMM)[None, :] <= tile_ids[:, None], axis=1),
                           N_EXPERTS - 1).astype(jnp.int32)
    n_valid = (pad_end[-1] // TM_GMM).reshape(1).astype(jnp.int32)
    ys_buf = _gmm(xs_buf, moe_w1[0].astype(BF16), moe_w3[0].astype(BF16), moe_w2[0].astype(BF16),
                  tile_exp, n_valid)

    out_p = _combine(ys_buf, y3_p, _tile_slots(slots_p, TM_COMBINE), meta_p,
                     pmod(1, 5, TM_COMBINE), ln_g[1], ln_b[1], TM_COMBINE, alpha)
    out_s = _combine(ys_buf, y3_s, _tile_slots(slots_s, db), meta_s,
                     smod(1, 5), ln_g[1], ln_b[1], db, alpha)

    nh = CONV_WIDTH - 1
    new_conv_p = tail[:, HIST_PAD - nh:, :][None]
    new_conv_s = jnp.transpose(jnp.concatenate([hist_t[1:], u_s[None]], axis=0), (1, 0, 2))[None]
    hshape = (N_HEADS, HEAD_DIM)

    def seq_major(t):
        return jnp.transpose(t.reshape(1, b, *hshape, s), (0, 1, 4, 2, 3))

    return (out_p.reshape(b, s, d), out_s.reshape(db, 1, d), seq_major(kt_p), seq_major(vt_p),
            ks.reshape(1, db, 1, *hshape), vs.reshape(1, db, 1, *hshape),
            new_conv_p, new_conv_s)
```

```python
import functools
import math

import jax
import jax.numpy as jnp
from jax import lax
from jax.experimental import pallas as pl
from jax.experimental.pallas import tpu as pltpu

F32 = jnp.float32
BF16 = jnp.bfloat16
HIGHEST = lax.Precision.HIGHEST
NEG_INF = float("-inf")
LOG2E = math.log2(math.e)

SUBLANES = 8
LANES = 128

D_MODEL = 1024
N_HEADS = 16
HEAD_DIM = D_MODEL // N_HEADS
HEADS_PER_STEP = 8
MOBA_BLOCK = 256
MOBA_TOPK = 3
PAGE_SIZE = 128
CONV_WIDTH = 31
HIST_PAD = 32
N_EXPERTS = 8
MOE_TOPK = 2
LN_EPS = 1e-5
ROW_TILES = D_MODEL // LANES

TM_QKV = 512
TM_POST = 512
TM_CONV = 512
TM_ROUTE = 512
TM_GMM = 512
TM_COMBINE = 256
TF_GMM = 512
FFN_CHUNK = 256
ZERO_ROWS = 64
VMEM_LIMIT = 56 * 1024 * 1024

META_EXPERT, META_GATE, META_RANK = 0, 2, 4

NT_DIMS = (((1,), (1,)), ((), ()))


def _dot(a, b, precision=None):
    return jnp.dot(a, b, preferred_element_type=F32, precision=precision)


def _dot_nt(a, b, precision=None):
    return lax.dot_general(a, b, NT_DIMS, preferred_element_type=F32, precision=precision)


def _silu(x):
    return x * jax.nn.sigmoid(x)


def _layernorm(z, g, b):
    mu = jnp.mean(z, axis=-1, keepdims=True)
    zc = z - mu
    var = jnp.mean(zc * zc, axis=-1, keepdims=True)
    return zc * lax.rsqrt(var + LN_EPS) * g + b


def _resident(shape):
    nd = len(shape)
    return pl.BlockSpec(shape, lambda *_: (0,) * nd, pipeline_mode=pl.Buffered(1))


def _params(n_grid_dims):
    return pltpu.CompilerParams(
        dimension_semantics=("arbitrary",) * n_grid_dims, vmem_limit_bytes=VMEM_LIMIT)


def _rows_to_tiles(ref, x):
    t = x.shape[0]
    for s in range(ROW_TILES):
        ref[pl.ds(s, t, stride=ROW_TILES), :] = x[:, s * LANES:(s + 1) * LANES]


def _tiles_to_rows(ref, t):
    return jnp.concatenate([ref[pl.ds(s, t, stride=ROW_TILES), :] for s in range(ROW_TILES)], axis=1)


def _ada_kernel(c_ref, w_ref, b_ref, o_ref):
    c = c_ref[...]
    o_ref[...] = _dot(_silu(c).astype(BF16), w_ref[...].astype(BF16)) + b_ref[...]


def _adaln(c_all, ada_w, ada_b):
    depth = ada_w.shape[0]
    n = c_all.shape[0]
    d = D_MODEL
    return pl.pallas_call(
        _ada_kernel,
        grid=(depth, 6),
        in_specs=[
            pl.BlockSpec((n, d), lambda l, k: (0, 0)),
            pl.BlockSpec((None, d, d), lambda l, k: (l, 0, k)),
            pl.BlockSpec((None, None, 1, d), lambda l, k: (l, k, 0, 0)),
        ],
        out_specs=pl.BlockSpec((None, None, n, d), lambda l, k: (l, k, 0, 0)),
        out_shape=jax.ShapeDtypeStruct((depth, 6, n, d), F32),
        compiler_params=_params(2),
        name="adaln",
    )(c_all, ada_w, ada_b.reshape(depth, 6, 1, d))


class _Mods:
    def __init__(self, mods, n_sample):
        self.n_sample = n_sample
        self.m3 = mods
        self.m4 = mods.reshape(mods.shape[0], mods.shape[1], 1, D_MODEL)

    def prompt(self, lk, seq_of_step):
        ns = self.n_sample
        return self.m4, pl.BlockSpec(
            (None, None, 1, D_MODEL), lambda *g: (lk, ns + seq_of_step(*g), 0, 0))

    def sample(self, lk, rows, block_of_step):
        return self.m3, pl.BlockSpec((None, rows, D_MODEL), lambda *g: (lk, block_of_step(*g), 0))


def _qkv_kernel(x_ref, sh_ref, sc_ref, w_ref, q_ref, k_ref, v_ref):
    d = D_MODEL
    h = (x_ref[...] * (1.0 + sc_ref[...]) + sh_ref[...]).astype(BF16)
    q_ref[...] = _dot(h, w_ref[:, 0:d])
    k_ref[...] = _dot(h, w_ref[:, d:2 * d])
    v_ref[...] = _dot(h, w_ref[:, 2 * d:3 * d])


def _qkv(x, sh, sc, w_bf16, tm):
    t = x.shape[0]
    d = D_MODEL
    row = pl.BlockSpec((tm, d), lambda i: (i, 0))
    out = jax.ShapeDtypeStruct((t, d), F32)
    return pl.pallas_call(
        _qkv_kernel,
        grid=(t // tm,),
        in_specs=[row, sh[1], sc[1], _resident((d, 3 * d))],
        out_specs=[row, row, row],
        out_shape=[out, out, out],
        compiler_params=_params(1),
        name="qkv",
    )(x, sh[0], sc[0], w_bf16)


def _select_blocks(gate_t, n_valid):
    nb = gate_t.shape[0]
    blk = lax.broadcasted_iota(jnp.int32, gate_t.shape, 0)
    valid = blk < n_valid
    rows = []
    for n in range(nb):
        gn = gate_t[n:n + 1, :]
        beats = ((gate_t > gn) | ((gate_t == gn) & (blk < n))) & valid
        rows.append(jnp.sum(beats.astype(F32), axis=0, keepdims=True))
    cnt = jnp.concatenate(rows, axis=0)
    return (valid & (cnt < MOBA_TOPK)).astype(F32)


def _attn_kernel(q_ref, k_ref, v_ref, bias_ref, o_ref, kt_ref, vt_ref, kb_ref, ve_ref, km_ref, mb_ref, acc_ref):
    blk = MOBA_BLOCK
    s_len = k_ref.shape[0]
    nb = s_len // blk
    nh = HEADS_PER_STEP
    cb = pl.program_id(1)
    own = pl.program_id(2)

    @pl.when(own == 0)
    def _():
        kb_ref[...] = k_ref[...].astype(BF16)
        lane_s = lax.broadcasted_iota(jnp.int32, (s_len, LANES), 1)
        lane_k = lax.broadcasted_iota(jnp.int32, (nb, LANES), 1) // HEAD_DIM
        for pp in range(nh // 2):
            cols = slice(pp * LANES, (pp + 1) * LANES)
            kt_ref[cols, :] = jnp.transpose(k_ref[:, cols])
            vt_ref[cols, :] = jnp.transpose(v_ref[:, cols])
            rows = [jnp.mean(k_ref[n * blk:(n + 1) * blk, cols], axis=0, keepdims=True) for n in range(nb)]
            km = jnp.concatenate(rows, axis=0)
            km_ref[pp] = jnp.concatenate([jnp.where(lane_k == 0, km, 0.0), jnp.where(lane_k == 1, km, 0.0)], axis=0)
            v = v_ref[:, cols]
            ve_ref[2 * pp] = jnp.where(lane_s < HEAD_DIM, v, jnp.where(lane_s == HEAD_DIM, 1.0, 0.0)).astype(BF16)
            ve_ref[2 * pp + 1] = jnp.where(lane_s >= HEAD_DIM, v, jnp.where(lane_s == 0, 1.0, 0.0)).astype(BF16)

    qi = lax.broadcasted_iota(jnp.int32, (blk, blk), 0)
    ki = lax.broadcasted_iota(jnp.int32, (blk, blk), 1)
    causal = qi >= ki
    eye = (qi == ki).astype(BF16)
    lane = lax.broadcasted_iota(jnp.int32, (blk, LANES), 1)
    lane_head = lane // HEAD_DIM
    blk_off = (own - lane % nb).astype(F32) * float(blk)
    own_start = pl.multiple_of(own * blk, blk)

    qh = []
    for pp in range(nh // 2):
        q = q_ref[:, pp * LANES:(pp + 1) * LANES]
        qs_all = (q * (HEAD_DIM ** -0.5 * LOG2E)).astype(BF16)
        gates = _dot_nt(km_ref[pp], q, precision=HIGHEST)
        sel_t = jnp.concatenate([_select_blocks(gates[hh * nb:(hh + 1) * nb, :], own) for hh in range(2)]
                                + [jnp.zeros((LANES - 2 * nb, blk), F32)], axis=0).astype(BF16)
        sel = _dot_nt(eye, sel_t)
        head_no = (cb * nh + 2 * pp + 1 + (lane >= nb).astype(jnp.int32)).astype(F32)
        slope2 = jnp.exp2(head_no * (-8.0 / N_HEADS)) * LOG2E
        mb_ref[pp] = jnp.where(sel > 0.5, -slope2 * blk_off, NEG_INF)
        for hh in range(2):
            qh.append(jnp.where(lane_head == hh, qs_all, jnp.zeros_like(qs_all)))

    ms = []
    for h in range(nh):
        pp = h // 2
        kd = kb_ref[pl.ds(own_start, blk), pp * LANES:(pp + 1) * LANES]
        s = jnp.where(causal, _dot_nt(qh[h], kd) + bias_ref[h], NEG_INF)
        m = jnp.max(s, axis=1, keepdims=True)
        p = jnp.exp2(s - m)
        acc_ref[h] = _dot(p.astype(BF16), ve_ref[h, pl.ds(own_start, blk), :])
        ms.append(m)

    def body(j, carry):
        start = pl.multiple_of(j * blk, blk)
        out = []
        for h in range(nh):
            pp, hh = h // 2, h % 2
            kj = kb_ref[pl.ds(start, blk), pp * LANES:(pp + 1) * LANES]
            m = carry[h]
            s = _dot_nt(qh[h], kj) + bias_ref[h]
            mb = jnp.sum(jnp.where(lane == j + hh * nb, mb_ref[pp], 0.0), axis=1, keepdims=True)
            m_new = jnp.maximum(m, jnp.max(s, axis=1, keepdims=True) + mb)
            p = jnp.exp2(s + (mb - m_new))
            a = jnp.exp2(m - m_new)
            acc_ref[h] = a * acc_ref[h] + _dot(p.astype(BF16), ve_ref[h, pl.ds(start, blk), :])
            out.append(m_new)
        return tuple(out)

    lax.fori_loop(0, own, body, tuple(ms))
    for pp in range(nh // 2):
        acc0 = acc_ref[2 * pp]
        acc1 = acc_ref[2 * pp + 1]
        out = jnp.where(lane < HEAD_DIM, acc0 / acc0[:, HEAD_DIM:HEAD_DIM + 1], acc1 / acc1[:, 0:1])
        o_ref[:, pp * LANES:(pp + 1) * LANES] = out.astype(o_ref.dtype)


def _alibi_bias():
    slopes = 2.0 ** (-8.0 * jnp.arange(1, N_HEADS + 1, dtype=F32) / N_HEADS)
    qi = jnp.arange(MOBA_BLOCK, dtype=F32)[:, None]
    ki = jnp.arange(MOBA_BLOCK, dtype=F32)[None, :]
    return -(slopes * LOG2E)[:, None, None] * (qi - ki)[None]


def _moba_prompt(q, k, v):
    b, s, d = q.shape
    nb = s // MOBA_BLOCK
    nh = HEADS_PER_STEP
    w = nh * HEAD_DIM
    qspec = pl.BlockSpec((None, MOBA_BLOCK, w), lambda bi, cb, qb: (bi, qb, cb))
    kvspec = pl.BlockSpec((None, s, w), lambda bi, cb, qb: (bi, 0, cb))
    bspec = pl.BlockSpec((nh, MOBA_BLOCK, MOBA_BLOCK), lambda bi, cb, qb: (cb, 0, 0))
    tspec = pl.BlockSpec((None, w, s), lambda bi, cb, qb: (bi, cb, 0))
    return pl.pallas_call(
        _attn_kernel,
        grid=(b, d // w, nb),
        in_specs=[qspec, kvspec, kvspec, bspec],
        out_specs=[qspec, tspec, tspec],
        out_shape=[jax.ShapeDtypeStruct((b, s, d), BF16), jax.ShapeDtypeStruct((b, d, s), F32),
                   jax.ShapeDtypeStruct((b, d, s), F32)],
        scratch_shapes=[pltpu.VMEM((s, w), BF16), pltpu.VMEM((nh, s, LANES), BF16),
                        pltpu.VMEM((nh // 2, 2 * nb, LANES), F32),
                        pltpu.VMEM((nh // 2, MOBA_BLOCK, LANES), F32),
                        pltpu.VMEM((nh, MOBA_BLOCK, LANES), F32)],
        compiler_params=_params(3),
        name="moba_prompt",
    )(q, k, v, _alibi_bias())


def _attn_sample_probs_kernel(n_pages, pt_ref, q_ref, qt_ref, kn_ref, *refs):
    del pt_ref
    kp = refs[:n_pages]
    p_ref, misc_ref, s_ref = refs[n_pages:]
    past = n_pages * PAGE_SIZE
    nb = past // MOBA_BLOCK
    scale = HEAD_DIM ** -0.5

    q = q_ref[...]
    qt = qt_ref[...]
    for h in range(N_HEADS):
        qb = jnp.broadcast_to(qt[:, h:h + 1], (HEAD_DIM, PAGE_SIZE))
        for p in range(n_pages):
            s_ref[h:h + 1, p * PAGE_SIZE:(p + 1) * PAGE_SIZE] = jnp.sum(kp[p][h] * qb, axis=0, keepdims=True)
    raw = s_ref[...]
    gates = [jnp.sum(raw[:, n * MOBA_BLOCK:(n + 1) * MOBA_BLOCK], axis=1, keepdims=True) * (1.0 / MOBA_BLOCK)
             for n in range(nb)]
    lane_blk = lax.broadcasted_iota(jnp.int32, (N_HEADS, past), 1) // MOBA_BLOCK
    keep = jnp.zeros((N_HEADS, past), jnp.bool_)
    n_kept = jnp.zeros((N_HEADS, 1), F32)
    kept_ids = [jnp.zeros((N_HEADS, 1), F32) for _ in range(MOBA_TOPK)]
    for n in range(nb):
        cnt = jnp.zeros((N_HEADS, 1), F32)
        for j in range(nb):
            if j != n:
                beats = (gates[j] > gates[n]) | ((gates[j] == gates[n]) & (j < n))
                cnt = cnt + beats.astype(F32)
        keep_n = cnt < MOBA_TOPK
        keep = keep | ((lane_blk == n) & keep_n)
        for j in range(MOBA_TOPK):
            kept_ids[j] = jnp.where(keep_n & (n_kept == j), float(n), kept_ids[j])
        n_kept = n_kept + keep_n.astype(F32)
    head = lax.broadcasted_iota(jnp.int32, (N_HEADS, 1), 0)
    slope = jnp.exp2((head + 1).astype(F32) * (-8.0 / N_HEADS))
    kpos = lax.broadcasted_iota(jnp.int32, (1, past), 1).astype(F32)
    s = jnp.where(keep, raw * scale - slope * (float(past) - kpos), NEG_INF)
    s_own = jnp.sum(q * kn_ref[...], axis=1, keepdims=True) * scale
    m = jnp.maximum(jnp.max(s, axis=1, keepdims=True), s_own)
    e = jnp.exp(s - m)
    w_own = jnp.exp(s_own - m)
    inv = 1.0 / (jnp.sum(e, axis=1, keepdims=True) + w_own)
    p = e * inv
    for pg in range(n_pages):
        p_ref[:, pg, :] = p[:, pg * PAGE_SIZE:(pg + 1) * PAGE_SIZE]
    lane = lax.broadcasted_iota(jnp.int32, (N_HEADS, LANES), 1)
    misc = jnp.where(lane == MOBA_TOPK, w_own * inv, 0.0)
    for j in range(MOBA_TOPK):
        misc = jnp.where(lane == j, kept_ids[j], misc)
    misc_ref[...] = misc


def _attn_sample_values_kernel(n_pages, layer, pt_ref, kept_ref, p_ref, misc_ref, vnt_ref, cv_ref, o_ref,
                               vbuf_ref, sem):
    ppb = MOBA_BLOCK // PAGE_SIZE
    per_head = MOBA_TOPK * ppb
    n_chunks = N_HEADS * per_head
    i = pl.program_id(0)
    cur = i % 2

    def chunk_copy(seq, c, slot):
        h, j, pg = c // per_head, (c % per_head) // ppb, c % ppb
        page = pt_ref[seq * n_pages + kept_ref[(seq * N_HEADS + h) * MOBA_TOPK + j] * ppb + pg]
        return pltpu.make_async_copy(cv_ref.at[layer, page, h], vbuf_ref.at[slot, c], sem.at[slot])

    def fetch(seq, slot):
        for c in range(n_chunks):
            chunk_copy(seq, c, slot).start()

    @pl.when(i == 0)
    def _():
        fetch(0, 0)

    for slot in range(2):
        @pl.when((i + 1 < pl.num_programs(0)) & (cur != slot))
        def _(slot=slot):
            fetch(i + 1, slot)

    for slot in range(2):
        @pl.when(cur == slot)
        def _(slot=slot):
            for c in range(n_chunks):
                chunk_copy(i, c, slot).wait()

    misc = misc_ref[...]
    lane_h = lax.broadcasted_iota(jnp.int32, (HEAD_DIM, N_HEADS), 1)
    out_t = jnp.zeros((HEAD_DIM, N_HEADS), F32)
    for h in range(N_HEADS):
        acc = jnp.zeros((HEAD_DIM, PAGE_SIZE), F32)
        for j in range(MOBA_TOPK):
            blk = kept_ref[(i * N_HEADS + h) * MOBA_TOPK + j]
            for pg in range(ppb):
                w = p_ref[h, pl.ds(blk * ppb + pg, 1), :]
                acc = acc + vbuf_ref[cur, h * per_head + j * ppb + pg] * w
        w_own = misc[h:h + 1, MOBA_TOPK:MOBA_TOPK + 1]
        col = jnp.sum(acc, axis=1, keepdims=True) + w_own * vnt_ref[:, h:h + 1]
        out_t = jnp.where(lane_h == h, col, out_t)
    o_ref[...] = out_t


def _moba_sample(q, k_new, v_new, cache_k, cache_v, layer, page_table):
    n_seq, n_pages = page_table.shape
    assert n_pages * PAGE_SIZE // MOBA_BLOCK >= MOBA_TOPK
    hd = (N_HEADS, HEAD_DIM)
    dh = (HEAD_DIM, N_HEADS)
    ck = jnp.transpose(cache_k, (0, 1, 3, 4, 2))
    cv = jnp.transpose(cache_v, (0, 1, 3, 4, 2))
    pt = page_table.reshape(-1).astype(jnp.int32)
    q3 = q.reshape((n_seq,) + hd)
    hd_spec = pl.BlockSpec((None,) + hd, lambda i, *_: (i, 0, 0))
    dh_spec = pl.BlockSpec((None,) + dh, lambda i, *_: (i, 0, 0))
    p_spec = pl.BlockSpec((None, N_HEADS, n_pages, PAGE_SIZE), lambda i, *_: (i, 0, 0, 0))
    misc_spec = pl.BlockSpec((None, N_HEADS, LANES), lambda i, *_: (i, 0, 0))

    def page_spec(p):
        return pl.BlockSpec((None, None) + hd + (PAGE_SIZE,),
                            lambda i, pt: (layer, pt[i * n_pages + p], 0, 0, 0))

    probs, misc = pl.pallas_call(
        functools.partial(_attn_sample_probs_kernel, n_pages),
        grid_spec=pltpu.PrefetchScalarGridSpec(
            num_scalar_prefetch=1,
            grid=(n_seq,),
            in_specs=[hd_spec, dh_spec, hd_spec] + [page_spec(p) for p in range(n_pages)],
            out_specs=[p_spec, misc_spec],
            scratch_shapes=[pltpu.VMEM((N_HEADS, n_pages * PAGE_SIZE), F32)],
        ),
        out_shape=[jax.ShapeDtypeStruct((n_seq, N_HEADS, n_pages, PAGE_SIZE), F32),
                   jax.ShapeDtypeStruct((n_seq, N_HEADS, LANES), F32)],
        compiler_params=_params(1),
        name="moba_sample_probs",
    )(pt, q3, q3.transpose(0, 2, 1), k_new.reshape((n_seq,) + hd), *([ck] * n_pages))

    kept = misc[:, :, 0:MOBA_TOPK].astype(jnp.int32).reshape(-1)
    n_chunks = N_HEADS * MOBA_TOPK * (MOBA_BLOCK // PAGE_SIZE)
    out_t = pl.pallas_call(
        functools.partial(_attn_sample_values_kernel, n_pages, layer),
        grid_spec=pltpu.PrefetchScalarGridSpec(
            num_scalar_prefetch=2,
            grid=(n_seq,),
            in_specs=[p_spec, misc_spec, dh_spec, pl.BlockSpec(memory_space=pl.ANY)],
            out_specs=dh_spec,
            scratch_shapes=[pltpu.VMEM((2, n_chunks, HEAD_DIM, PAGE_SIZE), F32), pltpu.SemaphoreType.DMA((2,))],
        ),
        out_shape=jax.ShapeDtypeStruct((n_seq,) + dh, F32),
        compiler_params=_params(1),
        name="moba_sample_values",
    )(pt, kept, probs, misc, v_new.reshape((n_seq,) + hd).transpose(0, 2, 1), cv)
    return out_t.transpose(0, 2, 1).reshape(n_seq, D_MODEL).astype(BF16)


def _post_kernel(alpha, a_ref, x_ref, gm_ref, shf_ref, scf_ref, gf_ref, lng_ref, lnb_ref,
                 wo_ref, w1_ref, w3_ref, w2_ref, o_ref, g_ref):
    m = _dot(a_ref[...], wo_ref[...])
    y1 = _layernorm(alpha * x_ref[...] + (1.0 + gm_ref[...]) * m, lng_ref[0:1, :], lnb_ref[0:1, :])
    h = (y1 * (1.0 + scf_ref[...]) + shf_ref[...]).astype(BF16)
    _swiglu_hidden(h, w1_ref, w3_ref, g_ref, FFN_CHUNK)
    f = _dot(g_ref[...], w2_ref[...])
    o_ref[...] = _layernorm(alpha * y1 + (1.0 + gf_ref[...]) * f, lng_ref[1:2, :], lnb_ref[1:2, :])


def _swiglu_hidden(h, w1_ref, w3_ref, g_ref, chunk):
    for c in range(w1_ref.shape[1] // chunk):
        cols = slice(c * chunk, (c + 1) * chunk)
        t1 = _dot(h, w1_ref[:, cols])
        t3 = _dot(h, w3_ref[:, cols])
        g_ref[:, cols] = (_silu(t1) * t3).astype(BF16)


def _post_attn(attn, x, gm, shf, scf, gf, ln_g, ln_b, wo, w1, w3, w2, tm, alpha):
    t = x.shape[0]
    d = D_MODEL
    row = pl.BlockSpec((tm, d), lambda i: (i, 0))
    return pl.pallas_call(
        functools.partial(_post_kernel, alpha),
        grid=(t // tm,),
        in_specs=[row, row, gm[1], shf[1], scf[1], gf[1], _resident((2, d)), _resident((2, d)),
                  _resident(wo.shape), _resident(w1.shape), _resident(w3.shape), _resident(w2.shape)],
        out_specs=row,
        out_shape=jax.ShapeDtypeStruct((t, d), F32),
        scratch_shapes=[pltpu.VMEM((tm, w1.shape[1]), BF16)],
        compiler_params=_params(1),
        name="post_attn_ffn",
    )(attn, x, gm[0], shf[0], scf[0], gf[0], ln_g, ln_b, wo, w1, w3, w2)


def _glu(y, shm_ref, scm_ref, wpw1_ref, bpw1_ref):
    d = D_MODEL
    h = (y * (1.0 + scm_ref[...]) + shm_ref[...]).astype(BF16)
    a = _dot(h, wpw1_ref[:, 0:d]) + bpw1_ref[:, 0:d]
    g = _dot(h, wpw1_ref[:, d:2 * d]) + bpw1_ref[:, d:2 * d]
    return a * jax.nn.sigmoid(g)


def _conv_tail(alpha, y, conv, gm_ref, shf_ref, scf_ref, lng_ref, lnb_ref, clg_ref, clb_ref,
               wpw2_ref, bpw2_ref, wr_ref, br_ref, y3_ref, h2_ref, lg_ref):
    z = _silu(_layernorm(conv, clg_ref[...], clb_ref[...])).astype(BF16)
    m = _dot(z, wpw2_ref[...]) + bpw2_ref[...]
    y3 = _layernorm(alpha * y + (1.0 + gm_ref[...]) * m, lng_ref[0:1, :], lnb_ref[0:1, :])
    y3_ref[...] = y3
    h2 = y3 * (1.0 + scf_ref[...]) + shf_ref[...]
    _rows_to_tiles(h2_ref, h2)
    h2_hi = h2.astype(BF16)
    h2_lo = (h2 - h2_hi.astype(F32)).astype(BF16)
    lg_ref[...] = (_dot(h2_hi, wr_ref[0]) + _dot(h2_lo, wr_ref[0]) + _dot(h2_hi, wr_ref[1])) + br_ref[...]


def _conv_prompt_kernel(alpha, y_ref, shm_ref, scm_ref, gm_ref, shf_ref, scf_ref, lng_ref, lnb_ref,
                        wpw1_ref, bpw1_ref, wdw_ref, bdw_ref, clg_ref, clb_ref, wpw2_ref, bpw2_ref,
                        wr_ref, br_ref, y3_ref, h2_ref, lg_ref, tail_ref, ext_ref, shift_ref, conv_ref):
    tm = y_ref.shape[0]
    d = D_MODEL
    rows_ext = tm + HIST_PAD

    @pl.when(pl.program_id(1) == 0)
    def _():
        ext_ref[0:HIST_PAD, :] = jnp.zeros((HIST_PAD, d), F32)

    y = y_ref[...]
    u = _glu(y, shm_ref, scm_ref, wpw1_ref, bpw1_ref)
    ext_ref[HIST_PAD:rows_ext, :] = u
    tail_ref[...] = u[tm - HIST_PAD:tm, :]
    span = rows_ext - SUBLANES
    for b in range(1, SUBLANES):
        shift_ref[b - 1, :, :] = ext_ref[b:b + span, :]
    base = HIST_PAD - (CONV_WIDTH - 1)
    rb = 4 * SUBLANES

    def body(r, carry):
        r0 = pl.multiple_of(r * rb, rb)
        acc = jnp.broadcast_to(bdw_ref[...], (rb, d))
        for k in range(CONV_WIDTH):
            off = base + k
            start = r0 + (off // SUBLANES) * SUBLANES
            if off % SUBLANES == 0:
                x = ext_ref[pl.ds(start, rb), :]
            else:
                x = shift_ref[off % SUBLANES - 1, pl.ds(start, rb), :]
            acc = acc + wdw_ref[k:k + 1, :] * x
        conv_ref[pl.ds(r0, rb), :] = acc
        return carry

    lax.fori_loop(0, tm // rb, body, 0)
    ext_ref[0:HIST_PAD, :] = u[tm - HIST_PAD:tm, :]
    _conv_tail(alpha, y, conv_ref[...], gm_ref, shf_ref, scf_ref, lng_ref, lnb_ref, clg_ref, clb_ref,
               wpw2_ref, bpw2_ref, wr_ref, br_ref, y3_ref, h2_ref, lg_ref)


def _conv_sample_kernel(alpha, y_ref, shm_ref, scm_ref, gm_ref, shf_ref, scf_ref, lng_ref, lnb_ref,
                        wpw1_ref, bpw1_ref, wdw_ref, bdw_ref, clg_ref, clb_ref, wpw2_ref, bpw2_ref,
                        wr_ref, br_ref, hist_ref, y3_ref, h2_ref, lg_ref, u_ref):
    nh = CONV_WIDTH - 1
    y = y_ref[...]
    u = _glu(y, shm_ref, scm_ref, wpw1_ref, bpw1_ref)
    u_ref[...] = u
    w = wdw_ref[...]
    conv = bdw_ref[...] + w[nh:nh + 1, :] * u
    for k in range(nh):
        conv = conv + w[k:k + 1, :] * hist_ref[k]
    _conv_tail(alpha, y, conv, gm_ref, shf_ref, scf_ref, lng_ref, lnb_ref, clg_ref, clb_ref,
               wpw2_ref, bpw2_ref, wr_ref, br_ref, y3_ref, h2_ref, lg_ref)


def _conv_weight_specs(d):
    return [_resident((2, d)), _resident((2, d)), _resident((d, 2 * d)), _resident((1, 2 * d)),
            _resident((CONV_WIDTH, d)), _resident((1, d)), _resident((1, d)), _resident((1, d)),
            _resident((d, d)), _resident((1, d)), _resident((2, d, LANES)), _resident((1, LANES))]


def _conv_prompt(y, mods5, weights, n_seq, seq_len, alpha):
    d = D_MODEL
    tm = TM_CONV
    tps = seq_len // tm
    t = n_seq * seq_len
    row = pl.BlockSpec((tm, d), lambda b, i: (b * tps + i, 0))
    trow = pl.BlockSpec((tm * ROW_TILES, LANES), lambda b, i: (b * tps + i, 0))
    lgrow = pl.BlockSpec((tm, LANES), lambda b, i: (b * tps + i, 0))
    tail = pl.BlockSpec((None, HIST_PAD, d), lambda b, i: (b, 0, 0))
    return pl.pallas_call(
        functools.partial(_conv_prompt_kernel, alpha),
        grid=(n_seq, tps),
        in_specs=[row] + [m[1] for m in mods5] + _conv_weight_specs(d),
        out_specs=[row, trow, lgrow, tail],
        out_shape=[jax.ShapeDtypeStruct((t, d), F32),
                   jax.ShapeDtypeStruct((t * ROW_TILES, LANES), F32),
                   jax.ShapeDtypeStruct((t, LANES), F32),
                   jax.ShapeDtypeStruct((n_seq, HIST_PAD, d), F32)],
        scratch_shapes=[pltpu.VMEM((tm + HIST_PAD, d), F32),
                        pltpu.VMEM((SUBLANES - 1, tm + HIST_PAD - SUBLANES, d), F32),
                        pltpu.VMEM((tm, d), F32)],
        compiler_params=_params(2),
        name="conv_prompt",
    )(y, *[m[0] for m in mods5], *weights)


def _conv_sample(y, mods5, weights, hist, alpha):
    d = D_MODEL
    n = y.shape[0]
    tm = 32
    row = pl.BlockSpec((tm, d), lambda i: (i, 0))
    return pl.pallas_call(
        functools.partial(_conv_sample_kernel, alpha),
        grid=(n // tm,),
        in_specs=[row] + [m[1] for m in mods5] + _conv_weight_specs(d)
        + [pl.BlockSpec((CONV_WIDTH - 1, tm, d), lambda i: (0, i, 0))],
        out_specs=[row, pl.BlockSpec((tm * ROW_TILES, LANES), lambda i: (i, 0)),
                   pl.BlockSpec((tm, LANES), lambda i: (i, 0)), row],
        out_shape=[jax.ShapeDtypeStruct((n, d), F32),
                   jax.ShapeDtypeStruct((n * ROW_TILES, LANES), F32),
                   jax.ShapeDtypeStruct((n, LANES), F32),
                   jax.ShapeDtypeStruct((n, d), F32)],
        compiler_params=_params(1),
        name="conv_sample",
    )(y, *[m[0] for m in mods5], *weights, hist)


def _route_tile(lg_ref, meta_ref, meta_t_ref, carry_ref):
    tm = lg_ref.shape[0]
    lane = lax.broadcasted_iota(jnp.int32, (tm, LANES), 1)
    lg = jnp.where(lane < N_EXPERTS, lg_ref[...], NEG_INF)
    m1 = jnp.max(lg, axis=1, keepdims=True)
    i1 = jnp.min(jnp.where(lg == m1, lane, LANES), axis=1, keepdims=True)
    lg2 = jnp.where(lane == i1, NEG_INF, lg)
    m2 = jnp.max(lg2, axis=1, keepdims=True)
    i2 = jnp.min(jnp.where(lg2 == m2, lane, LANES), axis=1, keepdims=True)
    e = jnp.exp(m2 - m1)
    g1 = 1.0 / (1.0 + e)
    g2 = e / (1.0 + e)
    hot1 = lane == i1
    hot2 = lane == i2
    onehot = (hot1 | hot2).astype(BF16)
    ri = lax.broadcasted_iota(jnp.int32, (tm, tm), 0)
    ci = lax.broadcasted_iota(jnp.int32, (tm, tm), 1)
    before = (ci < ri).astype(BF16)
    rank = _dot(before, onehot) + carry_ref[...]
    r1 = jnp.sum(jnp.where(hot1, rank, 0.0), axis=1, keepdims=True)
    r2 = jnp.sum(jnp.where(hot2, rank, 0.0), axis=1, keepdims=True)
    cols = (i1.astype(F32), i2.astype(F32), g1, g2, r1, r2)
    meta = jnp.zeros((tm, LANES), F32)
    for c, val in enumerate(cols):
        meta = jnp.where(lane == c, val, meta)
    meta_ref[...] = meta
    meta_t_ref[...] = jnp.transpose(meta)[0:SUBLANES, :]
    carry_ref[...] += jnp.sum(onehot.astype(F32), axis=0, keepdims=True)


def _route_kernel(n_prompt_tiles, lgp_ref, lgs_ref, mp_ref, mpt_ref, ms_ref, mst_ref, cnt_ref, carry_ref):
    i = pl.program_id(0)

    @pl.when(i == 0)
    def _():
        carry_ref[...] = jnp.zeros_like(carry_ref)

    @pl.when(i < n_prompt_tiles)
    def _():
        _route_tile(lgp_ref, mp_ref, mpt_ref, carry_ref)

    @pl.when(i == n_prompt_tiles)
    def _():
        _route_tile(lgs_ref, ms_ref, mst_ref, carry_ref)
        cnt_ref[...] = carry_ref[...]


def _route(lg_p, lg_s):
    tp, ts = lg_p.shape[0], lg_s.shape[0]
    tm = TM_ROUTE
    npt = tp // tm
    last = npt - 1
    return pl.pallas_call(
        functools.partial(_route_kernel, npt),
        grid=(npt + 1,),
        in_specs=[pl.BlockSpec((tm, LANES), lambda i: (jnp.minimum(i, last), 0)),
                  pl.BlockSpec((ts, LANES), lambda i: (0, 0))],
        out_specs=[pl.BlockSpec((tm, LANES), lambda i: (jnp.minimum(i, last), 0)),
                   pl.BlockSpec((SUBLANES, tm), lambda i: (0, jnp.minimum(i, last))),
                   pl.BlockSpec((ts, LANES), lambda i: (0, 0)),
                   pl.BlockSpec((SUBLANES, ts), lambda i: (0, 0)),
                   pl.BlockSpec((1, LANES), lambda i: (0, 0))],
        out_shape=[jax.ShapeDtypeStruct((tp, LANES), F32),
                   jax.ShapeDtypeStruct((SUBLANES, tp), F32),
                   jax.ShapeDtypeStruct((ts, LANES), F32),
                   jax.ShapeDtypeStruct((SUBLANES, ts), F32),
                   jax.ShapeDtypeStruct((1, LANES), F32)],
        scratch_shapes=[pltpu.VMEM((1, LANES), F32)],
        compiler_params=_params(1),
        name="moe_route",
    )(lg_p, lg_s)


def _row_copy(src_ref, src_row, dst_ref, dst_row, sem, n=1):
    return pltpu.make_async_copy(
        src_ref.at[pl.ds(pl.multiple_of(src_row * ROW_TILES, ROW_TILES), n * ROW_TILES), :],
        dst_ref.at[pl.ds(pl.multiple_of(dst_row * ROW_TILES, ROW_TILES), n * ROW_TILES), :], sem)


def _scatter_rows(slots_ref, h_ref, xs_ref, sem):
    tm = h_ref.shape[0] // ROW_TILES

    def issue(r, carry):
        for c in range(MOE_TOPK):
            _row_copy(h_ref, r, xs_ref, slots_ref[0, c * tm + r], sem).start()
        return carry

    lax.fori_loop(0, tm, issue, 0)
    for c in range(MOE_TOPK):
        _row_copy(h_ref, 0, xs_ref, 0, sem, n=tm).wait()


def _scatter_kernel(n_prompt_tiles, zlo_ref, zhi_ref, sp_ref, ss_ref, hp_ref, hs_ref, xs_ref,
                    zero_ref, sem, zsem):
    i = pl.program_id(0)

    @pl.when(i < n_prompt_tiles)
    def _():
        _scatter_rows(sp_ref, hp_ref, xs_ref, sem)

    @pl.when(i == n_prompt_tiles)
    def _():
        _scatter_rows(ss_ref, hs_ref, xs_ref, sem)
        zero_ref[...] = jnp.zeros_like(zero_ref)
        for e in range(N_EXPERTS):
            lo = zlo_ref[e]
            n = zhi_ref[e] - lo
            nbig = n // ZERO_ROWS

            def big(k, carry, lo=lo):
                return _row_copy(zero_ref, 0, xs_ref, lo + k * ZERO_ROWS, zsem, n=ZERO_ROWS)

            def small(r, carry, lo=lo):
                return _row_copy(zero_ref, 0, xs_ref, lo + r, zsem)

            lax.fori_loop(0, nbig, lambda k, c: (big(k, c).start(), c)[1], 0)
            lax.fori_loop(nbig * ZERO_ROWS, n, lambda r, c: (small(r, c).start(), c)[1], 0)
            lax.fori_loop(0, nbig, lambda k, c: (big(k, c).wait(), c)[1], 0)
            lax.fori_loop(nbig * ZERO_ROWS, n, lambda r, c: (small(r, c).wait(), c)[1], 0)


def _scatter(h2t_p, h2t_s, slots_p, slots_s, zero_lo, zero_hi, n_rows):
    tp = h2t_p.shape[0] // ROW_TILES
    ts = h2t_s.shape[0] // ROW_TILES
    tm = TM_ROUTE
    npt = tp // tm
    last = npt - 1
    grid_spec = pltpu.PrefetchScalarGridSpec(
        num_scalar_prefetch=2,
        grid=(npt + 1,),
        in_specs=[pl.BlockSpec((None, 1, MOE_TOPK * tm), lambda i, lo, hi: (jnp.minimum(i, last), 0, 0),
                               memory_space=pltpu.SMEM),
                  pl.BlockSpec((None, 1, MOE_TOPK * ts), lambda i, lo, hi: (0, 0, 0), memory_space=pltpu.SMEM),
                  pl.BlockSpec((tm * ROW_TILES, LANES), lambda i, lo, hi: (jnp.minimum(i, last), 0)),
                  pl.BlockSpec((ts * ROW_TILES, LANES), lambda i, lo, hi: (0, 0))],
        out_specs=pl.BlockSpec(memory_space=pl.ANY),
        scratch_shapes=[pltpu.VMEM((ZERO_ROWS * ROW_TILES, LANES), F32), pltpu.SemaphoreType.DMA(()),
                        pltpu.SemaphoreType.DMA(())],
    )
    return pl.pallas_call(
        functools.partial(_scatter_kernel, npt),
        grid_spec=grid_spec,
        out_shape=jax.ShapeDtypeStruct((n_rows * ROW_TILES, LANES), F32),
        compiler_params=pltpu.CompilerParams(dimension_semantics=("arbitrary",),
                                             vmem_limit_bytes=VMEM_LIMIT, has_side_effects=True),
        name="moe_scatter",
    )(zero_lo, zero_hi, slots_p, slots_s, h2t_p, h2t_s)


def _load_expert(e, w1_hbm, w3_hbm, w2_hbm, w1_ref, w3_ref, w2_ref, up_stage, down_stage, sem):
    tf = TF_GMM
    nck = w1_ref.shape[1] // tf
    jobs = []
    for k in range(nck):
        cols = slice(k * tf, (k + 1) * tf)
        for src, dst in ((w1_hbm, w1_ref), (w3_hbm, w3_ref)):
            slot = len(jobs) % 2
            jobs.append((pltpu.make_async_copy(src.at[e, :, cols], up_stage.at[slot], sem.at[slot]),
                         up_stage.at[slot], dst.at[:, cols]))
    for k in range(nck):
        rows = slice(k * tf, (k + 1) * tf)
        slot = len(jobs) % 2
        jobs.append((pltpu.make_async_copy(w2_hbm.at[e, rows, :], down_stage.at[slot], sem.at[slot]),
                     down_stage.at[slot], w2_ref.at[rows, :]))
    jobs[0][0].start()
    for k, (copy, stage, dst) in enumerate(jobs):
        if k + 1 < len(jobs):
            jobs[k + 1][0].start()
        copy.wait()
        dst[...] = stage[...].astype(BF16)


def _gmm_kernel(exp_ref, nv_ref, x_ref, w1_hbm, w3_hbm, w2_hbm, o_ref, g_ref, w1_ref, w3_ref, w2_ref,
                up_stage, down_stage, sem):
    tg = g_ref.shape[0]
    m = pl.program_id(0)
    live = m < nv_ref[0]
    e = exp_ref[m]

    @pl.when(live & ((m == 0) | (e != exp_ref[jnp.maximum(m - 1, 0)])))
    def _():
        _load_expert(e, w1_hbm, w3_hbm, w2_hbm, w1_ref, w3_ref, w2_ref, up_stage, down_stage, sem)

    @pl.when(live)
    def _():
        x = _tiles_to_rows(x_ref, tg).astype(BF16)
        _swiglu_hidden(x, w1_ref, w3_ref, g_ref, TF_GMM)
        _rows_to_tiles(o_ref, _dot(g_ref[...], w2_ref[...]))

    @pl.when(jnp.logical_not(live))
    def _():
        o_ref[...] = jnp.zeros_like(o_ref)


def _gmm(xs, w1, w3, w2, tile_exp, n_valid):
    d = D_MODEL
    tg = TM_GMM
    tf = TF_GMM
    dfe = w1.shape[2]
    n_tiles = tile_exp.shape[0]
    hbm = pl.BlockSpec(memory_space=pl.ANY)
    grid_spec = pltpu.PrefetchScalarGridSpec(
        num_scalar_prefetch=2,
        grid=(n_tiles,),
        in_specs=[pl.BlockSpec((tg * ROW_TILES, LANES), lambda m, ex, nv: (m, 0)), hbm, hbm, hbm],
        out_specs=pl.BlockSpec((tg * ROW_TILES, LANES), lambda m, ex, nv: (m, 0)),
        scratch_shapes=[pltpu.VMEM((tg, dfe), BF16), pltpu.VMEM((d, dfe), BF16), pltpu.VMEM((d, dfe), BF16),
                        pltpu.VMEM((dfe, d), BF16), pltpu.VMEM((2, d, tf), F32), pltpu.VMEM((2, tf, d), F32),
                        pltpu.SemaphoreType.DMA((2,))],
    )
    return pl.pallas_call(
        _gmm_kernel,
        grid_spec=grid_spec,
        out_shape=jax.ShapeDtypeStruct(xs.shape, F32),
        compiler_params=_params(1),
        name="moe_experts",
    )(tile_exp, n_valid, xs, w1, w3, w2)


def _combine_kernel(alpha, slots_ref, next_slots_ref, ys_ref, y_ref, meta_ref, gf_ref, lng_ref, lnb_ref,
                    o_ref, buf_ref, sem):
    tm = y_ref.shape[0]
    i = pl.program_id(0)
    cur = i % 2

    def gather(sl_ref, slot):
        def issue(r, carry):
            for c in range(MOE_TOPK):
                _row_copy(ys_ref, sl_ref[0, c * tm + r], buf_ref.at[slot, c], r, sem.at[slot]).start()
            return carry

        lax.fori_loop(0, tm, issue, 0)

    @pl.when(i == 0)
    def _():
        gather(slots_ref, 0)

    for slot in range(2):
        @pl.when((i + 1 < pl.num_programs(0)) & (cur != slot))
        def _(slot=slot):
            gather(next_slots_ref, slot)

    for c in range(MOE_TOPK):
        _row_copy(ys_ref, 0, buf_ref.at[cur, c], 0, sem.at[cur], n=tm).wait()
    meta = meta_ref[...]
    f = jnp.zeros((tm, D_MODEL), F32)
    for c in range(MOE_TOPK):
        f = f + _tiles_to_rows(buf_ref.at[cur, c], tm) * meta[:, META_GATE + c:META_GATE + c + 1]
    o_ref[...] = _layernorm(alpha * y_ref[...] + (1.0 + gf_ref[...]) * f,
                            lng_ref[1:2, :], lnb_ref[1:2, :])


def _combine(ys, y3, slots3, meta, gf, ln_g, ln_b, tm, alpha):
    t, d = y3.shape
    last = t // tm - 1

    def slot_spec(ahead):
        return pl.BlockSpec((None, 1, MOE_TOPK * tm), lambda i: (jnp.minimum(i + ahead, last), 0, 0),
                            memory_space=pltpu.SMEM)

    return pl.pallas_call(
        functools.partial(_combine_kernel, alpha),
        grid=(t // tm,),
        in_specs=[slot_spec(0), slot_spec(1),
                  pl.BlockSpec(memory_space=pl.ANY),
                  pl.BlockSpec((tm, d), lambda i: (i, 0)),
                  pl.BlockSpec((tm, LANES), lambda i: (i, 0)),
                  gf[1], _resident((2, d)), _resident((2, d))],
        out_specs=pl.BlockSpec((tm, d), lambda i: (i, 0)),
        out_shape=jax.ShapeDtypeStruct((t, d), F32),
        scratch_shapes=[pltpu.VMEM((2, MOE_TOPK, tm * ROW_TILES, LANES), F32), pltpu.SemaphoreType.DMA((2,))],
        compiler_params=_params(1),
        name="moe_combine",
    )(slots3, slots3, ys, y3, meta, gf[0], ln_g, ln_b)


def _tile_slots(slots, tm):
    k, t = slots.shape
    return slots.reshape(k, t // tm, tm).transpose(1, 0, 2).reshape(t // tm, 1, k * tm)


def kernel(x_prompt, x_sample, cache_k, cache_v, state_conv, page_table, c_prompt, c_sample,
           ada_w, ada_b, ln_g, ln_b, attn_w_qkv, attn_w_o,
           conv_w_pw1, conv_b_pw1, conv_w_dw, conv_b_dw, conv_ln_g, conv_ln_b, conv_w_pw2, conv_b_pw2,
           ffn_w1, ffn_w3, ffn_w2, moe_w_router, moe_b_router, moe_w1, moe_w3, moe_w2):
    b, s, d = x_prompt.shape
    db, ds, _ = x_sample.shape
    depth = ada_w.shape[0]
    assert d == D_MODEL and ds == 1 and depth == 2
    assert page_table.shape[1] * PAGE_SIZE % MOBA_BLOCK == 0
    alpha = (2 * depth) ** 0.25
    tp = b * s
    t_all = tp + db
    assert tp % TM_ROUTE == 0 and tp % TM_COMBINE == 0

    c_all = jnp.concatenate([c_sample, c_prompt], axis=0)
    mods = _adaln(c_all, ada_w, ada_b)
    mods = _Mods(mods.reshape(depth * 6, db + b, d), db)

    def pmod(layer, k, tm):
        tps = s // tm
        return mods.prompt(layer * 6 + k, lambda i: i // tps)

    def smod(layer, k, rows=None, block_of_step=lambda i: 0):
        return mods.sample(layer * 6 + k, db if rows is None else rows, block_of_step)

    xp = x_prompt.reshape(tp, d)
    xs_ = x_sample.reshape(db, d)

    wqkv = attn_w_qkv[0].astype(BF16)
    qp, kp, vp = _qkv(xp, pmod(0, 0, TM_QKV), pmod(0, 1, TM_QKV), wqkv, TM_QKV)
    qs, ks, vs = _qkv(xs_, smod(0, 0), smod(0, 1), wqkv, db)
    attn_p, kt_p, vt_p = _moba_prompt(qp.reshape(b, s, d), kp.reshape(b, s, d), vp.reshape(b, s, d))
    attn_p = attn_p.reshape(tp, d)
    attn_s = _moba_sample(qs, ks, vs, cache_k, cache_v, 0, page_table)

    assert ffn_w1.shape[2] % FFN_CHUNK == 0 and moe_w1.shape[3] % TF_GMM == 0
    wo = attn_w_o[0].astype(BF16)
    ffn_w = (ffn_w1[0].astype(BF16), ffn_w3[0].astype(BF16), ffn_w2[0].astype(BF16))
    y_p = _post_attn(attn_p, xp, pmod(0, 2, TM_POST), pmod(0, 3, TM_POST), pmod(0, 4, TM_POST),
                     pmod(0, 5, TM_POST), ln_g[0], ln_b[0], wo, *ffn_w, TM_POST, alpha)
    y_s = _post_attn(attn_s, xs_, smod(0, 2), smod(0, 3), smod(0, 4), smod(0, 5),
                     ln_g[0], ln_b[0], wo, *ffn_w, db, alpha)

    wr = jnp.pad(moe_w_router[0], ((0, 0), (0, LANES - N_EXPERTS)))
    wr_hi = wr.astype(BF16)
    wr = jnp.stack([wr_hi, (wr - wr_hi.astype(F32)).astype(BF16)])
    br = jnp.pad(moe_b_router[0], (0, LANES - N_EXPERTS)).reshape(1, LANES)
    conv_weights = (ln_g[1], ln_b[1], conv_w_pw1[0].astype(BF16), conv_b_pw1[0].reshape(1, 2 * d),
                    conv_w_dw[0], conv_b_dw[0].reshape(1, d), conv_ln_g[0].reshape(1, d),
                    conv_ln_b[0].reshape(1, d), conv_w_pw2[0].astype(BF16), conv_b_pw2[0].reshape(1, d),
                    wr, br)
    pm = [mods.prompt(6 + k, lambda bi, i: bi) for k in range(5)]
    y3_p, h2t_p, lg_p, tail = _conv_prompt(y_p, pm, conv_weights, b, s, alpha)
    sm = [smod(1, k, 32, lambda i: i) for k in range(5)]
    hist_t = jnp.transpose(state_conv[0], (1, 0, 2))
    y3_s, h2t_s, lg_s, u_s = _conv_sample(y_s, sm, conv_weights, hist_t, alpha)

    meta_p, meta_pt, meta_s, meta_st, cnt = _route(lg_p, lg_s)
    counts = cnt[0, 0:N_EXPERTS].astype(jnp.int32)
    padded = (counts + TM_GMM - 1) // TM_GMM * TM_GMM
    pad_end = jnp.cumsum(padded)
    pad_start = pad_end - padded
    n_tiles = -(-t_all * MOE_TOPK // TM_GMM) + N_EXPERTS
    n_rows = n_tiles * TM_GMM

    def slots_of(meta_t):
        e = meta_t[META_EXPERT:META_EXPERT + MOE_TOPK].astype(jnp.int32)
        slot = meta_t[META_RANK:META_RANK + MOE_TOPK].astype(jnp.int32)
        for x in range(N_EXPERTS):
            slot = slot + jnp.where(e == x, pad_start[x], 0)
        return slot

    slots_p, slots_s = slots_of(meta_pt), slots_of(meta_st)
    zero_lo = pad_start + counts
    zero_hi = jnp.concatenate([pad_start[1:], jnp.full((1,), n_rows, jnp.int32)])
    xs_buf = _scatter(h2t_p, h2t_s, _tile_slots(slots_p, TM_ROUTE), _tile_slots(slots_s, db),
                      zero_lo.astype(jnp.int32), zero_hi.astype(jnp.int32), n_rows)

    tile_ids = jnp.arange(n_tiles, dtype=jnp.int32)
    tile_exp = jnp.minimum(jnp.sum((pad_end // TM_GMM)[None, :] <= tile_ids[:, None], axis=1),
                           N_EXPERTS - 1).astype(jnp.int32)
    n_valid = (pad_end[-1] // TM_GMM).reshape(1).astype(jnp.int32)
    ys_buf = _gmm(xs_buf, moe_w1[0], moe_w3[0], moe_w2[0], tile_exp, n_valid)

    out_p = _combine(ys_buf, y3_p, _tile_slots(slots_p, TM_COMBINE), meta_p,
                     pmod(1, 5, TM_COMBINE), ln_g[1], ln_b[1], TM_COMBINE, alpha)
    out_s = _combine(ys_buf, y3_s, _tile_slots(slots_s, db), meta_s,
                     smod(1, 5), ln_g[1], ln_b[1], db, alpha)

    nh = CONV_WIDTH - 1
    new_conv_p = tail[:, HIST_PAD - nh:, :][None]
    new_conv_s = jnp.transpose(jnp.concatenate([hist_t[1:], u_s[None]], axis=0), (1, 0, 2))[None]
    hshape = (N_HEADS, HEAD_DIM)

    def seq_major(t):
        return jnp.transpose(t.reshape(1, b, *hshape, s), (0, 1, 4, 2, 3))

    return (out_p.reshape(b, s, d), out_s.reshape(db, 1, d), seq_major(kt_p), seq_major(vt_p),
            ks.reshape(1, db, 1, *hshape), vs.reshape(1, db, 1, *hshape),
            new_conv_p, new_conv_s)
```

```python
import functools
import math

import jax
import jax.numpy as jnp
from jax import lax
from jax.experimental import pallas as pl
from jax.experimental.pallas import tpu as pltpu

F32 = jnp.float32
BF16 = jnp.bfloat16
HIGHEST = lax.Precision.HIGHEST
NEG_INF = float("-inf")
LOG2E = math.log2(math.e)

SUBLANES = 8
LANES = 128

D_MODEL = 1024
N_HEADS = 16
HEAD_DIM = D_MODEL // N_HEADS
HEADS_PER_STEP = 8
MOBA_BLOCK = 256
MOBA_TOPK = 3
PAGE_SIZE = 128
CONV_WIDTH = 31
HIST_PAD = 32
N_EXPERTS = 8
MOE_TOPK = 2
LN_EPS = 1e-5
ROW_TILES = D_MODEL // LANES

TM_QKV = 512
TM_POST = 512
TM_CONV = 512
TM_ROUTE = 512
TM_GMM = 512
TM_COMBINE = 512
TF_GMM = 512
FFN_CHUNK = 256
ZERO_ROWS = 64
VMEM_LIMIT = 56 * 1024 * 1024

META_EXPERT, META_GATE, META_RANK = 0, 2, 4

NT_DIMS = (((1,), (1,)), ((), ()))


def _dot(a, b, precision=None):
    return jnp.dot(a, b, preferred_element_type=F32, precision=precision)


def _dot_nt(a, b, precision=None):
    return lax.dot_general(a, b, NT_DIMS, preferred_element_type=F32, precision=precision)


def _silu(x):
    return x * jax.nn.sigmoid(x)


def _layernorm(z, g, b):
    mu = jnp.mean(z, axis=-1, keepdims=True)
    zc = z - mu
    var = jnp.mean(zc * zc, axis=-1, keepdims=True)
    return zc * lax.rsqrt(var + LN_EPS) * g + b


def _resident(shape):
    nd = len(shape)
    return pl.BlockSpec(shape, lambda *_: (0,) * nd, pipeline_mode=pl.Buffered(1))


def _params(n_grid_dims):
    return pltpu.CompilerParams(
        dimension_semantics=("arbitrary",) * n_grid_dims, vmem_limit_bytes=VMEM_LIMIT)


def _rows_to_tiles(ref, x):
    t = x.shape[0]
    for s in range(ROW_TILES):
        ref[pl.ds(s, t, stride=ROW_TILES), :] = x[:, s * LANES:(s + 1) * LANES]


def _tiles_to_rows(ref, t):
    return jnp.concatenate([ref[pl.ds(s, t, stride=ROW_TILES), :] for s in range(ROW_TILES)], axis=1)


def _ada_kernel(c_ref, w_ref, b_ref, o_ref):
    c = c_ref[...]
    o_ref[...] = _dot(_silu(c).astype(BF16), w_ref[...].astype(BF16)) + b_ref[...]


def _adaln(c_all, ada_w, ada_b):
    depth = ada_w.shape[0]
    n = c_all.shape[0]
    d = D_MODEL
    return pl.pallas_call(
        _ada_kernel,
        grid=(depth, 6),
        in_specs=[
            pl.BlockSpec((n, d), lambda l, k: (0, 0)),
            pl.BlockSpec((None, d, d), lambda l, k: (l, 0, k)),
            pl.BlockSpec((None, None, 1, d), lambda l, k: (l, k, 0, 0)),
        ],
        out_specs=pl.BlockSpec((None, None, n, d), lambda l, k: (l, k, 0, 0)),
        out_shape=jax.ShapeDtypeStruct((depth, 6, n, d), F32),
        compiler_params=_params(2),
        name="adaln",
    )(c_all, ada_w, ada_b.reshape(depth, 6, 1, d))


class _Mods:
    def __init__(self, mods, n_sample):
        self.n_sample = n_sample
        self.m3 = mods
        self.m4 = mods.reshape(mods.shape[0], mods.shape[1], 1, D_MODEL)

    def prompt(self, lk, seq_of_step):
        ns = self.n_sample
        return self.m4, pl.BlockSpec(
            (None, None, 1, D_MODEL), lambda *g: (lk, ns + seq_of_step(*g), 0, 0))

    def sample(self, lk, rows, block_of_step):
        return self.m3, pl.BlockSpec((None, rows, D_MODEL), lambda *g: (lk, block_of_step(*g), 0))


def _qkv_kernel(x_ref, sh_ref, sc_ref, w_ref, q_ref, k_ref, v_ref):
    d = D_MODEL
    h = (x_ref[...] * (1.0 + sc_ref[...]) + sh_ref[...]).astype(BF16)
    q_ref[...] = _dot(h, w_ref[:, 0:d])
    k_ref[...] = _dot(h, w_ref[:, d:2 * d])
    v_ref[...] = _dot(h, w_ref[:, 2 * d:3 * d])


def _qkv(x, sh, sc, w_bf16, tm):
    t = x.shape[0]
    d = D_MODEL
    row = pl.BlockSpec((tm, d), lambda i: (i, 0))
    out = jax.ShapeDtypeStruct((t, d), F32)
    return pl.pallas_call(
        _qkv_kernel,
        grid=(t // tm,),
        in_specs=[row, sh[1], sc[1], _resident((d, 3 * d))],
        out_specs=[row, row, row],
        out_shape=[out, out, out],
        compiler_params=_params(1),
        name="qkv",
    )(x, sh[0], sc[0], w_bf16)


def _select_blocks(gate_t, n_valid):
    nb = gate_t.shape[0]
    blk = lax.broadcasted_iota(jnp.int32, gate_t.shape, 0)
    valid = blk < n_valid
    rows = []
    for n in range(nb):
        gn = gate_t[n:n + 1, :]
        beats = ((gate_t > gn) | ((gate_t == gn) & (blk < n))) & valid
        rows.append(jnp.sum(beats.astype(F32), axis=0, keepdims=True))
    cnt = jnp.concatenate(rows, axis=0)
    return (valid & (cnt < MOBA_TOPK)).astype(F32)


def _attn_kernel(q_ref, k_ref, v_ref, bias_ref, o_ref, kt_ref, vt_ref, kb_ref, ve_ref, km_ref, mb_ref, acc_ref):
    blk = MOBA_BLOCK
    s_len = k_ref.shape[0]
    nb = s_len // blk
    nh = HEADS_PER_STEP
    cb = pl.program_id(1)
    own = pl.program_id(2)

    @pl.when(own == 0)
    def _():
        kb_ref[...] = k_ref[...].astype(BF16)
        lane_s = lax.broadcasted_iota(jnp.int32, (s_len, LANES), 1)
        lane_k = lax.broadcasted_iota(jnp.int32, (nb, LANES), 1) // HEAD_DIM
        for pp in range(nh // 2):
            cols = slice(pp * LANES, (pp + 1) * LANES)
            kt_ref[cols, :] = jnp.transpose(k_ref[:, cols])
            vt_ref[cols, :] = jnp.transpose(v_ref[:, cols])
            rows = [jnp.mean(k_ref[n * blk:(n + 1) * blk, cols], axis=0, keepdims=True) for n in range(nb)]
            km = jnp.concatenate(rows, axis=0)
            km_ref[pp] = jnp.concatenate([jnp.where(lane_k == 0, km, 0.0), jnp.where(lane_k == 1, km, 0.0)], axis=0)
            v = v_ref[:, cols]
            ve_ref[2 * pp] = jnp.where(lane_s < HEAD_DIM, v, jnp.where(lane_s == HEAD_DIM, 1.0, 0.0)).astype(BF16)
            ve_ref[2 * pp + 1] = jnp.where(lane_s >= HEAD_DIM, v, jnp.where(lane_s == 0, 1.0, 0.0)).astype(BF16)

    qi = lax.broadcasted_iota(jnp.int32, (blk, blk), 0)
    ki = lax.broadcasted_iota(jnp.int32, (blk, blk), 1)
    causal = qi >= ki
    eye = (qi == ki).astype(BF16)
    lane = lax.broadcasted_iota(jnp.int32, (blk, LANES), 1)
    lane_head = lane // HEAD_DIM
    blk_off = (own - lane % nb).astype(F32) * float(blk)
    own_start = pl.multiple_of(own * blk, blk)

    qh = []
    for pp in range(nh // 2):
        q = q_ref[:, pp * LANES:(pp + 1) * LANES]
        qs_all = (q * (HEAD_DIM ** -0.5 * LOG2E)).astype(BF16)
        gates = _dot_nt(km_ref[pp], q, precision=HIGHEST)
        sel_t = jnp.concatenate([_select_blocks(gates[hh * nb:(hh + 1) * nb, :], own) for hh in range(2)]
                                + [jnp.zeros((LANES - 2 * nb, blk), F32)], axis=0).astype(BF16)
        sel = _dot_nt(eye, sel_t)
        head_no = (cb * nh + 2 * pp + 1 + (lane >= nb).astype(jnp.int32)).astype(F32)
        slope2 = jnp.exp2(head_no * (-8.0 / N_HEADS)) * LOG2E
        mb_ref[pp] = jnp.where(sel > 0.5, -slope2 * blk_off, NEG_INF)
        for hh in range(2):
            qh.append(jnp.where(lane_head == hh, qs_all, jnp.zeros_like(qs_all)))

    ms = []
    for h in range(nh):
        pp = h // 2
        kd = kb_ref[pl.ds(own_start, blk), pp * LANES:(pp + 1) * LANES]
        s = jnp.where(causal, _dot_nt(qh[h], kd) + bias_ref[h], NEG_INF)
        m = jnp.max(s, axis=1, keepdims=True)
        p = jnp.exp2(s - m)
        acc_ref[h] = _dot(p.astype(BF16), ve_ref[h, pl.ds(own_start, blk), :])
        ms.append(m)

    def body(j, carry):
        start = pl.multiple_of(j * blk, blk)
        out = []
        for h in range(nh):
            pp, hh = h // 2, h % 2
            kj = kb_ref[pl.ds(start, blk), pp * LANES:(pp + 1) * LANES]
            m = carry[h]
            s = _dot_nt(qh[h], kj) + bias_ref[h]
            mb = jnp.sum(jnp.where(lane == j + hh * nb, mb_ref[pp], 0.0), axis=1, keepdims=True)
            m_new = jnp.maximum(m, jnp.max(s, axis=1, keepdims=True) + mb)
            p = jnp.exp2(s + (mb - m_new))
            a = jnp.exp2(m - m_new)
            acc_ref[h] = a * acc_ref[h] + _dot(p.astype(BF16), ve_ref[h, pl.ds(start, blk), :])
            out.append(m_new)
        return tuple(out)

    lax.fori_loop(0, own, body, tuple(ms))
    for pp in range(nh // 2):
        acc0 = acc_ref[2 * pp]
        acc1 = acc_ref[2 * pp + 1]
        out = jnp.where(lane < HEAD_DIM, acc0 / acc0[:, HEAD_DIM:HEAD_DIM + 1], acc1 / acc1[:, 0:1])
        o_ref[:, pp * LANES:(pp + 1) * LANES] = out.astype(o_ref.dtype)


def _alibi_bias():
    slopes = 2.0 ** (-8.0 * jnp.arange(1, N_HEADS + 1, dtype=F32) / N_HEADS)
    qi = jnp.arange(MOBA_BLOCK, dtype=F32)[:, None]
    ki = jnp.arange(MOBA_BLOCK, dtype=F32)[None, :]
    return -(slopes * LOG2E)[:, None, None] * (qi - ki)[None]


def _moba_prompt(q, k, v):
    b, s, d = q.shape
    nb = s // MOBA_BLOCK
    nh = HEADS_PER_STEP
    w = nh * HEAD_DIM
    qspec = pl.BlockSpec((None, MOBA_BLOCK, w), lambda bi, cb, qb: (bi, qb, cb))
    kvspec = pl.BlockSpec((None, s, w), lambda bi, cb, qb: (bi, 0, cb))
    bspec = pl.BlockSpec((nh, MOBA_BLOCK, MOBA_BLOCK), lambda bi, cb, qb: (cb, 0, 0))
    tspec = pl.BlockSpec((None, w, s), lambda bi, cb, qb: (bi, cb, 0))
    return pl.pallas_call(
        _attn_kernel,
        grid=(b, d // w, nb),
        in_specs=[qspec, kvspec, kvspec, bspec],
        out_specs=[qspec, tspec, tspec],
        out_shape=[jax.ShapeDtypeStruct((b, s, d), BF16), jax.ShapeDtypeStruct((b, d, s), F32),
                   jax.ShapeDtypeStruct((b, d, s), F32)],
        scratch_shapes=[pltpu.VMEM((s, w), BF16), pltpu.VMEM((nh, s, LANES), BF16),
                        pltpu.VMEM((nh // 2, 2 * nb, LANES), F32),
                        pltpu.VMEM((nh // 2, MOBA_BLOCK, LANES), F32),
                        pltpu.VMEM((nh, MOBA_BLOCK, LANES), F32)],
        compiler_params=_params(3),
        name="moba_prompt",
    )(q, k, v, _alibi_bias())


def _attn_sample_probs_kernel(n_pages, pt_ref, q_ref, qt_ref, kn_ref, *refs):
    del pt_ref
    kp = refs[:n_pages]
    p_ref, misc_ref, s_ref = refs[n_pages:]
    past = n_pages * PAGE_SIZE
    nb = past // MOBA_BLOCK
    scale = HEAD_DIM ** -0.5

    q = q_ref[...]
    qt = qt_ref[...]
    for h in range(N_HEADS):
        qb = jnp.broadcast_to(qt[:, h:h + 1], (HEAD_DIM, PAGE_SIZE))
        for p in range(n_pages):
            s_ref[h:h + 1, p * PAGE_SIZE:(p + 1) * PAGE_SIZE] = jnp.sum(kp[p][h] * qb, axis=0, keepdims=True)
    raw = s_ref[...]
    gates = [jnp.sum(raw[:, n * MOBA_BLOCK:(n + 1) * MOBA_BLOCK], axis=1, keepdims=True) * (1.0 / MOBA_BLOCK)
             for n in range(nb)]
    lane_blk = lax.broadcasted_iota(jnp.int32, (N_HEADS, past), 1) // MOBA_BLOCK
    keep = jnp.zeros((N_HEADS, past), jnp.bool_)
    n_kept = jnp.zeros((N_HEADS, 1), F32)
    kept_ids = [jnp.zeros((N_HEADS, 1), F32) for _ in range(MOBA_TOPK)]
    for n in range(nb):
        cnt = jnp.zeros((N_HEADS, 1), F32)
        for j in range(nb):
            if j != n:
                beats = (gates[j] > gates[n]) | ((gates[j] == gates[n]) & (j < n))
                cnt = cnt + beats.astype(F32)
        keep_n = cnt < MOBA_TOPK
        keep = keep | ((lane_blk == n) & keep_n)
        for j in range(MOBA_TOPK):
            kept_ids[j] = jnp.where(keep_n & (n_kept == j), float(n), kept_ids[j])
        n_kept = n_kept + keep_n.astype(F32)
    head = lax.broadcasted_iota(jnp.int32, (N_HEADS, 1), 0)
    slope = jnp.exp2((head + 1).astype(F32) * (-8.0 / N_HEADS))
    kpos = lax.broadcasted_iota(jnp.int32, (1, past), 1).astype(F32)
    s = jnp.where(keep, raw * scale - slope * (float(past) - kpos), NEG_INF)
    s_own = jnp.sum(q * kn_ref[...], axis=1, keepdims=True) * scale
    m = jnp.maximum(jnp.max(s, axis=1, keepdims=True), s_own)
    e = jnp.exp(s - m)
    w_own = jnp.exp(s_own - m)
    inv = 1.0 / (jnp.sum(e, axis=1, keepdims=True) + w_own)
    p = e * inv
    for pg in range(n_pages):
        p_ref[:, pg, :] = p[:, pg * PAGE_SIZE:(pg + 1) * PAGE_SIZE]
    lane = lax.broadcasted_iota(jnp.int32, (N_HEADS, LANES), 1)
    misc = jnp.where(lane == MOBA_TOPK, w_own * inv, 0.0)
    for j in range(MOBA_TOPK):
        misc = jnp.where(lane == j, kept_ids[j], misc)
    misc_ref[...] = misc


def _attn_sample_values_kernel(n_pages, layer, pt_ref, kept_ref, p_ref, misc_ref, vnt_ref, cv_ref, o_ref,
                               vbuf_ref, sem):
    ppb = MOBA_BLOCK // PAGE_SIZE
    per_head = MOBA_TOPK * ppb
    n_chunks = N_HEADS * per_head
    i = pl.program_id(0)
    cur = i % 2

    def chunk_copy(seq, c, slot):
        h, j, pg = c // per_head, (c % per_head) // ppb, c % ppb
        page = pt_ref[seq * n_pages + kept_ref[(seq * N_HEADS + h) * MOBA_TOPK + j] * ppb + pg]
        return pltpu.make_async_copy(cv_ref.at[layer, page, h], vbuf_ref.at[slot, c], sem.at[slot])

    def fetch(seq, slot):
        for c in range(n_chunks):
            chunk_copy(seq, c, slot).start()

    @pl.when(i == 0)
    def _():
        fetch(0, 0)

    for slot in range(2):
        @pl.when((i + 1 < pl.num_programs(0)) & (cur != slot))
        def _(slot=slot):
            fetch(i + 1, slot)

    for slot in range(2):
        @pl.when(cur == slot)
        def _(slot=slot):
            for c in range(n_chunks):
                chunk_copy(i, c, slot).wait()

    misc = misc_ref[...]
    lane_h = lax.broadcasted_iota(jnp.int32, (HEAD_DIM, N_HEADS), 1)
    out_t = jnp.zeros((HEAD_DIM, N_HEADS), F32)
    for h in range(N_HEADS):
        acc = jnp.zeros((HEAD_DIM, PAGE_SIZE), F32)
        for j in range(MOBA_TOPK):
            blk = kept_ref[(i * N_HEADS + h) * MOBA_TOPK + j]
            for pg in range(ppb):
                w = p_ref[h, pl.ds(blk * ppb + pg, 1), :]
                acc = acc + vbuf_ref[cur, h * per_head + j * ppb + pg] * w
        w_own = misc[h:h + 1, MOBA_TOPK:MOBA_TOPK + 1]
        col = jnp.sum(acc, axis=1, keepdims=True) + w_own * vnt_ref[:, h:h + 1]
        out_t = jnp.where(lane_h == h, col, out_t)
    o_ref[...] = out_t


def _moba_sample(q, k_new, v_new, cache_k, cache_v, layer, page_table):
    n_seq, n_pages = page_table.shape
    assert n_pages * PAGE_SIZE // MOBA_BLOCK >= MOBA_TOPK
    hd = (N_HEADS, HEAD_DIM)
    dh = (HEAD_DIM, N_HEADS)
    ck = jnp.transpose(cache_k, (0, 1, 3, 4, 2))
    cv = jnp.transpose(cache_v, (0, 1, 3, 4, 2))
    pt = page_table.reshape(-1).astype(jnp.int32)
    q3 = q.reshape((n_seq,) + hd)
    hd_spec = pl.BlockSpec((None,) + hd, lambda i, *_: (i, 0, 0))
    dh_spec = pl.BlockSpec((None,) + dh, lambda i, *_: (i, 0, 0))
    p_spec = pl.BlockSpec((None, N_HEADS, n_pages, PAGE_SIZE), lambda i, *_: (i, 0, 0, 0))
    misc_spec = pl.BlockSpec((None, N_HEADS, LANES), lambda i, *_: (i, 0, 0))

    def page_spec(p):
        return pl.BlockSpec((None, None) + hd + (PAGE_SIZE,),
                            lambda i, pt: (layer, pt[i * n_pages + p], 0, 0, 0))

    probs, misc = pl.pallas_call(
        functools.partial(_attn_sample_probs_kernel, n_pages),
        grid_spec=pltpu.PrefetchScalarGridSpec(
            num_scalar_prefetch=1,
            grid=(n_seq,),
            in_specs=[hd_spec, dh_spec, hd_spec] + [page_spec(p) for p in range(n_pages)],
            out_specs=[p_spec, misc_spec],
            scratch_shapes=[pltpu.VMEM((N_HEADS, n_pages * PAGE_SIZE), F32)],
        ),
        out_shape=[jax.ShapeDtypeStruct((n_seq, N_HEADS, n_pages, PAGE_SIZE), F32),
                   jax.ShapeDtypeStruct((n_seq, N_HEADS, LANES), F32)],
        compiler_params=_params(1),
        name="moba_sample_probs",
    )(pt, q3, q3.transpose(0, 2, 1), k_new.reshape((n_seq,) + hd), *([ck] * n_pages))

    kept = misc[:, :, 0:MOBA_TOPK].astype(jnp.int32).reshape(-1)
    n_chunks = N_HEADS * MOBA_TOPK * (MOBA_BLOCK // PAGE_SIZE)
    out_t = pl.pallas_call(
        functools.partial(_attn_sample_values_kernel, n_pages, layer),
        grid_spec=pltpu.PrefetchScalarGridSpec(
            num_scalar_prefetch=2,
            grid=(n_seq,),
            in_specs=[p_spec, misc_spec, dh_spec, pl.BlockSpec(memory_space=pl.ANY)],
            out_specs=dh_spec,
            scratch_shapes=[pltpu.VMEM((2, n_chunks, HEAD_DIM, PAGE_SIZE), F32), pltpu.SemaphoreType.DMA((2,))],
        ),
        out_shape=jax.ShapeDtypeStruct((n_seq,) + dh, F32),
        compiler_params=_params(1),
        name="moba_sample_values",
    )(pt, kept, probs, misc, v_new.reshape((n_seq,) + hd).transpose(0, 2, 1), cv)
    return out_t.transpose(0, 2, 1).reshape(n_seq, D_MODEL).astype(BF16)


def _post_kernel(alpha, a_ref, x_ref, gm_ref, shf_ref, scf_ref, gf_ref, lng_ref, lnb_ref,
                 wo_ref, w1_ref, w3_ref, w2_ref, o_ref, g_ref):
    m = _dot(a_ref[...], wo_ref[...])
    y1 = _layernorm(alpha * x_ref[...] + (1.0 + gm_ref[...]) * m, lng_ref[0:1, :], lnb_ref[0:1, :])
    h = (y1 * (1.0 + scf_ref[...]) + shf_ref[...]).astype(BF16)
    _swiglu_hidden(h, w1_ref, w3_ref, g_ref, FFN_CHUNK)
    f = _dot(g_ref[...], w2_ref[...])
    o_ref[...] = _layernorm(alpha * y1 + (1.0 + gf_ref[...]) * f, lng_ref[1:2, :], lnb_ref[1:2, :])


def _swiglu_hidden(h, w1_ref, w3_ref, g_ref, chunk):
    for c in range(w1_ref.shape[1] // chunk):
        cols = slice(c * chunk, (c + 1) * chunk)
        t1 = _dot(h, w1_ref[:, cols])
        t3 = _dot(h, w3_ref[:, cols])
        g_ref[:, cols] = (_silu(t1) * t3).astype(BF16)


def _post_attn(attn, x, gm, shf, scf, gf, ln_g, ln_b, wo, w1, w3, w2, tm, alpha):
    t = x.shape[0]
    d = D_MODEL
    row = pl.BlockSpec((tm, d), lambda i: (i, 0))
    return pl.pallas_call(
        functools.partial(_post_kernel, alpha),
        grid=(t // tm,),
        in_specs=[row, row, gm[1], shf[1], scf[1], gf[1], _resident((2, d)), _resident((2, d)),
                  _resident(wo.shape), _resident(w1.shape), _resident(w3.shape), _resident(w2.shape)],
        out_specs=row,
        out_shape=jax.ShapeDtypeStruct((t, d), F32),
        scratch_shapes=[pltpu.VMEM((tm, w1.shape[1]), BF16)],
        compiler_params=_params(1),
        name="post_attn_ffn",
    )(attn, x, gm[0], shf[0], scf[0], gf[0], ln_g, ln_b, wo, w1, w3, w2)


def _glu(y, shm_ref, scm_ref, wpw1_ref, bpw1_ref):
    d = D_MODEL
    h = (y * (1.0 + scm_ref[...]) + shm_ref[...]).astype(BF16)
    a = _dot(h, wpw1_ref[:, 0:d]) + bpw1_ref[:, 0:d]
    g = _dot(h, wpw1_ref[:, d:2 * d]) + bpw1_ref[:, d:2 * d]
    return a * jax.nn.sigmoid(g)


def _conv_tail(alpha, y, conv, gm_ref, shf_ref, scf_ref, lng_ref, lnb_ref, clg_ref, clb_ref,
               wpw2_ref, bpw2_ref, wr_ref, br_ref, y3_ref, h2_ref, lg_ref):
    z = _silu(_layernorm(conv, clg_ref[...], clb_ref[...])).astype(BF16)
    m = _dot(z, wpw2_ref[...]) + bpw2_ref[...]
    y3 = _layernorm(alpha * y + (1.0 + gm_ref[...]) * m, lng_ref[0:1, :], lnb_ref[0:1, :])
    y3_ref[...] = y3
    h2 = y3 * (1.0 + scf_ref[...]) + shf_ref[...]
    _rows_to_tiles(h2_ref, h2)
    h2_hi = h2.astype(BF16)
    h2_lo = (h2 - h2_hi.astype(F32)).astype(BF16)
    lg_ref[...] = (_dot(h2_hi, wr_ref[0]) + _dot(h2_lo, wr_ref[0]) + _dot(h2_hi, wr_ref[1])) + br_ref[...]


def _conv_prompt_kernel(alpha, y_ref, shm_ref, scm_ref, gm_ref, shf_ref, scf_ref, lng_ref, lnb_ref,
                        wpw1_ref, bpw1_ref, wdw_ref, bdw_ref, clg_ref, clb_ref, wpw2_ref, bpw2_ref,
                        wr_ref, br_ref, y3_ref, h2_ref, lg_ref, tail_ref, ext_ref, shift_ref, conv_ref, wrep_ref):
    tm = y_ref.shape[0]
    d = D_MODEL
    rows_ext = tm + HIST_PAD

    @pl.when(pl.program_id(1) == 0)
    def _():
        ext_ref[0:HIST_PAD, :] = jnp.zeros((HIST_PAD, d), F32)

    y = y_ref[...]
    u = _glu(y, shm_ref, scm_ref, wpw1_ref, bpw1_ref)
    ext_ref[HIST_PAD:rows_ext, :] = u
    tail_ref[...] = u[tm - HIST_PAD:tm, :]
    span = rows_ext - SUBLANES
    for b in range(1, SUBLANES):
        shift_ref[b - 1, :, :] = ext_ref[b:b + span, :]
    base = HIST_PAD - (CONV_WIDTH - 1)
    rb = 4 * SUBLANES
    for k in range(CONV_WIDTH):
        wrep_ref[k] = jnp.broadcast_to(wdw_ref[k:k + 1, :], (SUBLANES, d))
    wrep_ref[CONV_WIDTH] = jnp.broadcast_to(bdw_ref[...], (SUBLANES, d))

    def body(r, carry):
        r0 = pl.multiple_of(r * rb, rb)
        acc = jnp.broadcast_to(wrep_ref[CONV_WIDTH], (rb // SUBLANES, SUBLANES, d))
        for k in range(CONV_WIDTH):
            off = base + k
            start = r0 + (off // SUBLANES) * SUBLANES
            if off % SUBLANES == 0:
                x = ext_ref[pl.ds(start, rb), :]
            else:
                x = shift_ref[off % SUBLANES - 1, pl.ds(start, rb), :]
            acc = acc + wrep_ref[k] * x.reshape(rb // SUBLANES, SUBLANES, d)
        conv_ref[pl.ds(r0, rb), :] = acc.reshape(rb, d)
        return carry

    lax.fori_loop(0, tm // rb, body, 0)
    ext_ref[0:HIST_PAD, :] = u[tm - HIST_PAD:tm, :]
    _conv_tail(alpha, y, conv_ref[...], gm_ref, shf_ref, scf_ref, lng_ref, lnb_ref, clg_ref, clb_ref,
               wpw2_ref, bpw2_ref, wr_ref, br_ref, y3_ref, h2_ref, lg_ref)


def _conv_sample_kernel(alpha, y_ref, shm_ref, scm_ref, gm_ref, shf_ref, scf_ref, lng_ref, lnb_ref,
                        wpw1_ref, bpw1_ref, wdw_ref, bdw_ref, clg_ref, clb_ref, wpw2_ref, bpw2_ref,
                        wr_ref, br_ref, hist_ref, y3_ref, h2_ref, lg_ref, u_ref):
    nh = CONV_WIDTH - 1
    y = y_ref[...]
    u = _glu(y, shm_ref, scm_ref, wpw1_ref, bpw1_ref)
    u_ref[...] = u
    w = wdw_ref[...]
    conv = bdw_ref[...] + w[nh:nh + 1, :] * u
    for k in range(nh):
        conv = conv + w[k:k + 1, :] * hist_ref[k]
    _conv_tail(alpha, y, conv, gm_ref, shf_ref, scf_ref, lng_ref, lnb_ref, clg_ref, clb_ref,
               wpw2_ref, bpw2_ref, wr_ref, br_ref, y3_ref, h2_ref, lg_ref)


def _conv_weight_specs(d):
    return [_resident((2, d)), _resident((2, d)), _resident((d, 2 * d)), _resident((1, 2 * d)),
            _resident((CONV_WIDTH, d)), _resident((1, d)), _resident((1, d)), _resident((1, d)),
            _resident((d, d)), _resident((1, d)), _resident((2, d, LANES)), _resident((1, LANES))]


def _conv_prompt(y, mods5, weights, n_seq, seq_len, alpha):
    d = D_MODEL
    tm = TM_CONV
    tps = seq_len // tm
    t = n_seq * seq_len
    row = pl.BlockSpec((tm, d), lambda b, i: (b * tps + i, 0))
    trow = pl.BlockSpec((tm * ROW_TILES, LANES), lambda b, i: (b * tps + i, 0))
    lgrow = pl.BlockSpec((tm, LANES), lambda b, i: (b * tps + i, 0))
    tail = pl.BlockSpec((None, HIST_PAD, d), lambda b, i: (b, 0, 0))
    return pl.pallas_call(
        functools.partial(_conv_prompt_kernel, alpha),
        grid=(n_seq, tps),
        in_specs=[row] + [m[1] for m in mods5] + _conv_weight_specs(d),
        out_specs=[row, trow, lgrow, tail],
        out_shape=[jax.ShapeDtypeStruct((t, d), F32),
                   jax.ShapeDtypeStruct((t * ROW_TILES, LANES), F32),
                   jax.ShapeDtypeStruct((t, LANES), F32),
                   jax.ShapeDtypeStruct((n_seq, HIST_PAD, d), F32)],
        scratch_shapes=[pltpu.VMEM((tm + HIST_PAD, d), F32),
                        pltpu.VMEM((SUBLANES - 1, tm + HIST_PAD - SUBLANES, d), F32),
                        pltpu.VMEM((tm, d), F32),
                        pltpu.VMEM((CONV_WIDTH + 1, SUBLANES, d), F32)],
        compiler_params=_params(2),
        name="conv_prompt",
    )(y, *[m[0] for m in mods5], *weights)


def _conv_sample(y, mods5, weights, hist, alpha):
    d = D_MODEL
    n = y.shape[0]
    tm = 32
    row = pl.BlockSpec((tm, d), lambda i: (i, 0))
    return pl.pallas_call(
        functools.partial(_conv_sample_kernel, alpha),
        grid=(n // tm,),
        in_specs=[row] + [m[1] for m in mods5] + _conv_weight_specs(d)
        + [pl.BlockSpec((CONV_WIDTH - 1, tm, d), lambda i: (0, i, 0))],
        out_specs=[row, pl.BlockSpec((tm * ROW_TILES, LANES), lambda i: (i, 0)),
                   pl.BlockSpec((tm, LANES), lambda i: (i, 0)), row],
        out_shape=[jax.ShapeDtypeStruct((n, d), F32),
                   jax.ShapeDtypeStruct((n * ROW_TILES, LANES), F32),
                   jax.ShapeDtypeStruct((n, LANES), F32),
                   jax.ShapeDtypeStruct((n, d), F32)],
        compiler_params=_params(1),
        name="conv_sample",
    )(y, *[m[0] for m in mods5], *weights, hist)


def _route_tile(lg_ref, meta_ref, meta_t_ref, carry_ref):
    tm = lg_ref.shape[0]
    lane = lax.broadcasted_iota(jnp.int32, (tm, LANES), 1)
    lg = jnp.where(lane < N_EXPERTS, lg_ref[...], NEG_INF)
    m1 = jnp.max(lg, axis=1, keepdims=True)
    i1 = jnp.min(jnp.where(lg == m1, lane, LANES), axis=1, keepdims=True)
    lg2 = jnp.where(lane == i1, NEG_INF, lg)
    m2 = jnp.max(lg2, axis=1, keepdims=True)
    i2 = jnp.min(jnp.where(lg2 == m2, lane, LANES), axis=1, keepdims=True)
    e = jnp.exp(m2 - m1)
    g1 = 1.0 / (1.0 + e)
    g2 = e / (1.0 + e)
    hot1 = lane == i1
    hot2 = lane == i2
    onehot = (hot1 | hot2).astype(BF16)
    ri = lax.broadcasted_iota(jnp.int32, (tm, tm), 0)
    ci = lax.broadcasted_iota(jnp.int32, (tm, tm), 1)
    before = (ci < ri).astype(BF16)
    rank = _dot(before, onehot) + carry_ref[...]
    r1 = jnp.sum(jnp.where(hot1, rank, 0.0), axis=1, keepdims=True)
    r2 = jnp.sum(jnp.where(hot2, rank, 0.0), axis=1, keepdims=True)
    cols = (i1.astype(F32), i2.astype(F32), g1, g2, r1, r2)
    meta = jnp.zeros((tm, LANES), F32)
    for c, val in enumerate(cols):
        meta = jnp.where(lane == c, val, meta)
    meta_ref[...] = meta
    meta_t_ref[...] = jnp.transpose(meta)[0:SUBLANES, :]
    carry_ref[...] += jnp.sum(onehot.astype(F32), axis=0, keepdims=True)


def _route_kernel(n_prompt_tiles, lgp_ref, lgs_ref, mp_ref, mpt_ref, ms_ref, mst_ref, cnt_ref, carry_ref):
    i = pl.program_id(0)

    @pl.when(i == 0)
    def _():
        carry_ref[...] = jnp.zeros_like(carry_ref)

    @pl.when(i < n_prompt_tiles)
    def _():
        _route_tile(lgp_ref, mp_ref, mpt_ref, carry_ref)

    @pl.when(i == n_prompt_tiles)
    def _():
        _route_tile(lgs_ref, ms_ref, mst_ref, carry_ref)
        cnt_ref[...] = carry_ref[...]


def _route(lg_p, lg_s):
    tp, ts = lg_p.shape[0], lg_s.shape[0]
    tm = TM_ROUTE
    npt = tp // tm
    last = npt - 1
    return pl.pallas_call(
        functools.partial(_route_kernel, npt),
        grid=(npt + 1,),
        in_specs=[pl.BlockSpec((tm, LANES), lambda i: (jnp.minimum(i, last), 0)),
                  pl.BlockSpec((ts, LANES), lambda i: (0, 0))],
        out_specs=[pl.BlockSpec((tm, LANES), lambda i: (jnp.minimum(i, last), 0)),
                   pl.BlockSpec((SUBLANES, tm), lambda i: (0, jnp.minimum(i, last))),
                   pl.BlockSpec((ts, LANES), lambda i: (0, 0)),
                   pl.BlockSpec((SUBLANES, ts), lambda i: (0, 0)),
                   pl.BlockSpec((1, LANES), lambda i: (0, 0))],
        out_shape=[jax.ShapeDtypeStruct((tp, LANES), F32),
                   jax.ShapeDtypeStruct((SUBLANES, tp), F32),
                   jax.ShapeDtypeStruct((ts, LANES), F32),
                   jax.ShapeDtypeStruct((SUBLANES, ts), F32),
                   jax.ShapeDtypeStruct((1, LANES), F32)],
        scratch_shapes=[pltpu.VMEM((1, LANES), F32)],
        compiler_params=_params(1),
        name="moe_route",
    )(lg_p, lg_s)


def _row_copy(src_ref, src_row, dst_ref, dst_row, sem, n=1):
    return pltpu.make_async_copy(
        src_ref.at[pl.ds(pl.multiple_of(src_row * ROW_TILES, ROW_TILES), n * ROW_TILES), :],
        dst_ref.at[pl.ds(pl.multiple_of(dst_row * ROW_TILES, ROW_TILES), n * ROW_TILES), :], sem)


def _scatter_rows(slots_ref, h_ref, xs_ref, sem):
    tm = h_ref.shape[0] // ROW_TILES

    def issue(r, carry):
        for c in range(MOE_TOPK):
            _row_copy(h_ref, r, xs_ref, slots_ref[0, c * tm + r], sem).start()
        return carry

    lax.fori_loop(0, tm, issue, 0)
    for c in range(MOE_TOPK):
        _row_copy(h_ref, 0, xs_ref, 0, sem, n=tm).wait()


def _scatter_kernel(n_prompt_tiles, zlo_ref, zhi_ref, sp_ref, ss_ref, hp_ref, hs_ref, xs_ref,
                    zero_ref, sem, zsem):
    i = pl.program_id(0)

    @pl.when(i < n_prompt_tiles)
    def _():
        _scatter_rows(sp_ref, hp_ref, xs_ref, sem)

    @pl.when(i == n_prompt_tiles)
    def _():
        _scatter_rows(ss_ref, hs_ref, xs_ref, sem)
        zero_ref[...] = jnp.zeros_like(zero_ref)
        for e in range(N_EXPERTS):
            lo = zlo_ref[e]
            n = zhi_ref[e] - lo
            nbig = n // ZERO_ROWS

            def big(k, carry, lo=lo):
                return _row_copy(zero_ref, 0, xs_ref, lo + k * ZERO_ROWS, zsem, n=ZERO_ROWS)

            def small(r, carry, lo=lo):
                return _row_copy(zero_ref, 0, xs_ref, lo + r, zsem)

            lax.fori_loop(0, nbig, lambda k, c: (big(k, c).start(), c)[1], 0)
            lax.fori_loop(nbig * ZERO_ROWS, n, lambda r, c: (small(r, c).start(), c)[1], 0)
            lax.fori_loop(0, nbig, lambda k, c: (big(k, c).wait(), c)[1], 0)
            lax.fori_loop(nbig * ZERO_ROWS, n, lambda r, c: (small(r, c).wait(), c)[1], 0)


def _scatter(h2t_p, h2t_s, slots_p, slots_s, zero_lo, zero_hi, n_rows):
    tp = h2t_p.shape[0] // ROW_TILES
    ts = h2t_s.shape[0] // ROW_TILES
    tm = TM_ROUTE
    npt = tp // tm
    last = npt - 1
    grid_spec = pltpu.PrefetchScalarGridSpec(
        num_scalar_prefetch=2,
        grid=(npt + 1,),
        in_specs=[pl.BlockSpec((None, 1, MOE_TOPK * tm), lambda i, lo, hi: (jnp.minimum(i, last), 0, 0),
                               memory_space=pltpu.SMEM),
                  pl.BlockSpec((None, 1, MOE_TOPK * ts), lambda i, lo, hi: (0, 0, 0), memory_space=pltpu.SMEM),
                  pl.BlockSpec((tm * ROW_TILES, LANES), lambda i, lo, hi: (jnp.minimum(i, last), 0)),
                  pl.BlockSpec((ts * ROW_TILES, LANES), lambda i, lo, hi: (0, 0))],
        out_specs=pl.BlockSpec(memory_space=pl.ANY),
        scratch_shapes=[pltpu.VMEM((ZERO_ROWS * ROW_TILES, LANES), F32), pltpu.SemaphoreType.DMA(()),
                        pltpu.SemaphoreType.DMA(())],
    )
    return pl.pallas_call(
        functools.partial(_scatter_kernel, npt),
        grid_spec=grid_spec,
        out_shape=jax.ShapeDtypeStruct((n_rows * ROW_TILES, LANES), F32),
        compiler_params=pltpu.CompilerParams(dimension_semantics=("arbitrary",),
                                             vmem_limit_bytes=VMEM_LIMIT, has_side_effects=True),
        name="moe_scatter",
    )(zero_lo, zero_hi, slots_p, slots_s, h2t_p, h2t_s)


def _load_expert(e, w1_hbm, w3_hbm, w2_hbm, w1_ref, w3_ref, w2_ref, up_stage, down_stage, sem):
    tf = TF_GMM
    nck = w1_ref.shape[1] // tf
    jobs = []
    for k in range(nck):
        cols = slice(k * tf, (k + 1) * tf)
        for src, dst in ((w1_hbm, w1_ref), (w3_hbm, w3_ref)):
            slot = len(jobs) % 2
            jobs.append((pltpu.make_async_copy(src.at[e, :, cols], up_stage.at[slot], sem.at[slot]),
                         up_stage.at[slot], dst.at[:, cols]))
    for k in range(nck):
        rows = slice(k * tf, (k + 1) * tf)
        slot = len(jobs) % 2
        jobs.append((pltpu.make_async_copy(w2_hbm.at[e, rows, :], down_stage.at[slot], sem.at[slot]),
                     down_stage.at[slot], w2_ref.at[rows, :]))
    jobs[0][0].start()
    for k, (copy, stage, dst) in enumerate(jobs):
        if k + 1 < len(jobs):
            jobs[k + 1][0].start()
        copy.wait()
        dst[...] = stage[...].astype(BF16)


def _gmm_kernel(exp_ref, nv_ref, x_ref, w1_hbm, w3_hbm, w2_hbm, o_ref, g_ref, w1_ref, w3_ref, w2_ref,
                up_stage, down_stage, sem):
    tg = g_ref.shape[0]
    m = pl.program_id(0)
    live = m < nv_ref[0]
    e = exp_ref[m]

    @pl.when(live & ((m == 0) | (e != exp_ref[jnp.maximum(m - 1, 0)])))
    def _():
        _load_expert(e, w1_hbm, w3_hbm, w2_hbm, w1_ref, w3_ref, w2_ref, up_stage, down_stage, sem)

    @pl.when(live)
    def _():
        x = _tiles_to_rows(x_ref, tg).astype(BF16)
        _swiglu_hidden(x, w1_ref, w3_ref, g_ref, TF_GMM)
        _rows_to_tiles(o_ref, _dot(g_ref[...], w2_ref[...]))

    @pl.when(jnp.logical_not(live))
    def _():
        o_ref[...] = jnp.zeros_like(o_ref)


def _gmm(xs, w1, w3, w2, tile_exp, n_valid):
    d = D_MODEL
    tg = TM_GMM
    tf = TF_GMM
    dfe = w1.shape[2]
    n_tiles = tile_exp.shape[0]
    hbm = pl.BlockSpec(memory_space=pl.ANY)
    grid_spec = pltpu.PrefetchScalarGridSpec(
        num_scalar_prefetch=2,
        grid=(n_tiles,),
        in_specs=[pl.BlockSpec((tg * ROW_TILES, LANES), lambda m, ex, nv: (m, 0)), hbm, hbm, hbm],
        out_specs=pl.BlockSpec((tg * ROW_TILES, LANES), lambda m, ex, nv: (m, 0)),
        scratch_shapes=[pltpu.VMEM((tg, dfe), BF16), pltpu.VMEM((d, dfe), BF16), pltpu.VMEM((d, dfe), BF16),
                        pltpu.VMEM((dfe, d), BF16), pltpu.VMEM((2, d, tf), F32), pltpu.VMEM((2, tf, d), F32),
                        pltpu.SemaphoreType.DMA((2,))],
    )
    return pl.pallas_call(
        _gmm_kernel,
        grid_spec=grid_spec,
        out_shape=jax.ShapeDtypeStruct(xs.shape, F32),
        compiler_params=_params(1),
        name="moe_experts",
    )(tile_exp, n_valid, xs, w1, w3, w2)


def _combine_kernel(alpha, slots_ref, next_slots_ref, ys_ref, y_ref, meta_ref, gf_ref, lng_ref, lnb_ref,
                    o_ref, buf_ref, sem):
    tm = y_ref.shape[0]
    i = pl.program_id(0)
    cur = i % 2

    def gather(sl_ref, slot):
        def issue(r, carry):
            for c in range(MOE_TOPK):
                _row_copy(ys_ref, sl_ref[0, c * tm + r], buf_ref.at[slot, c], r, sem.at[slot]).start()
            return carry

        lax.fori_loop(0, tm, issue, 0)

    @pl.when(i == 0)
    def _():
        gather(slots_ref, 0)

    for slot in range(2):
        @pl.when((i + 1 < pl.num_programs(0)) & (cur != slot))
        def _(slot=slot):
            gather(next_slots_ref, slot)

    for c in range(MOE_TOPK):
        _row_copy(ys_ref, 0, buf_ref.at[cur, c], 0, sem.at[cur], n=tm).wait()
    meta = meta_ref[...]
    f = jnp.zeros((tm, D_MODEL), F32)
    for c in range(MOE_TOPK):
        f = f + _tiles_to_rows(buf_ref.at[cur, c], tm) * meta[:, META_GATE + c:META_GATE + c + 1]
    o_ref[...] = _layernorm(alpha * y_ref[...] + (1.0 + gf_ref[...]) * f,
                            lng_ref[1:2, :], lnb_ref[1:2, :])


def _combine(ys, y3, slots3, meta, gf, ln_g, ln_b, tm, alpha):
    t, d = y3.shape
    last = t // tm - 1

    def slot_spec(ahead):
        return pl.BlockSpec((None, 1, MOE_TOPK * tm), lambda i: (jnp.minimum(i + ahead, last), 0, 0),
                            memory_space=pltpu.SMEM)

    return pl.pallas_call(
        functools.partial(_combine_kernel, alpha),
        grid=(t // tm,),
        in_specs=[slot_spec(0), slot_spec(1),
                  pl.BlockSpec(memory_space=pl.ANY),
                  pl.BlockSpec((tm, d), lambda i: (i, 0)),
                  pl.BlockSpec((tm, LANES), lambda i: (i, 0)),
                  gf[1], _resident((2, d)), _resident((2, d))],
        out_specs=pl.BlockSpec((tm, d), lambda i: (i, 0)),
        out_shape=jax.ShapeDtypeStruct((t, d), F32),
        scratch_shapes=[pltpu.VMEM((2, MOE_TOPK, tm * ROW_TILES, LANES), F32), pltpu.SemaphoreType.DMA((2,))],
        compiler_params=_params(1),
        name="moe_combine",
    )(slots3, slots3, ys, y3, meta, gf[0], ln_g, ln_b)


def _tile_slots(slots, tm):
    k, t = slots.shape
    return slots.reshape(k, t // tm, tm).transpose(1, 0, 2).reshape(t // tm, 1, k * tm)


def kernel(x_prompt, x_sample, cache_k, cache_v, state_conv, page_table, c_prompt, c_sample,
           ada_w, ada_b, ln_g, ln_b, attn_w_qkv, attn_w_o,
           conv_w_pw1, conv_b_pw1, conv_w_dw, conv_b_dw, conv_ln_g, conv_ln_b, conv_w_pw2, conv_b_pw2,
           ffn_w1, ffn_w3, ffn_w2, moe_w_router, moe_b_router, moe_w1, moe_w3, moe_w2):
    b, s, d = x_prompt.shape
    db, ds, _ = x_sample.shape
    depth = ada_w.shape[0]
    assert d == D_MODEL and ds == 1 and depth == 2
    assert page_table.shape[1] * PAGE_SIZE % MOBA_BLOCK == 0
    alpha = (2 * depth) ** 0.25
    tp = b * s
    t_all = tp + db
    assert tp % TM_ROUTE == 0 and tp % TM_COMBINE == 0

    c_all = jnp.concatenate([c_sample, c_prompt], axis=0)
    mods = _adaln(c_all, ada_w, ada_b)
    mods = _Mods(mods.reshape(depth * 6, db + b, d), db)

    def pmod(layer, k, tm):
        tps = s // tm
        return mods.prompt(layer * 6 + k, lambda i: i // tps)

    def smod(layer, k, rows=None, block_of_step=lambda i: 0):
        return mods.sample(layer * 6 + k, db if rows is None else rows, block_of_step)

    xp = x_prompt.reshape(tp, d)
    xs_ = x_sample.reshape(db, d)

    wqkv = attn_w_qkv[0].astype(BF16)
    qp, kp, vp = _qkv(xp, pmod(0, 0, TM_QKV), pmod(0, 1, TM_QKV), wqkv, TM_QKV)
    qs, ks, vs = _qkv(xs_, smod(0, 0), smod(0, 1), wqkv, db)
    attn_p, kt_p, vt_p = _moba_prompt(qp.reshape(b, s, d), kp.reshape(b, s, d), vp.reshape(b, s, d))
    attn_p = attn_p.reshape(tp, d)
    attn_s = _moba_sample(qs, ks, vs, cache_k, cache_v, 0, page_table)

    assert ffn_w1.shape[2] % FFN_CHUNK == 0 and moe_w1.shape[3] % TF_GMM == 0
    wo = attn_w_o[0].astype(BF16)
    ffn_w = (ffn_w1[0].astype(BF16), ffn_w3[0].astype(BF16), ffn_w2[0].astype(BF16))
    y_p = _post_attn(attn_p, xp, pmod(0, 2, TM_POST), pmod(0, 3, TM_POST), pmod(0, 4, TM_POST),
                     pmod(0, 5, TM_POST), ln_g[0], ln_b[0], wo, *ffn_w, TM_POST, alpha)
    y_s = _post_attn(attn_s, xs_, smod(0, 2), smod(0, 3), smod(0, 4), smod(0, 5),
                     ln_g[0], ln_b[0], wo, *ffn_w, db, alpha)

    wr = jnp.pad(moe_w_router[0], ((0, 0), (0, LANES - N_EXPERTS)))
    wr_hi = wr.astype(BF16)
    wr = jnp.stack([wr_hi, (wr - wr_hi.astype(F32)).astype(BF16)])
    br = jnp.pad(moe_b_router[0], (0, LANES - N_EXPERTS)).reshape(1, LANES)
    conv_weights = (ln_g[1], ln_b[1], conv_w_pw1[0].astype(BF16), conv_b_pw1[0].reshape(1, 2 * d),
                    conv_w_dw[0], conv_b_dw[0].reshape(1, d), conv_ln_g[0].reshape(1, d),
                    conv_ln_b[0].reshape(1, d), conv_w_pw2[0].astype(BF16), conv_b_pw2[0].reshape(1, d),
                    wr, br)
    pm = [mods.prompt(6 + k, lambda bi, i: bi) for k in range(5)]
    y3_p, h2t_p, lg_p, tail = _conv_prompt(y_p, pm, conv_weights, b, s, alpha)
    sm = [smod(1, k, 32, lambda i: i) for k in range(5)]
    hist_t = jnp.transpose(state_conv[0], (1, 0, 2))
    y3_s, h2t_s, lg_s, u_s = _conv_sample(y_s, sm, conv_weights, hist_t, alpha)

    meta_p, meta_pt, meta_s, meta_st, cnt = _route(lg_p, lg_s)
    counts = cnt[0, 0:N_EXPERTS].astype(jnp.int32)
    padded = (counts + TM_GMM - 1) // TM_GMM * TM_GMM
    pad_end = jnp.cumsum(padded)
    pad_start = pad_end - padded
    n_tiles = -(-t_all * MOE_TOPK // TM_GMM) + N_EXPERTS
    n_rows = n_tiles * TM_GMM

    def slots_of(meta_t):
        e = meta_t[META_EXPERT:META_EXPERT + MOE_TOPK].astype(jnp.int32)
        slot = meta_t[META_RANK:META_RANK + MOE_TOPK].astype(jnp.int32)
        for x in range(N_EXPERTS):
            slot = slot + jnp.where(e == x, pad_start[x], 0)
        return slot

    slots_p, slots_s = slots_of(meta_pt), slots_of(meta_st)
    zero_lo = pad_start + counts
    zero_hi = jnp.concatenate([pad_start[1:], jnp.full((1,), n_rows, jnp.int32)])
    xs_buf = _scatter(h2t_p, h2t_s, _tile_slots(slots_p, TM_ROUTE), _tile_slots(slots_s, db),
                      zero_lo.astype(jnp.int32), zero_hi.astype(jnp.int32), n_rows)

    tile_ids = jnp.arange(n_tiles, dtype=jnp.int32)
    tile_exp = jnp.minimum(jnp.sum((pad_end // TM_GMM)[None, :] <= tile_ids[:, None], axis=1),
                           N_EXPERTS - 1).astype(jnp.int32)
    n_valid = (pad_end[-1] // TM_GMM).reshape(1).astype(jnp.int32)
    ys_buf = _gmm(xs_buf, moe_w1[0], moe_w3[0], moe_w2[0], tile_exp, n_valid)

    out_p = _combine(ys_buf, y3_p, _tile_slots(slots_p, TM_COMBINE), meta_p,
                     pmod(1, 5, TM_COMBINE), ln_g[1], ln_b[1], TM_COMBINE, alpha)
    out_s = _combine(ys_buf, y3_s, _tile_slots(slots_s, db), meta_s,
                     smod(1, 5), ln_g[1], ln_b[1], db, alpha)

    nh = CONV_WIDTH - 1
    new_conv_p = tail[:, HIST_PAD - nh:, :][None]
    new_conv_s = jnp.transpose(jnp.concatenate([hist_t[1:], u_s[None]], axis=0), (1, 0, 2))[None]
    hshape = (N_HEADS, HEAD_DIM)

    def seq_major(t):
        return jnp.transpose(t.reshape(1, b, *hshape, s), (0, 1, 4, 2, 3))

    return (out_p.reshape(b, s, d), out_s.reshape(db, 1, d), seq_major(kt_p), seq_major(vt_p),
            ks.reshape(1, db, 1, *hshape), vs.reshape(1, db, 1, *hshape),
            new_conv_p, new_conv_s)
```

```python
import functools
import math

import jax
import jax.numpy as jnp
from jax import lax
from jax.experimental import pallas as pl
from jax.experimental.pallas import tpu as pltpu

F32 = jnp.float32
BF16 = jnp.bfloat16
HIGHEST = lax.Precision.HIGHEST
NEG_INF = float("-inf")
LOG2E = math.log2(math.e)

SUBLANES = 8
LANES = 128

D_MODEL = 1024
N_HEADS = 16
HEAD_DIM = D_MODEL // N_HEADS
HEADS_PER_STEP = 8
MOBA_BLOCK = 256
MOBA_TOPK = 3
PAGE_SIZE = 128
CONV_WIDTH = 31
HIST_PAD = 32
N_EXPERTS = 8
MOE_TOPK = 2
LN_EPS = 1e-5
ROW_TILES = D_MODEL // LANES

TM_QKV = 512
TM_POST = 512
TM_CONV = 512
TM_ROUTE = 512
TM_GMM = 512
TM_COMBINE = 512
TF_GMM = 512
FFN_CHUNK = 256
ZERO_ROWS = 64
VMEM_LIMIT = 56 * 1024 * 1024

META_EXPERT, META_GATE, META_RANK = 0, 2, 4

NT_DIMS = (((1,), (1,)), ((), ()))


def _dot(a, b, precision=None):
    return jnp.dot(a, b, preferred_element_type=F32, precision=precision)


def _dot_nt(a, b, precision=None):
    return lax.dot_general(a, b, NT_DIMS, preferred_element_type=F32, precision=precision)


def _silu(x):
    return x * jax.nn.sigmoid(x)


def _layernorm(z, g, b):
    mu = jnp.mean(z, axis=-1, keepdims=True)
    zc = z - mu
    var = jnp.mean(zc * zc, axis=-1, keepdims=True)
    return zc * lax.rsqrt(var + LN_EPS) * g + b


def _resident(shape):
    nd = len(shape)
    return pl.BlockSpec(shape, lambda *_: (0,) * nd, pipeline_mode=pl.Buffered(1))


def _params(n_grid_dims):
    return pltpu.CompilerParams(
        dimension_semantics=("arbitrary",) * n_grid_dims, vmem_limit_bytes=VMEM_LIMIT)


def _rows_to_tiles(ref, x):
    t = x.shape[0]
    for s in range(ROW_TILES):
        ref[pl.ds(s, t, stride=ROW_TILES), :] = x[:, s * LANES:(s + 1) * LANES]


def _tiles_to_rows(ref, t):
    return jnp.concatenate([ref[pl.ds(s, t, stride=ROW_TILES), :] for s in range(ROW_TILES)], axis=1)


def _ada_kernel(c_ref, w_ref, b_ref, o_ref):
    c = c_ref[...]
    o_ref[...] = _dot(_silu(c).astype(BF16), w_ref[...].astype(BF16)) + b_ref[...]


def _adaln(c_all, ada_w, ada_b):
    depth = ada_w.shape[0]
    n = c_all.shape[0]
    d = D_MODEL
    return pl.pallas_call(
        _ada_kernel,
        grid=(depth, 6),
        in_specs=[
            pl.BlockSpec((n, d), lambda l, k: (0, 0)),
            pl.BlockSpec((None, d, d), lambda l, k: (l, 0, k)),
            pl.BlockSpec((None, None, 1, d), lambda l, k: (l, k, 0, 0)),
        ],
        out_specs=pl.BlockSpec((None, None, n, d), lambda l, k: (l, k, 0, 0)),
        out_shape=jax.ShapeDtypeStruct((depth, 6, n, d), F32),
        compiler_params=_params(2),
        name="adaln",
    )(c_all, ada_w, ada_b.reshape(depth, 6, 1, d))


class _Mods:
    def __init__(self, mods, n_sample):
        self.m3 = mods
        self.m4 = mods[:, n_sample:, :].reshape(mods.shape[0], mods.shape[1] - n_sample, 1, D_MODEL)

    def prompt(self, lk, seq_of_step):
        return self.m4, pl.BlockSpec((None, None, 1, D_MODEL), lambda *g: (lk, seq_of_step(*g), 0, 0))

    def sample(self, lk, rows, block_of_step):
        return self.m3, pl.BlockSpec((None, rows, D_MODEL), lambda *g: (lk, block_of_step(*g), 0))


def _qkv_kernel(x_ref, sh_ref, sc_ref, w_ref, q_ref, k_ref, v_ref):
    d = D_MODEL
    h = (x_ref[...] * (1.0 + sc_ref[...]) + sh_ref[...]).astype(BF16)
    q_ref[...] = _dot(h, w_ref[:, 0:d])
    k_ref[...] = _dot(h, w_ref[:, d:2 * d])
    v_ref[...] = _dot(h, w_ref[:, 2 * d:3 * d])


def _qkv(x, sh, sc, w_bf16, tm):
    t = x.shape[0]
    d = D_MODEL
    row = pl.BlockSpec((tm, d), lambda i: (i, 0))
    out = jax.ShapeDtypeStruct((t, d), F32)
    return pl.pallas_call(
        _qkv_kernel,
        grid=(t // tm,),
        in_specs=[row, sh[1], sc[1], _resident((d, 3 * d))],
        out_specs=[row, row, row],
        out_shape=[out, out, out],
        compiler_params=_params(1),
        name="qkv",
    )(x, sh[0], sc[0], w_bf16)


def _select_blocks(gate_t, n_valid):
    nb = gate_t.shape[0]
    blk = lax.broadcasted_iota(jnp.int32, gate_t.shape, 0)
    valid = blk < n_valid
    rows = []
    for n in range(nb):
        gn = gate_t[n:n + 1, :]
        beats = ((gate_t > gn) | ((gate_t == gn) & (blk < n))) & valid
        rows.append(jnp.sum(beats.astype(F32), axis=0, keepdims=True))
    cnt = jnp.concatenate(rows, axis=0)
    return (valid & (cnt < MOBA_TOPK)).astype(F32)


def _attn_kernel(q_ref, k_ref, v_ref, bias_ref, o_ref, kt_ref, vt_ref, kb_ref, ve_ref, km_ref, mb_ref, acc_ref):
    blk = MOBA_BLOCK
    s_len = k_ref.shape[0]
    nb = s_len // blk
    nh = HEADS_PER_STEP
    cb = pl.program_id(1)
    own = pl.program_id(2)

    @pl.when(own == 0)
    def _():
        kb_ref[...] = k_ref[...].astype(BF16)
        lane_s = lax.broadcasted_iota(jnp.int32, (s_len, LANES), 1)
        lane_k = lax.broadcasted_iota(jnp.int32, (nb, LANES), 1) // HEAD_DIM
        for pp in range(nh // 2):
            cols = slice(pp * LANES, (pp + 1) * LANES)
            kt_ref[cols, :] = jnp.transpose(k_ref[:, cols])
            vt_ref[cols, :] = jnp.transpose(v_ref[:, cols])
            rows = [jnp.mean(k_ref[n * blk:(n + 1) * blk, cols], axis=0, keepdims=True) for n in range(nb)]
            km = jnp.concatenate(rows, axis=0)
            km_ref[pp] = jnp.concatenate([jnp.where(lane_k == 0, km, 0.0), jnp.where(lane_k == 1, km, 0.0)], axis=0)
            v = v_ref[:, cols]
            ve_ref[2 * pp] = jnp.where(lane_s < HEAD_DIM, v, jnp.where(lane_s == HEAD_DIM, 1.0, 0.0)).astype(BF16)
            ve_ref[2 * pp + 1] = jnp.where(lane_s >= HEAD_DIM, v, jnp.where(lane_s == 0, 1.0, 0.0)).astype(BF16)

    qi = lax.broadcasted_iota(jnp.int32, (blk, blk), 0)
    ki = lax.broadcasted_iota(jnp.int32, (blk, blk), 1)
    causal = qi >= ki
    eye = (qi == ki).astype(BF16)
    lane = lax.broadcasted_iota(jnp.int32, (blk, LANES), 1)
    lane_head = lane // HEAD_DIM
    blk_off = (own - lane % nb).astype(F32) * float(blk)
    own_start = pl.multiple_of(own * blk, blk)

    qh = []
    for pp in range(nh // 2):
        q = q_ref[:, pp * LANES:(pp + 1) * LANES]
        qs_all = (q * (HEAD_DIM ** -0.5 * LOG2E)).astype(BF16)
        gates = _dot_nt(km_ref[pp], q, precision=HIGHEST)
        sel_t = jnp.concatenate([_select_blocks(gates[hh * nb:(hh + 1) * nb, :], own) for hh in range(2)]
                                + [jnp.zeros((LANES - 2 * nb, blk), F32)], axis=0).astype(BF16)
        sel = _dot_nt(eye, sel_t)
        head_no = (cb * nh + 2 * pp + 1 + (lane >= nb).astype(jnp.int32)).astype(F32)
        slope2 = jnp.exp2(head_no * (-8.0 / N_HEADS)) * LOG2E
        mb_ref[pp] = jnp.where(sel > 0.5, -slope2 * blk_off, NEG_INF)
        for hh in range(2):
            qh.append(jnp.where(lane_head == hh, qs_all, jnp.zeros_like(qs_all)))

    ms = []
    for h in range(nh):
        pp = h // 2
        kd = kb_ref[pl.ds(own_start, blk), pp * LANES:(pp + 1) * LANES]
        s = jnp.where(causal, _dot_nt(qh[h], kd) + bias_ref[h], NEG_INF)
        m = jnp.max(s, axis=1, keepdims=True)
        p = jnp.exp2(s - m)
        acc_ref[h] = _dot(p.astype(BF16), ve_ref[h, pl.ds(own_start, blk), :])
        ms.append(m)

    def body(j, carry):
        start = pl.multiple_of(j * blk, blk)
        out = []
        for h in range(nh):
            pp, hh = h // 2, h % 2
            kj = kb_ref[pl.ds(start, blk), pp * LANES:(pp + 1) * LANES]
            m = carry[h]
            s = _dot_nt(qh[h], kj) + bias_ref[h]
            mb = jnp.sum(jnp.where(lane == j + hh * nb, mb_ref[pp], 0.0), axis=1, keepdims=True)
            m_new = jnp.maximum(m, jnp.max(s, axis=1, keepdims=True) + mb)
            p = jnp.exp2(s + (mb - m_new))
            a = jnp.exp2(m - m_new)
            acc_ref[h] = a * acc_ref[h] + _dot(p.astype(BF16), ve_ref[h, pl.ds(start, blk), :])
            out.append(m_new)
        return tuple(out)

    lax.fori_loop(0, own, body, tuple(ms))
    for pp in range(nh // 2):
        acc0 = acc_ref[2 * pp]
        acc1 = acc_ref[2 * pp + 1]
        out = jnp.where(lane < HEAD_DIM, acc0 / acc0[:, HEAD_DIM:HEAD_DIM + 1], acc1 / acc1[:, 0:1])
        o_ref[:, pp * LANES:(pp + 1) * LANES] = out.astype(o_ref.dtype)


def _alibi_bias():
    slopes = 2.0 ** (-8.0 * jnp.arange(1, N_HEADS + 1, dtype=F32) / N_HEADS)
    qi = jnp.arange(MOBA_BLOCK, dtype=F32)[:, None]
    ki = jnp.arange(MOBA_BLOCK, dtype=F32)[None, :]
    return -(slopes * LOG2E)[:, None, None] * (qi - ki)[None]


def _moba_prompt(q, k, v):
    b, s, d = q.shape
    nb = s // MOBA_BLOCK
    nh = HEADS_PER_STEP
    w = nh * HEAD_DIM
    qspec = pl.BlockSpec((None, MOBA_BLOCK, w), lambda bi, cb, qb: (bi, qb, cb))
    kvspec = pl.BlockSpec((None, s, w), lambda bi, cb, qb: (bi, 0, cb))
    bspec = pl.BlockSpec((nh, MOBA_BLOCK, MOBA_BLOCK), lambda bi, cb, qb: (cb, 0, 0))
    tspec = pl.BlockSpec((None, w, s), lambda bi, cb, qb: (bi, cb, 0))
    return pl.pallas_call(
        _attn_kernel,
        grid=(b, d // w, nb),
        in_specs=[qspec, kvspec, kvspec, bspec],
        out_specs=[qspec, tspec, tspec],
        out_shape=[jax.ShapeDtypeStruct((b, s, d), BF16), jax.ShapeDtypeStruct((b, d, s), F32),
                   jax.ShapeDtypeStruct((b, d, s), F32)],
        scratch_shapes=[pltpu.VMEM((s, w), BF16), pltpu.VMEM((nh, s, LANES), BF16),
                        pltpu.VMEM((nh // 2, 2 * nb, LANES), F32),
                        pltpu.VMEM((nh // 2, MOBA_BLOCK, LANES), F32),
                        pltpu.VMEM((nh, MOBA_BLOCK, LANES), F32)],
        compiler_params=_params(3),
        name="moba_prompt",
    )(q, k, v, _alibi_bias())


def _attn_sample_probs_kernel(n_pages, pt_ref, q_ref, qt_ref, kn_ref, *refs):
    del pt_ref
    kp = refs[:n_pages]
    p_ref, misc_ref, s_ref = refs[n_pages:]
    past = n_pages * PAGE_SIZE
    nb = past // MOBA_BLOCK
    scale = HEAD_DIM ** -0.5

    q = q_ref[...]
    qt = qt_ref[...]
    for h in range(N_HEADS):
        qb = jnp.broadcast_to(qt[:, h:h + 1], (HEAD_DIM, PAGE_SIZE))
        for p in range(n_pages):
            s_ref[h:h + 1, p * PAGE_SIZE:(p + 1) * PAGE_SIZE] = jnp.sum(kp[p][h] * qb, axis=0, keepdims=True)
    raw = s_ref[...]
    gates = [jnp.sum(raw[:, n * MOBA_BLOCK:(n + 1) * MOBA_BLOCK], axis=1, keepdims=True) * (1.0 / MOBA_BLOCK)
             for n in range(nb)]
    lane_blk = lax.broadcasted_iota(jnp.int32, (N_HEADS, past), 1) // MOBA_BLOCK
    keep = jnp.zeros((N_HEADS, past), jnp.bool_)
    n_kept = jnp.zeros((N_HEADS, 1), F32)
    kept_ids = [jnp.zeros((N_HEADS, 1), F32) for _ in range(MOBA_TOPK)]
    for n in range(nb):
        cnt = jnp.zeros((N_HEADS, 1), F32)
        for j in range(nb):
            if j != n:
                beats = (gates[j] > gates[n]) | ((gates[j] == gates[n]) & (j < n))
                cnt = cnt + beats.astype(F32)
        keep_n = cnt < MOBA_TOPK
        keep = keep | ((lane_blk == n) & keep_n)
        for j in range(MOBA_TOPK):
            kept_ids[j] = jnp.where(keep_n & (n_kept == j), float(n), kept_ids[j])
        n_kept = n_kept + keep_n.astype(F32)
    head = lax.broadcasted_iota(jnp.int32, (N_HEADS, 1), 0)
    slope = jnp.exp2((head + 1).astype(F32) * (-8.0 / N_HEADS))
    kpos = lax.broadcasted_iota(jnp.int32, (1, past), 1).astype(F32)
    s = jnp.where(keep, raw * scale - slope * (float(past) - kpos), NEG_INF)
    s_own = jnp.sum(q * kn_ref[...], axis=1, keepdims=True) * scale
    m = jnp.maximum(jnp.max(s, axis=1, keepdims=True), s_own)
    e = jnp.exp(s - m)
    w_own = jnp.exp(s_own - m)
    inv = 1.0 / (jnp.sum(e, axis=1, keepdims=True) + w_own)
    p = e * inv
    for pg in range(n_pages):
        p_ref[:, pg, :] = p[:, pg * PAGE_SIZE:(pg + 1) * PAGE_SIZE]
    lane = lax.broadcasted_iota(jnp.int32, (N_HEADS, LANES), 1)
    misc = jnp.where(lane == MOBA_TOPK, w_own * inv, 0.0)
    for j in range(MOBA_TOPK):
        misc = jnp.where(lane == j, kept_ids[j], misc)
    misc_ref[...] = misc


def _attn_sample_values_kernel(n_pages, layer, pt_ref, kept_ref, p_ref, misc_ref, vnt_ref, cv_ref, o_ref,
                               vbuf_ref, sem):
    ppb = MOBA_BLOCK // PAGE_SIZE
    per_head = MOBA_TOPK * ppb
    n_chunks = N_HEADS * per_head
    i = pl.program_id(0)
    cur = i % 2

    def chunk_copy(seq, c, slot):
        h, j, pg = c // per_head, (c % per_head) // ppb, c % ppb
        page = pt_ref[seq * n_pages + kept_ref[(seq * N_HEADS + h) * MOBA_TOPK + j] * ppb + pg]
        return pltpu.make_async_copy(cv_ref.at[layer, page, h], vbuf_ref.at[slot, c], sem.at[slot])

    def fetch(seq, slot):
        for c in range(n_chunks):
            chunk_copy(seq, c, slot).start()

    @pl.when(i == 0)
    def _():
        fetch(0, 0)

    for slot in range(2):
        @pl.when((i + 1 < pl.num_programs(0)) & (cur != slot))
        def _(slot=slot):
            fetch(i + 1, slot)

    for slot in range(2):
        @pl.when(cur == slot)
        def _(slot=slot):
            for c in range(n_chunks):
                chunk_copy(i, c, slot).wait()

    misc = misc_ref[...]
    lane_h = lax.broadcasted_iota(jnp.int32, (HEAD_DIM, N_HEADS), 1)
    out_t = jnp.zeros((HEAD_DIM, N_HEADS), F32)
    for h in range(N_HEADS):
        acc = jnp.zeros((HEAD_DIM, PAGE_SIZE), F32)
        for j in range(MOBA_TOPK):
            blk = kept_ref[(i * N_HEADS + h) * MOBA_TOPK + j]
            for pg in range(ppb):
                w = p_ref[h, pl.ds(blk * ppb + pg, 1), :]
                acc = acc + vbuf_ref[cur, h * per_head + j * ppb + pg] * w
        w_own = misc[h:h + 1, MOBA_TOPK:MOBA_TOPK + 1]
        col = jnp.sum(acc, axis=1, keepdims=True) + w_own * vnt_ref[:, h:h + 1]
        out_t = jnp.where(lane_h == h, col, out_t)
    o_ref[...] = out_t


def _moba_sample(q, k_new, v_new, cache_k, cache_v, layer, page_table):
    n_seq, n_pages = page_table.shape
    assert n_pages * PAGE_SIZE // MOBA_BLOCK >= MOBA_TOPK
    hd = (N_HEADS, HEAD_DIM)
    dh = (HEAD_DIM, N_HEADS)
    ck = jnp.transpose(cache_k, (0, 1, 3, 4, 2))
    cv = jnp.transpose(cache_v, (0, 1, 3, 4, 2))
    pt = page_table.reshape(-1).astype(jnp.int32)
    q3 = q.reshape((n_seq,) + hd)
    hd_spec = pl.BlockSpec((None,) + hd, lambda i, *_: (i, 0, 0))
    dh_spec = pl.BlockSpec((None,) + dh, lambda i, *_: (i, 0, 0))
    p_spec = pl.BlockSpec((None, N_HEADS, n_pages, PAGE_SIZE), lambda i, *_: (i, 0, 0, 0))
    misc_spec = pl.BlockSpec((None, N_HEADS, LANES), lambda i, *_: (i, 0, 0))

    def page_spec(p):
        return pl.BlockSpec((None, None) + hd + (PAGE_SIZE,),
                            lambda i, pt: (layer, pt[i * n_pages + p], 0, 0, 0))

    probs, misc = pl.pallas_call(
        functools.partial(_attn_sample_probs_kernel, n_pages),
        grid_spec=pltpu.PrefetchScalarGridSpec(
            num_scalar_prefetch=1,
            grid=(n_seq,),
            in_specs=[hd_spec, dh_spec, hd_spec] + [page_spec(p) for p in range(n_pages)],
            out_specs=[p_spec, misc_spec],
            scratch_shapes=[pltpu.VMEM((N_HEADS, n_pages * PAGE_SIZE), F32)],
        ),
        out_shape=[jax.ShapeDtypeStruct((n_seq, N_HEADS, n_pages, PAGE_SIZE), F32),
                   jax.ShapeDtypeStruct((n_seq, N_HEADS, LANES), F32)],
        compiler_params=_params(1),
        name="moba_sample_probs",
    )(pt, q3, q3.transpose(0, 2, 1), k_new.reshape((n_seq,) + hd), *([ck] * n_pages))

    kept = misc[:, :, 0:MOBA_TOPK].astype(jnp.int32).reshape(-1)
    n_chunks = N_HEADS * MOBA_TOPK * (MOBA_BLOCK // PAGE_SIZE)
    out_t = pl.pallas_call(
        functools.partial(_attn_sample_values_kernel, n_pages, layer),
        grid_spec=pltpu.PrefetchScalarGridSpec(
            num_scalar_prefetch=2,
            grid=(n_seq,),
            in_specs=[p_spec, misc_spec, dh_spec, pl.BlockSpec(memory_space=pl.ANY)],
            out_specs=dh_spec,
            scratch_shapes=[pltpu.VMEM((2, n_chunks, HEAD_DIM, PAGE_SIZE), F32), pltpu.SemaphoreType.DMA((2,))],
        ),
        out_shape=jax.ShapeDtypeStruct((n_seq,) + dh, F32),
        compiler_params=_params(1),
        name="moba_sample_values",
    )(pt, kept, probs, misc, v_new.reshape((n_seq,) + hd).transpose(0, 2, 1), cv)
    return out_t.transpose(0, 2, 1).reshape(n_seq, D_MODEL).astype(BF16)


def _post_kernel(alpha, a_ref, x_ref, gm_ref, shf_ref, scf_ref, gf_ref, lng_ref, lnb_ref,
                 wo_ref, w1_ref, w3_ref, w2_ref, o_ref, g_ref):
    m = _dot(a_ref[...], wo_ref[...])
    y1 = _layernorm(alpha * x_ref[...] + (1.0 + gm_ref[...]) * m, lng_ref[0:1, :], lnb_ref[0:1, :])
    h = (y1 * (1.0 + scf_ref[...]) + shf_ref[...]).astype(BF16)
    _swiglu_hidden(h, w1_ref, w3_ref, g_ref, FFN_CHUNK)
    f = _dot(g_ref[...], w2_ref[...])
    o_ref[...] = _layernorm(alpha * y1 + (1.0 + gf_ref[...]) * f, lng_ref[1:2, :], lnb_ref[1:2, :])


def _swiglu_hidden(h, w1_ref, w3_ref, g_ref, chunk):
    for c in range(w1_ref.shape[1] // chunk):
        cols = slice(c * chunk, (c + 1) * chunk)
        t1 = _dot(h, w1_ref[:, cols])
        t3 = _dot(h, w3_ref[:, cols])
        g_ref[:, cols] = (_silu(t1) * t3).astype(BF16)


def _post_attn(attn, x, gm, shf, scf, gf, ln_g, ln_b, wo, w1, w3, w2, tm, alpha):
    t = x.shape[0]
    d = D_MODEL
    row = pl.BlockSpec((tm, d), lambda i: (i, 0))
    return pl.pallas_call(
        functools.partial(_post_kernel, alpha),
        grid=(t // tm,),
        in_specs=[row, row, gm[1], shf[1], scf[1], gf[1], _resident((2, d)), _resident((2, d)),
                  _resident(wo.shape), _resident(w1.shape), _resident(w3.shape), _resident(w2.shape)],
        out_specs=row,
        out_shape=jax.ShapeDtypeStruct((t, d), F32),
        scratch_shapes=[pltpu.VMEM((tm, w1.shape[1]), BF16)],
        compiler_params=_params(1),
        name="post_attn_ffn",
    )(attn, x, gm[0], shf[0], scf[0], gf[0], ln_g, ln_b, wo, w1, w3, w2)


def _glu(y, shm_ref, scm_ref, wpw1_ref, bpw1_ref):
    d = D_MODEL
    h = (y * (1.0 + scm_ref[...]) + shm_ref[...]).astype(BF16)
    a = _dot(h, wpw1_ref[:, 0:d]) + bpw1_ref[:, 0:d]
    g = _dot(h, wpw1_ref[:, d:2 * d]) + bpw1_ref[:, d:2 * d]
    return a * jax.nn.sigmoid(g)


def _conv_tail(alpha, y, conv, gm_ref, shf_ref, scf_ref, lng_ref, lnb_ref, clg_ref, clb_ref,
               wpw2_ref, bpw2_ref, wr_ref, br_ref, y3_ref, h2_ref, lg_ref):
    z = _silu(_layernorm(conv, clg_ref[...], clb_ref[...])).astype(BF16)
    m = _dot(z, wpw2_ref[...]) + bpw2_ref[...]
    y3 = _layernorm(alpha * y + (1.0 + gm_ref[...]) * m, lng_ref[0:1, :], lnb_ref[0:1, :])
    y3_ref[...] = y3
    h2 = y3 * (1.0 + scf_ref[...]) + shf_ref[...]
    _rows_to_tiles(h2_ref, h2)
    h2_hi = h2.astype(BF16)
    h2_lo = (h2 - h2_hi.astype(F32)).astype(BF16)
    lg_ref[...] = (_dot(h2_hi, wr_ref[0]) + _dot(h2_lo, wr_ref[0]) + _dot(h2_hi, wr_ref[1])) + br_ref[...]


def _conv_prompt_kernel(alpha, y_ref, shm_ref, scm_ref, gm_ref, shf_ref, scf_ref, lng_ref, lnb_ref,
                        wpw1_ref, bpw1_ref, wdw_ref, bdw_ref, clg_ref, clb_ref, wpw2_ref, bpw2_ref,
                        wr_ref, br_ref, y3_ref, h2_ref, lg_ref, tail_ref, ext_ref, shift_ref, conv_ref, wrep_ref):
    tm = y_ref.shape[0]
    d = D_MODEL
    rows_ext = tm + HIST_PAD

    @pl.when(pl.program_id(1) == 0)
    def _():
        ext_ref[0:HIST_PAD, :] = jnp.zeros((HIST_PAD, d), F32)

    y = y_ref[...]
    u = _glu(y, shm_ref, scm_ref, wpw1_ref, bpw1_ref)
    ext_ref[HIST_PAD:rows_ext, :] = u
    tail_ref[...] = u[tm - HIST_PAD:tm, :]
    span = rows_ext - SUBLANES
    for b in range(1, SUBLANES):
        shift_ref[b - 1, :, :] = ext_ref[b:b + span, :]
    base = HIST_PAD - (CONV_WIDTH - 1)
    rb = 4 * SUBLANES
    for k in range(CONV_WIDTH):
        wrep_ref[k] = jnp.broadcast_to(wdw_ref[k:k + 1, :], (SUBLANES, d))
    wrep_ref[CONV_WIDTH] = jnp.broadcast_to(bdw_ref[...], (SUBLANES, d))

    def body(r, carry):
        r0 = pl.multiple_of(r * rb, rb)
        acc = jnp.broadcast_to(wrep_ref[CONV_WIDTH], (rb // SUBLANES, SUBLANES, d))
        for k in range(CONV_WIDTH):
            off = base + k
            start = r0 + (off // SUBLANES) * SUBLANES
            if off % SUBLANES == 0:
                x = ext_ref[pl.ds(start, rb), :]
            else:
                x = shift_ref[off % SUBLANES - 1, pl.ds(start, rb), :]
            acc = acc + wrep_ref[k] * x.reshape(rb // SUBLANES, SUBLANES, d)
        conv_ref[pl.ds(r0, rb), :] = acc.reshape(rb, d)
        return carry

    lax.fori_loop(0, tm // rb, body, 0)
    ext_ref[0:HIST_PAD, :] = u[tm - HIST_PAD:tm, :]
    _conv_tail(alpha, y, conv_ref[...], gm_ref, shf_ref, scf_ref, lng_ref, lnb_ref, clg_ref, clb_ref,
               wpw2_ref, bpw2_ref, wr_ref, br_ref, y3_ref, h2_ref, lg_ref)


def _conv_sample_kernel(alpha, y_ref, shm_ref, scm_ref, gm_ref, shf_ref, scf_ref, lng_ref, lnb_ref,
                        wpw1_ref, bpw1_ref, wdw_ref, bdw_ref, clg_ref, clb_ref, wpw2_ref, bpw2_ref,
                        wr_ref, br_ref, hist_ref, y3_ref, h2_ref, lg_ref, u_ref):
    nh = CONV_WIDTH - 1
    y = y_ref[...]
    u = _glu(y, shm_ref, scm_ref, wpw1_ref, bpw1_ref)
    u_ref[...] = u
    w = wdw_ref[...]
    conv = bdw_ref[...] + w[nh:nh + 1, :] * u
    for k in range(nh):
        conv = conv + w[k:k + 1, :] * hist_ref[k]
    _conv_tail(alpha, y, conv, gm_ref, shf_ref, scf_ref, lng_ref, lnb_ref, clg_ref, clb_ref,
               wpw2_ref, bpw2_ref, wr_ref, br_ref, y3_ref, h2_ref, lg_ref)


def _conv_weight_specs(d):
    return [_resident((2, d)), _resident((2, d)), _resident((d, 2 * d)), _resident((1, 2 * d)),
            _resident((CONV_WIDTH, d)), _resident((1, d)), _resident((1, d)), _resident((1, d)),
            _resident((d, d)), _resident((1, d)), _resident((2, d, LANES)), _resident((1, LANES))]


def _conv_prompt(y, mods5, weights, n_seq, seq_len, alpha):
    d = D_MODEL
    tm = TM_CONV
    tps = seq_len // tm
    t = n_seq * seq_len
    row = pl.BlockSpec((tm, d), lambda b, i: (b * tps + i, 0))
    trow = pl.BlockSpec((tm * ROW_TILES, LANES), lambda b, i: (b * tps + i, 0))
    lgrow = pl.BlockSpec((tm, LANES), lambda b, i: (b * tps + i, 0))
    tail = pl.BlockSpec((None, HIST_PAD, d), lambda b, i: (b, 0, 0))
    return pl.pallas_call(
        functools.partial(_conv_prompt_kernel, alpha),
        grid=(n_seq, tps),
        in_specs=[row] + [m[1] for m in mods5] + _conv_weight_specs(d),
        out_specs=[row, trow, lgrow, tail],
        out_shape=[jax.ShapeDtypeStruct((t, d), F32),
                   jax.ShapeDtypeStruct((t * ROW_TILES, LANES), F32),
                   jax.ShapeDtypeStruct((t, LANES), F32),
                   jax.ShapeDtypeStruct((n_seq, HIST_PAD, d), F32)],
        scratch_shapes=[pltpu.VMEM((tm + HIST_PAD, d), F32),
                        pltpu.VMEM((SUBLANES - 1, tm + HIST_PAD - SUBLANES, d), F32),
                        pltpu.VMEM((tm, d), F32),
                        pltpu.VMEM((CONV_WIDTH + 1, SUBLANES, d), F32)],
        compiler_params=_params(2),
        name="conv_prompt",
    )(y, *[m[0] for m in mods5], *weights)


def _conv_sample(y, mods5, weights, hist, alpha):
    d = D_MODEL
    n = y.shape[0]
    tm = 32
    row = pl.BlockSpec((tm, d), lambda i: (i, 0))
    return pl.pallas_call(
        functools.partial(_conv_sample_kernel, alpha),
        grid=(n // tm,),
        in_specs=[row] + [m[1] for m in mods5] + _conv_weight_specs(d)
        + [pl.BlockSpec((CONV_WIDTH - 1, tm, d), lambda i: (0, i, 0))],
        out_specs=[row, pl.BlockSpec((tm * ROW_TILES, LANES), lambda i: (i, 0)),
                   pl.BlockSpec((tm, LANES), lambda i: (i, 0)), row],
        out_shape=[jax.ShapeDtypeStruct((n, d), F32),
                   jax.ShapeDtypeStruct((n * ROW_TILES, LANES), F32),
                   jax.ShapeDtypeStruct((n, LANES), F32),
                   jax.ShapeDtypeStruct((n, d), F32)],
        compiler_params=_params(1),
        name="conv_sample",
    )(y, *[m[0] for m in mods5], *weights, hist)


def _route_tile(lg_ref, meta_ref, meta_t_ref, carry_ref):
    tm = lg_ref.shape[0]
    lane = lax.broadcasted_iota(jnp.int32, (tm, LANES), 1)
    lg = jnp.where(lane < N_EXPERTS, lg_ref[...], NEG_INF)
    m1 = jnp.max(lg, axis=1, keepdims=True)
    i1 = jnp.min(jnp.where(lg == m1, lane, LANES), axis=1, keepdims=True)
    lg2 = jnp.where(lane == i1, NEG_INF, lg)
    m2 = jnp.max(lg2, axis=1, keepdims=True)
    i2 = jnp.min(jnp.where(lg2 == m2, lane, LANES), axis=1, keepdims=True)
    e = jnp.exp(m2 - m1)
    g1 = 1.0 / (1.0 + e)
    g2 = e / (1.0 + e)
    hot1 = lane == i1
    hot2 = lane == i2
    onehot = (hot1 | hot2).astype(BF16)
    ri = lax.broadcasted_iota(jnp.int32, (tm, tm), 0)
    ci = lax.broadcasted_iota(jnp.int32, (tm, tm), 1)
    before = (ci < ri).astype(BF16)
    rank = _dot(before, onehot) + carry_ref[...]
    r1 = jnp.sum(jnp.where(hot1, rank, 0.0), axis=1, keepdims=True)
    r2 = jnp.sum(jnp.where(hot2, rank, 0.0), axis=1, keepdims=True)
    cols = (i1.astype(F32), i2.astype(F32), g1, g2, r1, r2)
    meta = jnp.zeros((tm, LANES), F32)
    for c, val in enumerate(cols):
        meta = jnp.where(lane == c, val, meta)
    meta_ref[...] = meta
    meta_t_ref[...] = jnp.transpose(meta)[0:SUBLANES, :]
    carry_ref[...] += jnp.sum(onehot.astype(F32), axis=0, keepdims=True)


def _route_kernel(n_prompt_tiles, lgp_ref, lgs_ref, mp_ref, mpt_ref, ms_ref, mst_ref, cnt_ref, carry_ref):
    i = pl.program_id(0)

    @pl.when(i == 0)
    def _():
        carry_ref[...] = jnp.zeros_like(carry_ref)

    @pl.when(i < n_prompt_tiles)
    def _():
        _route_tile(lgp_ref, mp_ref, mpt_ref, carry_ref)

    @pl.when(i == n_prompt_tiles)
    def _():
        _route_tile(lgs_ref, ms_ref, mst_ref, carry_ref)
        cnt_ref[...] = carry_ref[...]


def _route(lg_p, lg_s):
    tp, ts = lg_p.shape[0], lg_s.shape[0]
    tm = TM_ROUTE
    npt = tp // tm
    last = npt - 1
    return pl.pallas_call(
        functools.partial(_route_kernel, npt),
        grid=(npt + 1,),
        in_specs=[pl.BlockSpec((tm, LANES), lambda i: (jnp.minimum(i, last), 0)),
                  pl.BlockSpec((ts, LANES), lambda i: (0, 0))],
        out_specs=[pl.BlockSpec((tm, LANES), lambda i: (jnp.minimum(i, last), 0)),
                   pl.BlockSpec((SUBLANES, tm), lambda i: (0, jnp.minimum(i, last))),
                   pl.BlockSpec((ts, LANES), lambda i: (0, 0)),
                   pl.BlockSpec((SUBLANES, ts), lambda i: (0, 0)),
                   pl.BlockSpec((1, LANES), lambda i: (0, 0))],
        out_shape=[jax.ShapeDtypeStruct((tp, LANES), F32),
                   jax.ShapeDtypeStruct((SUBLANES, tp), F32),
                   jax.ShapeDtypeStruct((ts, LANES), F32),
                   jax.ShapeDtypeStruct((SUBLANES, ts), F32),
                   jax.ShapeDtypeStruct((1, LANES), F32)],
        scratch_shapes=[pltpu.VMEM((1, LANES), F32)],
        compiler_params=_params(1),
        name="moe_route",
    )(lg_p, lg_s)


def _row_copy(src_ref, src_row, dst_ref, dst_row, sem, n=1):
    return pltpu.make_async_copy(
        src_ref.at[pl.ds(pl.multiple_of(src_row * ROW_TILES, ROW_TILES), n * ROW_TILES), :],
        dst_ref.at[pl.ds(pl.multiple_of(dst_row * ROW_TILES, ROW_TILES), n * ROW_TILES), :], sem)


def _scatter_rows(slots_ref, h_ref, xs_ref, sem):
    tm = h_ref.shape[0] // ROW_TILES

    def issue(r, carry):
        for c in range(MOE_TOPK):
            _row_copy(h_ref, r, xs_ref, slots_ref[0, c * tm + r], sem).start()
        return carry

    lax.fori_loop(0, tm, issue, 0)
    for c in range(MOE_TOPK):
        _row_copy(h_ref, 0, xs_ref, 0, sem, n=tm).wait()


def _scatter_kernel(n_prompt_tiles, zlo_ref, zhi_ref, sp_ref, ss_ref, hp_ref, hs_ref, xs_ref,
                    zero_ref, sem, zsem):
    i = pl.program_id(0)

    @pl.when(i < n_prompt_tiles)
    def _():
        _scatter_rows(sp_ref, hp_ref, xs_ref, sem)

    @pl.when(i == n_prompt_tiles)
    def _():
        _scatter_rows(ss_ref, hs_ref, xs_ref, sem)
        zero_ref[...] = jnp.zeros_like(zero_ref)
        for e in range(N_EXPERTS):
            lo = zlo_ref[e]
            n = zhi_ref[e] - lo
            nbig = n // ZERO_ROWS

            def big(k, carry, lo=lo):
                return _row_copy(zero_ref, 0, xs_ref, lo + k * ZERO_ROWS, zsem, n=ZERO_ROWS)

            def small(r, carry, lo=lo):
                return _row_copy(zero_ref, 0, xs_ref, lo + r, zsem)

            lax.fori_loop(0, nbig, lambda k, c: (big(k, c).start(), c)[1], 0)
            lax.fori_loop(nbig * ZERO_ROWS, n, lambda r, c: (small(r, c).start(), c)[1], 0)
            lax.fori_loop(0, nbig, lambda k, c: (big(k, c).wait(), c)[1], 0)
            lax.fori_loop(nbig * ZERO_ROWS, n, lambda r, c: (small(r, c).wait(), c)[1], 0)


def _scatter(h2t_p, h2t_s, slots_p, slots_s, zero_lo, zero_hi, n_rows):
    tp = h2t_p.shape[0] // ROW_TILES
    ts = h2t_s.shape[0] // ROW_TILES
    tm = TM_ROUTE
    npt = tp // tm
    last = npt - 1
    grid_spec = pltpu.PrefetchScalarGridSpec(
        num_scalar_prefetch=2,
        grid=(npt + 1,),
        in_specs=[pl.BlockSpec((None, 1, MOE_TOPK * tm), lambda i, lo, hi: (jnp.minimum(i, last), 0, 0),
                               memory_space=pltpu.SMEM),
                  pl.BlockSpec((None, 1, MOE_TOPK * ts), lambda i, lo, hi: (0, 0, 0), memory_space=pltpu.SMEM),
                  pl.BlockSpec((tm * ROW_TILES, LANES), lambda i, lo, hi: (jnp.minimum(i, last), 0)),
                  pl.BlockSpec((ts * ROW_TILES, LANES), lambda i, lo, hi: (0, 0))],
        out_specs=pl.BlockSpec(memory_space=pl.ANY),
        scratch_shapes=[pltpu.VMEM((ZERO_ROWS * ROW_TILES, LANES), F32), pltpu.SemaphoreType.DMA(()),
                        pltpu.SemaphoreType.DMA(())],
    )
    return pl.pallas_call(
        functools.partial(_scatter_kernel, npt),
        grid_spec=grid_spec,
        out_shape=jax.ShapeDtypeStruct((n_rows * ROW_TILES, LANES), F32),
        compiler_params=pltpu.CompilerParams(dimension_semantics=("arbitrary",),
                                             vmem_limit_bytes=VMEM_LIMIT, has_side_effects=True),
        name="moe_scatter",
    )(zero_lo, zero_hi, slots_p, slots_s, h2t_p, h2t_s)


def _load_expert(e, w1_hbm, w3_hbm, w2_hbm, w1_ref, w3_ref, w2_ref, up_stage, down_stage, sem):
    tf = TF_GMM
    nck = w1_ref.shape[1] // tf
    jobs = []
    for k in range(nck):
        cols = slice(k * tf, (k + 1) * tf)
        for src, dst in ((w1_hbm, w1_ref), (w3_hbm, w3_ref)):
            slot = len(jobs) % 2
            jobs.append((pltpu.make_async_copy(src.at[e, :, cols], up_stage.at[slot], sem.at[slot]),
                         up_stage.at[slot], dst.at[:, cols]))
    for k in range(nck):
        rows = slice(k * tf, (k + 1) * tf)
        slot = len(jobs) % 2
        jobs.append((pltpu.make_async_copy(w2_hbm.at[e, rows, :], down_stage.at[slot], sem.at[slot]),
                     down_stage.at[slot], w2_ref.at[rows, :]))
    jobs[0][0].start()
    for k, (copy, stage, dst) in enumerate(jobs):
        if k + 1 < len(jobs):
            jobs[k + 1][0].start()
        copy.wait()
        dst[...] = stage[...].astype(BF16)


def _gmm_kernel(exp_ref, nv_ref, x_ref, w1_hbm, w3_hbm, w2_hbm, o_ref, g_ref, w1_ref, w3_ref, w2_ref,
                up_stage, down_stage, sem):
    tg = g_ref.shape[0]
    m = pl.program_id(0)
    live = m < nv_ref[0]
    e = exp_ref[m]

    @pl.when(live & ((m == 0) | (e != exp_ref[jnp.maximum(m - 1, 0)])))
    def _():
        _load_expert(e, w1_hbm, w3_hbm, w2_hbm, w1_ref, w3_ref, w2_ref, up_stage, down_stage, sem)

    @pl.when(live)
    def _():
        x = _tiles_to_rows(x_ref, tg).astype(BF16)
        _swiglu_hidden(x, w1_ref, w3_ref, g_ref, TF_GMM)
        _rows_to_tiles(o_ref, _dot(g_ref[...], w2_ref[...]))

    @pl.when(jnp.logical_not(live))
    def _():
        o_ref[...] = jnp.zeros_like(o_ref)


def _gmm(xs, w1, w3, w2, tile_exp, n_valid):
    d = D_MODEL
    tg = TM_GMM
    tf = TF_GMM
    dfe = w1.shape[2]
    n_tiles = tile_exp.shape[0]
    hbm = pl.BlockSpec(memory_space=pl.ANY)
    grid_spec = pltpu.PrefetchScalarGridSpec(
        num_scalar_prefetch=2,
        grid=(n_tiles,),
        in_specs=[pl.BlockSpec((tg * ROW_TILES, LANES), lambda m, ex, nv: (m, 0)), hbm, hbm, hbm],
        out_specs=pl.BlockSpec((tg * ROW_TILES, LANES), lambda m, ex, nv: (m, 0)),
        scratch_shapes=[pltpu.VMEM((tg, dfe), BF16), pltpu.VMEM((d, dfe), BF16), pltpu.VMEM((d, dfe), BF16),
                        pltpu.VMEM((dfe, d), BF16), pltpu.VMEM((2, d, tf), F32), pltpu.VMEM((2, tf, d), F32),
                        pltpu.SemaphoreType.DMA((2,))],
    )
    return pl.pallas_call(
        _gmm_kernel,
        grid_spec=grid_spec,
        out_shape=jax.ShapeDtypeStruct(xs.shape, F32),
        compiler_params=_params(1),
        name="moe_experts",
    )(tile_exp, n_valid, xs, w1, w3, w2)


def _combine_kernel(alpha, slots_ref, next_slots_ref, ys_ref, y_ref, meta_ref, gf_ref, lng_ref, lnb_ref,
                    o_ref, buf_ref, sem):
    tm = y_ref.shape[0]
    i = pl.program_id(0)
    cur = i % 2

    def gather(sl_ref, slot):
        def issue(r, carry):
            for c in range(MOE_TOPK):
                _row_copy(ys_ref, sl_ref[0, c * tm + r], buf_ref.at[slot, c], r, sem.at[slot]).start()
            return carry

        lax.fori_loop(0, tm, issue, 0)

    @pl.when(i == 0)
    def _():
        gather(slots_ref, 0)

    for slot in range(2):
        @pl.when((i + 1 < pl.num_programs(0)) & (cur != slot))
        def _(slot=slot):
            gather(next_slots_ref, slot)

    for c in range(MOE_TOPK):
        _row_copy(ys_ref, 0, buf_ref.at[cur, c], 0, sem.at[cur], n=tm).wait()
    meta = meta_ref[...]
    f = jnp.zeros((tm, D_MODEL), F32)
    for c in range(MOE_TOPK):
        f = f + _tiles_to_rows(buf_ref.at[cur, c], tm) * meta[:, META_GATE + c:META_GATE + c + 1]
    o_ref[...] = _layernorm(alpha * y_ref[...] + (1.0 + gf_ref[...]) * f,
                            lng_ref[1:2, :], lnb_ref[1:2, :])


def _combine(ys, y3, slots3, meta, gf, ln_g, ln_b, tm, alpha):
    t, d = y3.shape
    last = t // tm - 1

    def slot_spec(ahead):
        return pl.BlockSpec((None, 1, MOE_TOPK * tm), lambda i: (jnp.minimum(i + ahead, last), 0, 0),
                            memory_space=pltpu.SMEM)

    return pl.pallas_call(
        functools.partial(_combine_kernel, alpha),
        grid=(t // tm,),
        in_specs=[slot_spec(0), slot_spec(1),
                  pl.BlockSpec(memory_space=pl.ANY),
                  pl.BlockSpec((tm, d), lambda i: (i, 0)),
                  pl.BlockSpec((tm, LANES), lambda i: (i, 0)),
                  gf[1], _resident((2, d)), _resident((2, d))],
        out_specs=pl.BlockSpec((tm, d), lambda i: (i, 0)),
        out_shape=jax.ShapeDtypeStruct((t, d), F32),
        scratch_shapes=[pltpu.VMEM((2, MOE_TOPK, tm * ROW_TILES, LANES), F32), pltpu.SemaphoreType.DMA((2,))],
        compiler_params=_params(1),
        name="moe_combine",
    )(slots3, slots3, ys, y3, meta, gf[0], ln_g, ln_b)


def _tile_slots(slots, tm):
    k, t = slots.shape
    return slots.reshape(k, t // tm, tm).transpose(1, 0, 2).reshape(t // tm, 1, k * tm)


def kernel(x_prompt, x_sample, cache_k, cache_v, state_conv, page_table, c_prompt, c_sample,
           ada_w, ada_b, ln_g, ln_b, attn_w_qkv, attn_w_o,
           conv_w_pw1, conv_b_pw1, conv_w_dw, conv_b_dw, conv_ln_g, conv_ln_b, conv_w_pw2, conv_b_pw2,
           ffn_w1, ffn_w3, ffn_w2, moe_w_router, moe_b_router, moe_w1, moe_w3, moe_w2):
    b, s, d = x_prompt.shape
    db, ds, _ = x_sample.shape
    depth = ada_w.shape[0]
    assert d == D_MODEL and ds == 1 and depth == 2
    assert page_table.shape[1] * PAGE_SIZE % MOBA_BLOCK == 0
    alpha = (2 * depth) ** 0.25
    tp = b * s
    t_all = tp + db
    assert tp % TM_ROUTE == 0 and tp % TM_COMBINE == 0

    c_all = jnp.concatenate([c_sample, c_prompt], axis=0)
    mods = _adaln(c_all, ada_w, ada_b)
    mods = _Mods(mods.reshape(depth * 6, db + b, d), db)

    def pmod(layer, k, tm):
        tps = s // tm
        return mods.prompt(layer * 6 + k, lambda i: i // tps)

    def smod(layer, k, rows=None, block_of_step=lambda i: 0):
        return mods.sample(layer * 6 + k, db if rows is None else rows, block_of_step)

    xp = x_prompt.reshape(tp, d)
    xs_ = x_sample.reshape(db, d)

    wqkv = attn_w_qkv[0].astype(BF16)
    qp, kp, vp = _qkv(xp, pmod(0, 0, TM_QKV), pmod(0, 1, TM_QKV), wqkv, TM_QKV)
    qs, ks, vs = _qkv(xs_, smod(0, 0), smod(0, 1), wqkv, db)
    attn_p, kt_p, vt_p = _moba_prompt(qp.reshape(b, s, d), kp.reshape(b, s, d), vp.reshape(b, s, d))
    attn_p = attn_p.reshape(tp, d)
    attn_s = _moba_sample(qs, ks, vs, cache_k, cache_v, 0, page_table)

    assert ffn_w1.shape[2] % FFN_CHUNK == 0 and moe_w1.shape[3] % TF_GMM == 0
    wo = attn_w_o[0].astype(BF16)
    ffn_w = (ffn_w1[0].astype(BF16), ffn_w3[0].astype(BF16), ffn_w2[0].astype(BF16))
    y_p = _post_attn(attn_p, xp, pmod(0, 2, TM_POST), pmod(0, 3, TM_POST), pmod(0, 4, TM_POST),
                     pmod(0, 5, TM_POST), ln_g[0], ln_b[0], wo, *ffn_w, TM_POST, alpha)
    y_s = _post_attn(attn_s, xs_, smod(0, 2), smod(0, 3), smod(0, 4), smod(0, 5),
                     ln_g[0], ln_b[0], wo, *ffn_w, db, alpha)

    wr = jnp.pad(moe_w_router[0], ((0, 0), (0, LANES - N_EXPERTS)))
    wr_hi = wr.astype(BF16)
    wr = jnp.stack([wr_hi, (wr - wr_hi.astype(F32)).astype(BF16)])
    br = jnp.pad(moe_b_router[0], (0, LANES - N_EXPERTS)).reshape(1, LANES)
    conv_weights = (ln_g[1], ln_b[1], conv_w_pw1[0].astype(BF16), conv_b_pw1[0].reshape(1, 2 * d),
                    conv_w_dw[0], conv_b_dw[0].reshape(1, d), conv_ln_g[0].reshape(1, d),
                    conv_ln_b[0].reshape(1, d), conv_w_pw2[0].astype(BF16), conv_b_pw2[0].reshape(1, d),
                    wr, br)
    pm = [mods.prompt(6 + k, lambda bi, i: bi) for k in range(5)]
    y3_p, h2t_p, lg_p, tail = _conv_prompt(y_p, pm, conv_weights, b, s, alpha)
    sm = [smod(1, k, 32, lambda i: i) for k in range(5)]
    hist_t = jnp.transpose(state_conv[0], (1, 0, 2))
    y3_s, h2t_s, lg_s, u_s = _conv_sample(y_s, sm, conv_weights, hist_t, alpha)

    meta_p, meta_pt, meta_s, meta_st, cnt = _route(lg_p, lg_s)
    counts = cnt[0, 0:N_EXPERTS].astype(jnp.int32)
    padded = (counts + TM_GMM - 1) // TM_GMM * TM_GMM
    pad_end = jnp.cumsum(padded)
    pad_start = pad_end - padded
    n_tiles = -(-t_all * MOE_TOPK // TM_GMM) + N_EXPERTS
    n_rows = n_tiles * TM_GMM

    def slots_of(meta_t):
        e = meta_t[META_EXPERT:META_EXPERT + MOE_TOPK].astype(jnp.int32)
        slot = meta_t[META_RANK:META_RANK + MOE_TOPK].astype(jnp.int32)
        for x in range(N_EXPERTS):
            slot = slot + jnp.where(e == x, pad_start[x], 0)
        return slot

    slots_p, slots_s = slots_of(meta_pt), slots_of(meta_st)
    zero_lo = pad_start + counts
    zero_hi = jnp.concatenate([pad_start[1:], jnp.full((1,), n_rows, jnp.int32)])
    xs_buf = _scatter(h2t_p, h2t_s, _tile_slots(slots_p, TM_ROUTE), _tile_slots(slots_s, db),
                      zero_lo.astype(jnp.int32), zero_hi.astype(jnp.int32), n_rows)

    tile_ids = jnp.arange(n_tiles, dtype=jnp.int32)
    tile_exp = jnp.minimum(jnp.sum((pad_end // TM_GMM)[None, :] <= tile_ids[:, None], axis=1),
                           N_EXPERTS - 1).astype(jnp.int32)
    n_valid = (pad_end[-1] // TM_GMM).reshape(1).astype(jnp.int32)
    ys_buf = _gmm(xs_buf, moe_w1[0], moe_w3[0], moe_w2[0], tile_exp, n_valid)

    out_p = _combine(ys_buf, y3_p, _tile_slots(slots_p, TM_COMBINE), meta_p,
                     pmod(1, 5, TM_COMBINE), ln_g[1], ln_b[1], TM_COMBINE, alpha)
    out_s = _combine(ys_buf, y3_s, _tile_slots(slots_s, db), meta_s,
                     smod(1, 5), ln_g[1], ln_b[1], db, alpha)

    nh = CONV_WIDTH - 1
    new_conv_p = tail[:, HIST_PAD - nh:, :][None]
    new_conv_s = jnp.transpose(jnp.concatenate([hist_t[1:], u_s[None]], axis=0), (1, 0, 2))[None]
    hshape = (N_HEADS, HEAD_DIM)

    def seq_major(t):
        return jnp.transpose(t.reshape(1, b, *hshape, s), (0, 1, 4, 2, 3))

    return (out_p.reshape(b, s, d), out_s.reshape(db, 1, d), seq_major(kt_p), seq_major(vt_p),
            ks.reshape(1, db, 1, *hshape), vs.reshape(1, db, 1, *hshape),
            new_conv_p, new_conv_s)
```
